```python
import math
import jax, jax.numpy as jnp
from jax import lax
import numpy as np

D_MODEL = 4096
BATCH = 2
SEQ = 8192
DEPTH = 1

NSA_HEADS = 32
NSA_KV_GROUPS = 4
NSA_HEAD_DIM = 128
NSA_CMP_BLOCK = 32
NSA_CMP_STRIDE = 16
NSA_CMP_HIDDEN = 256
NSA_SEL_BLOCK = 64
NSA_SEL_TOPN = 16
NSA_WINDOW = 512
NSA_Q_BLOCK = 64
ROPE_THETA = 500000.0
ROPE_DIM = NSA_HEAD_DIM // 4
FORCE_BONUS = 1.0e4
NEG_INF = -1.0e30

SSD_D_INNER = D_MODEL
SSD_HEAD_DIM = 64
SSD_HEADS = SSD_D_INNER // SSD_HEAD_DIM
SSD_GROUPS = 8
SSD_STATE = 128
SSD_CONV = 4
SSD_CHUNK = 256
SSD_CONV_CH = SSD_D_INNER + 2 * SSD_GROUPS * SSD_STATE

FFN_DIM = 256 * ((8 * D_MODEL // 3 + 255) // 256)
N_ADA = 9
ADA_SCALE = 0.5
LN_EPS = 1e-5
RMS_EPS = 1e-5
DEEPNORM_ALPHA = (2 * DEPTH) ** 0.25
DEEPNORM_BETA = (8 * DEPTH) ** -0.25
POS_OFFSET_MAX = 4096

NSA_Q_WIDTH = NSA_HEADS * NSA_HEAD_DIM
NSA_KV_WIDTH = NSA_KV_GROUPS * NSA_HEAD_DIM
IN_SPLIT_SIZES = (NSA_Q_WIDTH,) + (NSA_KV_WIDTH,) * 6 + (3 * NSA_HEADS, SSD_D_INNER, SSD_CONV_CH, SSD_HEADS, D_MODEL, D_MODEL)
IN_PROJ_DIM = sum(IN_SPLIT_SIZES)

kernel_name = "hybrid_nsa_ssd_macaron_deepnorm_adaln"


def layer_norm(x, g, b):
    xf = x.astype(jnp.float32)
    mu = jnp.mean(xf, axis=-1, keepdims=True)
    var = jnp.mean(jnp.square(xf - mu), axis=-1, keepdims=True)
    return ((xf - mu) * lax.rsqrt(var + LN_EPS) * g.astype(jnp.float32) + b.astype(jnp.float32)).astype(x.dtype)


def swiglu(h, w_gate, w_up, w_down):
    return (jax.nn.silu(h @ w_gate) * (h @ w_up)) @ w_down


def modulate(x, shift, scale):
    return x * (1.0 + scale[:, None, :]) + shift[:, None, :]


def rope_tables(pos):
    half = ROPE_DIM // 2
    inv_freq = jnp.float32(ROPE_THETA) ** (-jnp.arange(half, dtype=jnp.float32) / half)
    ang = pos.astype(jnp.float32)[..., None] * inv_freq
    return jnp.cos(ang), jnp.sin(ang)


def apply_partial_rope(t, cos, sin):
    half = ROPE_DIM // 2
    cs = cos[:, :, None, :].astype(t.dtype)
    sn = sin[:, :, None, :].astype(t.dtype)
    t1 = t[..., :half]
    t2 = t[..., half:ROPE_DIM]
    return jnp.concatenate([t1 * cs - t2 * sn, t2 * cs + t1 * sn, t[..., ROPE_DIM:]], axis=-1)


def masked_softmax(s, mask):
    p = jax.nn.softmax(jnp.where(mask, s, NEG_INF), axis=-1)
    return jnp.where(mask, p, 0.0)


def nsa_compress(kv, pos_emb, w1, w2):
    B, S, G, dh = kv.shape
    n_cmp = (S - NSA_CMP_BLOCK) // NSA_CMP_STRIDE + 1
    idx = NSA_CMP_STRIDE * np.arange(n_cmp)[:, None] + np.arange(NSA_CMP_BLOCK)[None, :]
    blocks = kv[:, idx] + pos_emb[None, None, :, None, :]
    blocks = jnp.swapaxes(blocks, 2, 3).reshape(B, n_cmp, G, NSA_CMP_BLOCK * dh)
    return jax.nn.silu(blocks @ w1) @ w2


def nsa_attention(q, k_c, v_c, k_s, v_s, k_w, v_w, g_c, g_s, g_w):
    B, S, H, dh = q.shape
    G = NSA_KV_GROUPS
    hpg = H // G
    n_cmp = k_c.shape[1]
    n_sel = S // NSA_SEL_BLOCK
    top_n = min(NSA_SEL_TOPN, n_sel)
    Tq = NSA_Q_BLOCK
    n_qb = S // Tq
    W = NSA_WINDOW
    Bl = NSA_SEL_BLOCK
    scale = NSA_HEAD_DIM ** -0.5
    f32 = jnp.float32

    qg = q.reshape(B, S, G, hpg, dh)
    gc = g_c.reshape(B, S, G, hpg, 1).astype(q.dtype)
    gs = g_s.reshape(B, S, G, hpg, 1).astype(q.dtype)
    gw = g_w.reshape(B, S, G, hpg, 1).astype(q.dtype)

    c_start = jnp.arange(n_cmp) * NSA_CMP_STRIDE
    c_end = c_start + NSA_CMP_BLOCK - 1
    sel_start = jnp.arange(n_sel) * Bl
    overlap = ((c_start[:, None] < sel_start[None, :] + Bl) & (c_end[:, None] >= sel_start[None, :])).astype(f32)
    j_idx = jnp.arange(n_sel)

    ks_blocks = k_s.reshape(B, n_sel, Bl, G, dh).transpose(0, 3, 1, 2, 4)
    vs_blocks = v_s.reshape(B, n_sel, Bl, G, dh).transpose(0, 3, 1, 2, 4)
    pad = ((0, 0), (W, 0), (0, 0), (0, 0))
    kw_pad = jnp.pad(k_w, pad)
    vw_pad = jnp.pad(v_w, pad)
    b_ix = jnp.arange(B)[:, None, None, None]
    g_ix = jnp.arange(G)[None, :, None, None]

    def block_fn(i):
        t0 = i * Tq
        t = t0 + jnp.arange(Tq)
        qb = lax.dynamic_slice_in_dim(qg, t0, Tq, axis=1)

        s_c = jnp.einsum('bqghd,bcgd->bghqc', qb, k_c, preferred_element_type=f32) * scale
        m_c = c_end[None, :] <= t[:, None]
        p_c = masked_softmax(s_c, m_c)
        o_c = jnp.einsum('bghqc,bcgd->bqghd', p_c.astype(v_c.dtype), v_c)

        imp = jnp.einsum('bghqc,cj->bgqj', p_c, overlap)
        cur = t // Bl
        causal_blk = sel_start[None, :] <= t[:, None]
        forced = (j_idx[None, :] == 0) | (j_idx[None, :] == cur[:, None]) | (j_idx[None, :] == cur[:, None] - 1)
        score = jnp.where(causal_blk, imp + jnp.where(forced, FORCE_BONUS, 0.0), NEG_INF)
        _, sel = lax.top_k(score, top_n)
        k_sel = ks_blocks[b_ix, g_ix, sel].reshape(B, G, Tq, top_n * Bl, dh)
        v_sel = vs_blocks[b_ix, g_ix, sel].reshape(B, G, Tq, top_n * Bl, dh)
        key_pos = (sel[..., None] * Bl + jnp.arange(Bl)).reshape(B, G, Tq, top_n * Bl)
        m_s = (key_pos <= t[None, None, :, None])[:, :, None]
        s_s = jnp.einsum('bqghd,bgqkd->bghqk', qb, k_sel, preferred_element_type=f32) * scale
        p_s = masked_softmax(s_s, m_s)
        o_s = jnp.einsum('bghqk,bgqkd->bqghd', p_s.astype(v_sel.dtype), v_sel)

        kwb = lax.dynamic_slice_in_dim(kw_pad, t0, Tq + W, axis=1)
        vwb = lax.dynamic_slice_in_dim(vw_pad, t0, Tq + W, axis=1)
        kpos = t0 - W + jnp.arange(Tq + W)
        m_w = (kpos[None, :] <= t[:, None]) & (kpos[None, :] > t[:, None] - W) & (kpos[None, :] >= 0)
        s_w = jnp.einsum('bqghd,bkgd->bghqk', qb, kwb, preferred_element_type=f32) * scale
        p_w = masked_softmax(s_w, m_w)
        o_w = jnp.einsum('bghqk,bkgd->bqghd', p_w.astype(vwb.dtype), vwb)

        gcb = lax.dynamic_slice_in_dim(gc, t0, Tq, axis=1)
        gsb = lax.dynamic_slice_in_dim(gs, t0, Tq, axis=1)
        gwb = lax.dynamic_slice_in_dim(gw, t0, Tq, axis=1)
        o = gcb * o_c + gsb * o_s + gwb * o_w
        return o.reshape(B, Tq, H * dh)

    out = lax.map(block_fn, jnp.arange(n_qb))
    return jnp.swapaxes(out, 0, 1).reshape(B, S, H * dh)


def causal_depthwise_conv(x, w, b):
    C = x.shape[-1]
    out = lax.conv_general_dilated(x, w[:, None, :].astype(x.dtype), window_strides=(1,),
                                   padding=[(SSD_CONV - 1, 0)], dimension_numbers=('NWC', 'WIO', 'NWC'),
                                   feature_group_count=C)
    return out + b.astype(x.dtype)


def ssd_mixer(z, xbc, dt_raw, conv_w, conv_b, dt_bias, a_log, d_skip, norm_w):
    B, S, _ = z.shape
    G, N, P = SSD_GROUPS, SSD_STATE, SSD_HEAD_DIM
    hpg = SSD_HEADS // G
    f32 = jnp.float32
    xbc = jax.nn.silu(causal_depthwise_conv(xbc, conv_w, conv_b))
    xs = xbc[..., :SSD_D_INNER].reshape(B, S, G, hpg, P)
    b_in = xbc[..., SSD_D_INNER:SSD_D_INNER + G * N].reshape(B, S, G, N).astype(f32)
    c_in = xbc[..., SSD_D_INNER + G * N:].reshape(B, S, G, N).astype(f32)
    dt = jax.nn.softplus(dt_raw.astype(f32) + dt_bias.astype(f32))
    a = -jnp.exp(a_log.astype(f32))
    da = (dt * a).reshape(B, S, G, hpg)
    xdt = xs.astype(f32) * dt.reshape(B, S, G, hpg, 1)

    L = math.gcd(S, SSD_CHUNK)
    nc = S // L

    def to_chunks(t):
        return jnp.moveaxis(t.reshape((B, nc, L) + t.shape[2:]), 1, 0)

    causal = jnp.tril(jnp.ones((L, L), dtype=bool))[None, :, :, None, None]

    def step(state, inp):
        xc, dac, bc, cc = inp
        a_cum = jnp.cumsum(dac, axis=1)
        diff = a_cum[:, :, None] - a_cum[:, None, :]
        decay = jnp.exp(jnp.where(causal, diff, -jnp.inf))
        cb = jnp.einsum('btgn,bsgn->btsg', cc, bc)
        y_intra = jnp.einsum('btsg,btsgh,bsghp->btghp', cb, decay, xc)
        y_inter = jnp.einsum('btgn,bghpn->btghp', cc, state) * jnp.exp(a_cum)[..., None]
        decay_end = jnp.exp(a_cum[:, -1:] - a_cum)
        new_state = state * jnp.exp(a_cum[:, -1])[..., None, None] + jnp.einsum('bsgn,bsgh,bsghp->bghpn', bc, decay_end, xc)
        return new_state, y_intra + y_inter

    state0 = jnp.zeros((B, G, hpg, P, N), f32)
    _, ys = lax.scan(step, state0, (to_chunks(xdt), to_chunks(da), to_chunks(b_in), to_chunks(c_in)))
    y = jnp.moveaxis(ys, 0, 1).reshape(B, S, G, hpg, P)
    y = y + d_skip.astype(f32).reshape(G, hpg, 1) * xs.astype(f32)
    yg = (y.reshape(B, S, SSD_D_INNER) * jax.nn.silu(z.astype(f32))).reshape(B, S, G, -1)
    yg = yg * lax.rsqrt(jnp.mean(jnp.square(yg), axis=-1, keepdims=True) + RMS_EPS)
    return (yg.reshape(B, S, SSD_D_INNER) * norm_w.astype(f32)).astype(z.dtype)


def hybrid_mixer(h, positions, cos, sin, w_in, cmp_pos, cmp_k_w1, cmp_k_w2, cmp_v_w1, cmp_v_w2,
                 conv_w, conv_b, dt_bias, a_log, d_skip, norm_w, w_branch_a, w_branch_b, w_out):
    B, S, _ = h.shape
    H, G, dh = NSA_HEADS, NSA_KV_GROUPS, NSA_HEAD_DIM
    split_points = [int(v) for v in np.cumsum(IN_SPLIT_SIZES)[:-1]]
    proj = h @ w_in
    (q, k_cmp, v_cmp, k_slc, v_slc, k_win, v_win, g_nsa,
     z, xbc, dt_raw, gate_a, gate_b) = jnp.split(proj, split_points, axis=-1)

    q = apply_partial_rope(q.reshape(B, S, H, dh), cos, sin)
    k_slc = apply_partial_rope(k_slc.reshape(B, S, G, dh), cos, sin)
    k_win = apply_partial_rope(k_win.reshape(B, S, G, dh), cos, sin)
    v_slc = v_slc.reshape(B, S, G, dh)
    v_win = v_win.reshape(B, S, G, dh)
    k_c = nsa_compress(k_cmp.reshape(B, S, G, dh), cmp_pos, cmp_k_w1, cmp_k_w2)
    v_c = nsa_compress(v_cmp.reshape(B, S, G, dh), cmp_pos, cmp_v_w1, cmp_v_w2)
    n_cmp = k_c.shape[1]
    c_end_idx = NSA_CMP_STRIDE * np.arange(n_cmp) + NSA_CMP_BLOCK - 1
    cos_c, sin_c = rope_tables(positions[:, c_end_idx])
    k_c = apply_partial_rope(k_c, cos_c, sin_c)
    g3 = jax.nn.sigmoid(g_nsa.astype(jnp.float32)).reshape(B, S, 3, H)
    o_a = nsa_attention(q, k_c, v_c, k_slc, v_slc, k_win, v_win, g3[:, :, 0], g3[:, :, 1], g3[:, :, 2])

    o_b = ssd_mixer(z, xbc, dt_raw, conv_w, conv_b, dt_bias, a_log, d_skip, norm_w)

    y_a = o_a @ w_branch_a
    y_b = o_b @ w_branch_b
    merged = jax.nn.sigmoid(gate_a) * y_a + jax.nn.sigmoid(gate_b) * y_b
    return merged @ w_out


def setup_inputs(seed: int = 0) -> dict:
    key = jax.random.key(seed)
    ks = jax.random.split(key, 40)
    L, D, F = DEPTH, D_MODEL, FFN_DIM
    f32 = jnp.float32

    def nrm(k, shape, scale):
        return jax.random.normal(k, shape, f32) * scale

    x = nrm(ks[0], (BATCH, SEQ, D), 1.0)
    c = nrm(ks[1], (BATCH, D), 1.0)
    positions = (jnp.arange(SEQ, dtype=jnp.int32)[None, :]
                 + jax.random.randint(ks[2], (BATCH, 1), 0, POS_OFFSET_MAX, dtype=jnp.int32))
    w_ada = nrm(ks[3], (L, D, N_ADA * D), ADA_SCALE * D ** -0.5)
    b_ada = nrm(ks[4], (L, N_ADA * D), 0.02)
    ffn1_w_gate = nrm(ks[5], (L, D, F), D ** -0.5)
    ffn1_w_up = nrm(ks[6], (L, D, F), D ** -0.5)
    ffn1_w_down = nrm(ks[7], (L, F, D), DEEPNORM_BETA * F ** -0.5)
    w_in = nrm(ks[8], (L, D, IN_PROJ_DIM), D ** -0.5)
    nsa_cmp_pos = nrm(ks[9], (L, NSA_CMP_BLOCK, NSA_HEAD_DIM), 0.1)
    fan_c = NSA_CMP_BLOCK * NSA_HEAD_DIM
    nsa_cmp_k_w1 = nrm(ks[10], (L, fan_c, NSA_CMP_HIDDEN), fan_c ** -0.5)
    nsa_cmp_k_w2 = nrm(ks[11], (L, NSA_CMP_HIDDEN, NSA_HEAD_DIM), NSA_CMP_HIDDEN ** -0.5)
    nsa_cmp_v_w1 = nrm(ks[12], (L, fan_c, NSA_CMP_HIDDEN), fan_c ** -0.5)
    nsa_cmp_v_w2 = nrm(ks[13], (L, NSA_CMP_HIDDEN, NSA_HEAD_DIM), NSA_CMP_HIDDEN ** -0.5)
    ssd_conv_w = nrm(ks[14], (L, SSD_CONV, SSD_CONV_CH), SSD_CONV ** -0.5)
    ssd_conv_b = nrm(ks[15], (L, SSD_CONV_CH), 0.02)
    dt0 = jnp.exp(jax.random.uniform(ks[16], (L, SSD_HEADS), f32, math.log(1e-3), math.log(1e-1)))
    ssd_dt_bias = dt0 + jnp.log(-jnp.expm1(-dt0))
    ssd_a_log = jnp.log(jax.random.uniform(ks[17], (L, SSD_HEADS), f32, 1.0, 16.0))
    ssd_d = 1.0 + nrm(ks[18], (L, SSD_HEADS), 0.02)
    ssd_norm_w = 1.0 + nrm(ks[19], (L, SSD_D_INNER), 0.02)
    w_branch_a = nrm(ks[20], (L, NSA_Q_WIDTH, D), NSA_Q_WIDTH ** -0.5)
    w_branch_b = nrm(ks[21], (L, SSD_D_INNER, D), SSD_D_INNER ** -0.5)
    w_out = nrm(ks[22], (L, D, D), DEEPNORM_BETA * D ** -0.5)
    ffn2_w_gate = nrm(ks[23], (L, D, F), D ** -0.5)
    ffn2_w_up = nrm(ks[24], (L, D, F), D ** -0.5)
    ffn2_w_down = nrm(ks[25], (L, F, D), DEEPNORM_BETA * F ** -0.5)
    ln1_g = 1.0 + nrm(ks[26], (L, D), 0.02)
    ln1_b = nrm(ks[27], (L, D), 0.02)
    ln2_g = 1.0 + nrm(ks[28], (L, D), 0.02)
    ln2_b = nrm(ks[29], (L, D), 0.02)
    ln3_g = 1.0 + nrm(ks[30], (L, D), 0.02)
    ln3_b = nrm(ks[31], (L, D), 0.02)
    return {"x": x, "c": c, "positions": positions, "w_ada": w_ada, "b_ada": b_ada,
            "ffn1_w_gate": ffn1_w_gate, "ffn1_w_up": ffn1_w_up, "ffn1_w_down": ffn1_w_down,
            "w_in": w_in, "nsa_cmp_pos": nsa_cmp_pos, "nsa_cmp_k_w1": nsa_cmp_k_w1, "nsa_cmp_k_w2": nsa_cmp_k_w2,
            "nsa_cmp_v_w1": nsa_cmp_v_w1, "nsa_cmp_v_w2": nsa_cmp_v_w2,
            "ssd_conv_w": ssd_conv_w, "ssd_conv_b": ssd_conv_b, "ssd_dt_bias": ssd_dt_bias,
            "ssd_a_log": ssd_a_log, "ssd_d": ssd_d, "ssd_norm_w": ssd_norm_w,
            "w_branch_a": w_branch_a, "w_branch_b": w_branch_b, "w_out": w_out,
            "ffn2_w_gate": ffn2_w_gate, "ffn2_w_up": ffn2_w_up, "ffn2_w_down": ffn2_w_down,
            "ln1_g": ln1_g, "ln1_b": ln1_b, "ln2_g": ln2_g, "ln2_b": ln2_b, "ln3_g": ln3_g, "ln3_b": ln3_b}


def reference(x, c, positions, w_ada, b_ada, ffn1_w_gate, ffn1_w_up, ffn1_w_down, w_in, nsa_cmp_pos,
              nsa_cmp_k_w1, nsa_cmp_k_w2, nsa_cmp_v_w1, nsa_cmp_v_w2, ssd_conv_w, ssd_conv_b, ssd_dt_bias,
              ssd_a_log, ssd_d, ssd_norm_w, w_branch_a, w_branch_b, w_out, ffn2_w_gate, ffn2_w_up, ffn2_w_down,
              ln1_g, ln1_b, ln2_g, ln2_b, ln3_g, ln3_b):
    B = x.shape[0]
    cos, sin = rope_tables(positions)
    c_act = jax.nn.silu(c)
    for l in range(DEPTH):
        mod = (c_act @ w_ada[l] + b_ada[l]).reshape(B, N_ADA, D_MODEL)
        h = modulate(x, mod[:, 0], mod[:, 1])
        y = swiglu(h, ffn1_w_gate[l], ffn1_w_up[l], ffn1_w_down[l])
        x = layer_norm(DEEPNORM_ALPHA * x + 0.5 * mod[:, 2, None, :] * y, ln1_g[l], ln1_b[l])
        h = modulate(x, mod[:, 3], mod[:, 4])
        y = hybrid_mixer(h, positions, cos, sin, w_in[l], nsa_cmp_pos[l], nsa_cmp_k_w1[l], nsa_cmp_k_w2[l],
                         nsa_cmp_v_w1[l], nsa_cmp_v_w2[l], ssd_conv_w[l], ssd_conv_b[l], ssd_dt_bias[l],
                         ssd_a_log[l], ssd_d[l], ssd_norm_w[l], w_branch_a[l], w_branch_b[l], w_out[l])
        x = layer_norm(DEEPNORM_ALPHA * x + mod[:, 5, None, :] * y, ln2_g[l], ln2_b[l])
        h = modulate(x, mod[:, 6], mod[:, 7])
        y = swiglu(h, ffn2_w_gate[l], ffn2_w_up[l], ffn2_w_down[l])
        x = layer_norm(DEEPNORM_ALPHA * x + 0.5 * mod[:, 8, None, :] * y, ln3_g[l], ln3_b[l])
    return x
```

```python
import functools
import math

import jax
import jax.numpy as jnp
import numpy as np
from jax import lax
from jax.experimental import pallas as pl
from jax.experimental.pallas import tpu as pltpu

D_MODEL = 4096
BATCH = 2
SEQ = 8192
DEPTH = 1

NSA_HEADS = 32
NSA_KV_GROUPS = 4
NSA_HEAD_DIM = 128
NSA_CMP_BLOCK = 32
NSA_CMP_STRIDE = 16
NSA_CMP_HIDDEN = 256
NSA_SEL_BLOCK = 64
NSA_SEL_TOPN = 16
NSA_WINDOW = 512
ROPE_THETA = 500000.0
ROPE_DIM = NSA_HEAD_DIM // 4
FORCE_BONUS = 1.0e4
NEG_INF = -1.0e30

SSD_D_INNER = D_MODEL
SSD_HEAD_DIM = 64
SSD_HEADS = SSD_D_INNER // SSD_HEAD_DIM
SSD_GROUPS = 8
SSD_STATE = 128
SSD_CONV = 4
SSD_CHUNK = 256
SSD_CONV_CH = SSD_D_INNER + 2 * SSD_GROUPS * SSD_STATE

FFN_DIM = 256 * ((8 * D_MODEL // 3 + 255) // 256)
N_ADA = 9
LN_EPS = 1e-5
RMS_EPS = 1e-5
DEEPNORM_ALPHA = (2 * DEPTH) ** 0.25

NSA_Q_WIDTH = NSA_HEADS * NSA_HEAD_DIM
NSA_KV_WIDTH = NSA_KV_GROUPS * NSA_HEAD_DIM
IN_SPLIT_SIZES = (NSA_Q_WIDTH,) + (NSA_KV_WIDTH,) * 6 + (3 * NSA_HEADS, SSD_D_INNER, SSD_CONV_CH, SSD_HEADS, D_MODEL, D_MODEL)

F32 = jnp.float32
BF16 = jnp.bfloat16
LANES = 128
SUBLANES = 8
LOG2E = 1.4426950408889634
MIB = 1024 * 1024

TM_MATMUL = 1024
TM_DUAL = 512
TM_ROWWISE = 256
ATTN_TQ = 128
ATTN_TK = 512
FFN_PAD = 512
FFN_KSPLIT = 4


def _tile(pref, dim):
    t = min(pref, dim)
    assert dim % t == 0, (pref, dim)
    return t


def _params(sem, vmem_mib):
    return pltpu.CompilerParams(dimension_semantics=sem, vmem_limit_bytes=vmem_mib * MIB)


def _sigmoid(x):
    return 1.0 / (1.0 + jnp.exp(-x))


def _silu(x):
    return x * _sigmoid(x)


def _nt_dot(a, b):
    return lax.dot_general(a, b, (((1,), (1,)), ((), ())), preferred_element_type=F32)


def _ada_body(c_ref, w_ref, b_ref, o_ref):
    c = c_ref[...]
    ca = _silu(c).astype(BF16)
    o_ref[...] = jnp.dot(ca, w_ref[...].astype(BF16), preferred_element_type=F32) + b_ref[...]


def ada_proj(c, w_ada, b_ada):
    B, D = c.shape
    N = w_ada.shape[1]
    tn = _tile(512, N)
    c8 = jnp.zeros((SUBLANES, D), F32).at[:B].set(c)
    out = pl.pallas_call(
        _ada_body,
        grid=(N // tn,),
        in_specs=[pl.BlockSpec((SUBLANES, D), lambda j: (0, 0)),
                  pl.BlockSpec((D, tn), lambda j: (0, j)),
                  pl.BlockSpec((1, tn), lambda j: (0, j))],
        out_specs=pl.BlockSpec((SUBLANES, tn), lambda j: (0, j)),
        out_shape=jax.ShapeDtypeStruct((SUBLANES, N), F32),
        compiler_params=_params(("arbitrary",), 40),
        name="ada_proj",
    )(c8, w_ada, b_ada.reshape(1, N))
    return out[:B]


def _mod_body(x_ref, shift_ref, scale_ref, h_ref):
    h_ref[...] = (x_ref[...] * (1.0 + scale_ref[...]) + shift_ref[...]).astype(h_ref.dtype)


def _ln_mod_body(r_ref, g_ref, b_ref, shift_ref, scale_ref, x_ref, h_ref):
    r = r_ref[...]
    mu = jnp.mean(r, axis=-1, keepdims=True)
    d = r - mu
    var = jnp.mean(d * d, axis=-1, keepdims=True)
    x = d * lax.rsqrt(var + LN_EPS) * g_ref[...] + b_ref[...]
    x_ref[...] = x
    h_ref[...] = (x * (1.0 + scale_ref[...]) + shift_ref[...]).astype(h_ref.dtype)


def _ln_body(r_ref, g_ref, b_ref, x_ref):
    r = r_ref[...]
    mu = jnp.mean(r, axis=-1, keepdims=True)
    d = r - mu
    var = jnp.mean(d * d, axis=-1, keepdims=True)
    x_ref[...] = d * lax.rsqrt(var + LN_EPS) * g_ref[...] + b_ref[...]


def _mod_spec(D, tiles_per_batch, slot):
    return pl.BlockSpec((None, 1, D), lambda i: ((i // tiles_per_batch) * N_ADA + slot, 0, 0))


def modulate(x, mod, seq, slot_shift, slot_scale):
    T, D = x.shape
    tm = _tile(TM_ROWWISE, seq)
    tpb = seq // tm
    return pl.pallas_call(
        _mod_body,
        grid=(T // tm,),
        in_specs=[pl.BlockSpec((tm, D), lambda i: (i, 0)),
                  _mod_spec(D, tpb, slot_shift), _mod_spec(D, tpb, slot_scale)],
        out_specs=pl.BlockSpec((tm, D), lambda i: (i, 0)),
        out_shape=jax.ShapeDtypeStruct((T, D), BF16),
        compiler_params=_params(("parallel",), 32),
        name="modulate",
    )(x, mod, mod)


def ln_mod(r, g, b, mod, seq, slot_shift, slot_scale):
    T, D = r.shape
    tm = _tile(TM_ROWWISE, seq)
    tpb = seq // tm
    row = pl.BlockSpec((tm, D), lambda i: (i, 0))
    vec = pl.BlockSpec((1, D), lambda i: (0, 0))
    return pl.pallas_call(
        _ln_mod_body,
        grid=(T // tm,),
        in_specs=[row, vec, vec, _mod_spec(D, tpb, slot_shift), _mod_spec(D, tpb, slot_scale)],
        out_specs=[row, row],
        out_shape=[jax.ShapeDtypeStruct((T, D), F32), jax.ShapeDtypeStruct((T, D), BF16)],
        compiler_params=_params(("parallel",), 40),
        name="ln_mod",
    )(r, g.reshape(1, D), b.reshape(1, D), mod, mod)


def layer_norm(r, g, b):
    T, D = r.shape
    tm = _tile(TM_ROWWISE, T)
    row = pl.BlockSpec((tm, D), lambda i: (i, 0))
    vec = pl.BlockSpec((1, D), lambda i: (0, 0))
    return pl.pallas_call(
        _ln_body,
        grid=(T // tm,),
        in_specs=[row, vec, vec],
        out_specs=row,
        out_shape=jax.ShapeDtypeStruct((T, D), F32),
        compiler_params=_params(("parallel",), 32),
        name="layer_norm",
    )(r, g.reshape(1, D), b.reshape(1, D))


def _fused_matmul_body(nx, nw, ne, pairs, epilogue, *refs):
    x_refs = refs[:nx]
    w_refs = refs[nx:nx + nw]
    e_refs = refs[nx + nw:nx + nw + ne]
    o_refs = refs[nx + nw + ne:]
    accs = [jnp.dot(x_refs[a][...], w_refs[b][...], preferred_element_type=F32) for a, b in pairs]
    epilogue(accs, e_refs, o_refs)


def fused_matmul(xs, ws, pairs, epilogue, extras, extra_specs, out_shapes, out_specs, tm, tn, vmem_mib, name):
    M = xs[0].shape[0]
    N = ws[0].shape[1]
    in_specs = ([pl.BlockSpec((tm, x.shape[1]), lambda i, j: (i, 0)) for x in xs]
                + [pl.BlockSpec((w.shape[0], tn), lambda i, j: (0, j)) for w in ws]
                + list(extra_specs))
    body = functools.partial(_fused_matmul_body, len(xs), len(ws), len(extras), tuple(pairs), epilogue)
    return pl.pallas_call(
        body,
        grid=(M // tm, N // tn),
        in_specs=in_specs,
        out_specs=out_specs,
        out_shape=out_shapes,
        compiler_params=_params(("parallel", "arbitrary"), vmem_mib),
        name=name,
    )(*xs, *ws, *extras)


def _swiglu_epilogue(accs, e_refs, o_refs):
    g, u = accs
    o_refs[0][...] = (_silu(g) * u).astype(o_refs[0].dtype)


def ffn_up(h, wg, wu):
    T, D = h.shape
    Fp = wg.shape[1]
    tm, tn = _tile(TM_MATMUL, T), _tile(FFN_PAD, Fp)
    return fused_matmul([h], [wg, wu], [(0, 0), (0, 1)], _swiglu_epilogue, [], [],
                        jax.ShapeDtypeStruct((T, Fp), BF16), pl.BlockSpec((tm, tn), lambda i, j: (i, j)),
                        tm, tn, 52, "ffn_up")


def _residual_epilogue(coef, accs, e_refs, o_refs):
    x_ref, gate_ref = e_refs
    o_refs[0][...] = DEEPNORM_ALPHA * x_ref[...] + (coef * gate_ref[...]) * accs[0]


def _ffn_down_body(coef, a_ref, w_ref, x_ref, gate_ref, o_ref, acc_ref):
    k = pl.program_id(2)
    part = jnp.dot(a_ref[...], w_ref[...], preferred_element_type=F32)

    @pl.when(k == 0)
    def _():
        acc_ref[...] = part

    @pl.when(k > 0)
    def _():
        acc_ref[...] += part

    @pl.when(k == pl.num_programs(2) - 1)
    def _():
        o_ref[...] = DEEPNORM_ALPHA * x_ref[...] + (coef * gate_ref[...]) * acc_ref[...]


def ffn_down(a, wd, x, mod, seq, slot_gate, coef):
    T, Fp = a.shape
    D = wd.shape[1]
    tm, tn = _tile(TM_MATMUL, seq), _tile(1024, D)
    tk = Fp // FFN_KSPLIT
    assert Fp % FFN_KSPLIT == 0 and tk % LANES == 0
    tpb = seq // tm
    return pl.pallas_call(
        functools.partial(_ffn_down_body, coef),
        grid=(T // tm, D // tn, FFN_KSPLIT),
        in_specs=[pl.BlockSpec((tm, tk), lambda i, j, k: (i, k)),
                  pl.BlockSpec((tk, tn), lambda i, j, k: (k, j)),
                  pl.BlockSpec((tm, tn), lambda i, j, k: (i, j)),
                  pl.BlockSpec((None, 1, tn), lambda i, j, k: ((i // tpb) * N_ADA + slot_gate, 0, j))],
        out_specs=pl.BlockSpec((tm, tn), lambda i, j, k: (i, j)),
        out_shape=jax.ShapeDtypeStruct((T, D), F32),
        scratch_shapes=[pltpu.VMEM((tm, tn), F32)],
        compiler_params=_params(("parallel", "arbitrary", "arbitrary"), 52),
        name="ffn_down",
    )(a, wd, x, mod)


def _rope(t, c, s1, s2):
    half = ROPE_DIM // 2
    return t * c + pltpu.roll(t, LANES - half, axis=1) * s1 + pltpu.roll(t, half, axis=1) * s2


def _q_epilogue(n_heads, accs, e_refs, o_refs):
    c, s1, s2 = (r[...] for r in e_refs)
    qscale = NSA_HEAD_DIM ** -0.5 * LOG2E
    for h in range(n_heads):
        t = accs[0][:, h * LANES:(h + 1) * LANES]
        o_refs[0][h] = (_rope(t, c, s1, s2) * qscale).astype(o_refs[0].dtype)


def _kv_plain_epilogue(n_groups, accs, e_refs, o_refs):
    for g in range(n_groups):
        o_refs[0][g] = accs[0][:, g * LANES:(g + 1) * LANES].astype(o_refs[0].dtype)


def _kv_rope_epilogue(n_groups, accs, e_refs, o_refs):
    c, s1, s2 = (r[...] for r in e_refs)
    is_key = pl.program_id(1) % 2 == 0

    @pl.when(is_key)
    def _():
        for g in range(n_groups):
            t = accs[0][:, g * LANES:(g + 1) * LANES]
            o_refs[0][g] = _rope(t, c, s1, s2).astype(o_refs[0].dtype)

    @pl.when(jnp.logical_not(is_key))
    def _():
        for g in range(n_groups):
            o_refs[0][g] = accs[0][:, g * LANES:(g + 1) * LANES].astype(o_refs[0].dtype)


def _plain_epilogue(accs, e_refs, o_refs):
    o_refs[0][...] = accs[0].astype(o_refs[0].dtype)


def _sigmoid_epilogue(accs, e_refs, o_refs):
    o_refs[0][...] = _sigmoid(accs[0]).astype(o_refs[0].dtype)


def _small_epilogue(accs, e_refs, o_refs):
    a = accs[0]
    lane = lax.broadcasted_iota(jnp.int32, a.shape, 1)
    o_refs[0][...] = jnp.where(lane >= LANES, _sigmoid(a), a)


def _compress_body(seg_ref, w1_ref, w2_ref, pos_ref, c_ref, s1_ref, s2_ref, o_ref):
    n_seg = seg_ref.shape[0]
    half_k = seg_ref.shape[1]
    seg = seg_ref[...].astype(BF16)
    top = jnp.dot(seg, w1_ref[0:half_k, :], preferred_element_type=F32)
    bot = jnp.dot(seg, w1_ref[half_k:2 * half_k, :], preferred_element_type=F32)
    cb = jnp.dot(pos_ref[...], w1_ref[...], preferred_element_type=F32)[0:1, :]
    hid = _silu(top + pltpu.roll(bot, n_seg - 1, axis=0) + cb).astype(BF16)
    out = jnp.dot(hid, w2_ref[...], preferred_element_type=F32)
    out = _rope(out, c_ref[...], s1_ref[...], s2_ref[...])
    rowi = lax.broadcasted_iota(jnp.int32, out.shape, 0)
    o_ref[...] = jnp.where(rowi < n_seg - 1, out, 0.0).astype(o_ref.dtype)


def nsa_compress(kvseg, w1, w2, pos8, tabs):
    _, B, G, n_seg, K2 = kvseg.shape
    dh = NSA_HEAD_DIM
    hid = w1.shape[-1]
    tab_spec = pl.BlockSpec((None, None, n_seg, dh), lambda kv, b, g: (kv, b, 0, 0))
    return pl.pallas_call(
        _compress_body,
        grid=(2, B, G),
        in_specs=[pl.BlockSpec((None, None, None, n_seg, K2), lambda kv, b, g: (kv, b, g, 0, 0)),
                  pl.BlockSpec((None, 2 * K2, hid), lambda kv, b, g: (kv, 0, 0)),
                  pl.BlockSpec((None, hid, dh), lambda kv, b, g: (kv, 0, 0)),
                  pl.BlockSpec((SUBLANES, 2 * K2), lambda kv, b, g: (0, 0)),
                  tab_spec, tab_spec, tab_spec],
        out_specs=pl.BlockSpec((None, None, None, n_seg, dh), lambda kv, b, g: (kv, b, g, 0, 0)),
        out_shape=jax.ShapeDtypeStruct((2, B, G, n_seg, dh), BF16),
        compiler_params=_params(("arbitrary", "arbitrary", "arbitrary"), 40),
        name="nsa_compress",
    )(kvseg, w1, w2, pos8, *tabs)


def _masked_softmax(s, mask):
    sm = jnp.where(mask, s, NEG_INF)
    mx = jnp.max(sm, axis=-1, keepdims=True)
    p = jnp.where(mask, jnp.exp2(sm - mx), 0.0)
    l = jnp.sum(p, axis=-1, keepdims=True)
    return p * jnp.where(l > 0.0, 1.0 / l, 0.0)


def _attn_body(q_ref, kc_ref, vc_ref, kx_ref, vx_ref, kw_ref, vw_ref, gate_ref, ov_ref, o_ref,
               qx_ref, m_ref, acc_ref, *, tq, tk, wk, hpg, top_n):
    dh = NSA_HEAD_DIM
    R = hpg * tq
    n_cp = kc_ref.shape[0]
    t0 = pl.program_id(2) * tq
    for h in range(hpg):
        qx_ref[h * tq:(h + 1) * tq, 0:dh] = q_ref[h]
    q2 = qx_ref[:, 0:dh]
    t_row = t0 + (lax.broadcasted_iota(jnp.int32, (R, 1), 0) & (tq - 1))

    s = _nt_dot(q2, kc_ref[...])
    c_end = lax.broadcasted_iota(jnp.int32, (1, n_cp), 1) * NSA_CMP_STRIDE + (NSA_CMP_BLOCK - 1)
    p_c = _masked_softmax(s, c_end <= t_row)
    o_c = jnp.dot(p_c.astype(BF16), vc_ref[...], preferred_element_type=F32)

    p_sum = p_c[0:tq]
    for h in range(1, hpg):
        p_sum = p_sum + p_c[h * tq:(h + 1) * tq]
    p_hi = p_sum.astype(BF16)
    p_lo = (p_sum - p_hi.astype(F32)).astype(BF16)
    ov = ov_ref[...]
    imp = jnp.dot(p_hi, ov, preferred_element_type=F32) + jnp.dot(p_lo, ov, preferred_element_type=F32)
    imp_t = imp.T
    t_q = t0 + lax.broadcasted_iota(jnp.int32, (1, tq), 1)
    j_i = lax.broadcasted_iota(jnp.int32, (LANES, 1), 0)
    j_f = j_i.astype(F32)
    cur = t_q // NSA_SEL_BLOCK
    causal_blk = j_i * NSA_SEL_BLOCK <= t_q
    forced = (j_i == 0) | (j_i == cur) | (j_i == cur - 1)
    work = jnp.where(causal_blk, imp_t + jnp.where(forced, FORCE_BONUS, 0.0), NEG_INF)
    sel = jnp.zeros(work.shape, jnp.bool_)
    for _ in range(top_n):
        mx = jnp.max(work, axis=0, keepdims=True)
        first = jnp.min(jnp.where(work == mx, j_f, float(LANES)), axis=0, keepdims=True)
        pick = j_f == first
        sel = sel | pick
        work = jnp.where(pick, -jnp.inf, work)
    bias = jnp.where(sel & causal_blk, 0.0, NEG_INF).T.astype(BF16)
    for h in range(hpg):
        qx_ref[h * tq:(h + 1) * tq, dh:2 * dh] = bias

    m_ref[...] = jnp.full(m_ref.shape, NEG_INF, F32)
    acc_ref[...] = jnp.zeros(acc_ref.shape, F32)

    def sel_step(kt, elementwise_causal):
        k0 = pl.multiple_of(kt * tk, tk)
        s = _nt_dot(qx_ref[...], kx_ref[pl.ds(k0, tk), :])
        if elementwise_causal:
            kpos = k0 + lax.broadcasted_iota(jnp.int32, (1, tk), 1)
            s = jnp.where(kpos <= t_row, s, NEG_INF)
        m_old = m_ref[...]
        m_new = jnp.maximum(m_old, jnp.max(s, axis=-1, keepdims=True))
        p = jnp.exp2(s - m_new).astype(BF16)
        pv = jnp.dot(p, vx_ref[pl.ds(k0, tk), :], preferred_element_type=F32)
        acc_ref[...] = jnp.exp2(m_old - m_new) * acc_ref[...] + pv
        m_ref[...] = m_new

    n_full = t0 // tk

    def full_step(kt, carry):
        sel_step(kt, False)
        return carry

    lax.fori_loop(0, n_full, full_step, 0)
    sel_step(n_full, True)
    acc = acc_ref[...]
    l_s = acc[:, dh:dh + 1]
    ok = (m_ref[...] > 0.5 * NEG_INF) & (l_s > 0.0)
    o_s = acc[:, 0:dh] * jnp.where(ok, 1.0 / l_s, 0.0)

    w0 = pl.multiple_of(jnp.maximum(t0 + tq - wk, 0), tq)
    s = _nt_dot(q2, kw_ref[pl.ds(w0, wk), :])
    kpos = w0 + lax.broadcasted_iota(jnp.int32, (1, wk), 1)
    p_w = _masked_softmax(s, (kpos <= t_row) & (kpos > t_row - NSA_WINDOW))
    o_w = jnp.dot(p_w.astype(BF16), vw_ref[pl.ds(w0, wk), :], preferred_element_type=F32)

    gates = gate_ref[...]
    for h in range(hpg):
        rs = slice(h * tq, (h + 1) * tq)
        o = (gates[:, h:h + 1] * o_c[rs] + gates[:, hpg + h:hpg + h + 1] * o_s[rs]
             + gates[:, 2 * hpg + h:2 * hpg + h + 1] * o_w[rs])
        o_ref[:, h * dh:(h + 1) * dh] = o.astype(o_ref.dtype)


def nsa_attention(q, kvc, kx, vx, sw, gates, overlap, seq):
    B, H, S, dh = q.shape
    G = kx.shape[1]
    hpg = H // G
    n_cp = kvc.shape[3]
    tq = _tile(ATTN_TQ, S)
    tk = _tile(ATTN_TK, S)
    wk = min(NSA_WINDOW + tq, S)
    assert tk % tq == 0 and NSA_WINDOW % tq == 0 and tq % NSA_SEL_BLOCK == 0 and S // NSA_SEL_BLOCK <= LANES
    top_n = min(NSA_SEL_TOPN, S // NSA_SEL_BLOCK)
    R = hpg * tq
    nq = S // tq
    body = functools.partial(_attn_body, tq=tq, tk=tk, wk=wk, hpg=hpg, top_n=top_n)
    return pl.pallas_call(
        body,
        grid=(B, G, nq),
        in_specs=[pl.BlockSpec((None, hpg, tq, dh), lambda b, g, i: (b, g, i, 0)),
                  pl.BlockSpec((None, None, None, n_cp, dh), lambda b, g, i: (0, b, g, 0, 0)),
                  pl.BlockSpec((None, None, None, n_cp, dh), lambda b, g, i: (1, b, g, 0, 0)),
                  pl.BlockSpec((None, None, S, 2 * dh), lambda b, g, i: (b, g, 0, 0)),
                  pl.BlockSpec((None, None, S, 2 * dh), lambda b, g, i: (b, g, 0, 0)),
                  pl.BlockSpec((None, None, None, S, dh), lambda b, g, i: (2, b, g, 0, 0)),
                  pl.BlockSpec((None, None, None, S, dh), lambda b, g, i: (3, b, g, 0, 0)),
                  pl.BlockSpec((None, None, tq, 3 * hpg), lambda b, g, i: (b, g, i, 0)),
                  pl.BlockSpec((n_cp, LANES), lambda b, g, i: (0, 0))],
        out_specs=pl.BlockSpec((tq, hpg * dh), lambda b, g, i: (b * nq + i, g)),
        out_shape=jax.ShapeDtypeStruct((B * S, H * dh), BF16),
        scratch_shapes=[pltpu.VMEM((R, 2 * dh), BF16), pltpu.VMEM((R, 1), F32), pltpu.VMEM((R, 2 * dh), F32)],
        compiler_params=_params(("parallel", "parallel", "arbitrary"), 52),
        name="nsa_attention",
    )(q, kvc, kvc, kx, vx, sw, sw, gates, overlap)


def _split3(x):
    hi = x.astype(BF16)
    r = x - hi.astype(F32)
    mid = r.astype(BF16)
    lo = (r - mid.astype(F32)).astype(BF16)
    return hi, mid, lo


def _softplus(x):
    return jnp.maximum(x, 0.0) + jnp.log1p(jnp.exp(-jnp.abs(x)))


def _ssd_body(x_ref, xh_ref, b_ref, bh_ref, c_ref, ch_ref, z_ref, dt_ref, dtt_ref,
              wx_ref, wb_ref, wc_ref, cbx_ref, cbb_ref, cbc_ref,
              dtb_ref, dtbt_ref, alog_ref, alogt_ref, dsk_ref, nw_ref, o_ref,
              state_ref, pad_ref, y_ref, acum_ref, dtg_ref, acumt_ref, dtt_s_ref, *, L, hpg, n_groups):
    P, N = SSD_HEAD_DIM, SSD_STATE
    GW = hpg * P
    HALO = SUBLANES
    c = pl.program_id(1)
    g = pl.program_id(2)
    n_heads = hpg * n_groups

    @pl.when((c == 0) & (g == 0))
    def _():
        state_ref[...] = jnp.zeros(state_ref.shape, F32)

    ri = lax.broadcasted_iota(jnp.int32, (L, L), 0)
    ci = lax.broadcasted_iota(jnp.int32, (L, L), 1)
    tril = ri >= ci

    @pl.when(g == 0)
    def _():
        dt = _softplus(dt_ref[:, 0:n_heads] + dtb_ref[...])
        da = dt * (-jnp.exp(alog_ref[...]))
        dtt = _softplus(dtt_ref[...] + dtbt_ref[...])
        dat = dtt * (-jnp.exp(alogt_ref[...]))
        lower = tril.astype(BF16)
        upper = (ri <= ci).astype(BF16)
        acum = sum(jnp.dot(lower, part, preferred_element_type=F32) for part in _split3(da))
        acumt_ref[...] = sum(jnp.dot(part, upper, preferred_element_type=F32) for part in _split3(dat))
        dtt_s_ref[...] = dtt
        for gg in range(n_groups):
            acum_ref[gg, :, 0:hpg] = acum[:, gg * hpg:(gg + 1) * hpg]
            dtg_ref[gg, :, 0:hpg] = dt[:, gg * hpg:(gg + 1) * hpg]

    first = c == 0
    pad_ref[0:HALO, 0:GW] = jnp.where(first, 0.0, xh_ref[...])
    pad_ref[0:HALO, GW:GW + N] = jnp.where(first, 0.0, bh_ref[...])
    pad_ref[0:HALO, GW + N:GW + 2 * N] = jnp.where(first, 0.0, ch_ref[...])
    pad_ref[HALO:HALO + L, 0:GW] = x_ref[...]
    pad_ref[HALO:HALO + L, GW:GW + N] = b_ref[...]
    pad_ref[HALO:HALO + L, GW + N:GW + 2 * N] = c_ref[...]

    def conv(lo, hi, w_ref, bias_ref):
        out = bias_ref[...]
        for k in range(SSD_CONV):
            off = HALO - (SSD_CONV - 1) + k
            out = out + w_ref[k:k + 1, :] * pad_ref[off:off + L, lo:hi]
        return _silu(out)

    xa = conv(0, GW, wx_ref, cbx_ref)
    bg = conv(GW, GW + N, wb_ref, cbb_ref).astype(BF16)
    cg = conv(GW + N, GW + 2 * N, wc_ref, cbc_ref).astype(BF16)
    xat = xa.T
    cb = _nt_dot(cg, bg)
    acg = acum_ref[g]
    dtg = dtg_ref[g]
    h0 = pl.multiple_of(g * hpg, hpg)
    act = acumt_ref[pl.ds(h0, hpg), :]
    dtt = dtt_s_ref[pl.ds(h0, hpg), :]
    dsk = dsk_ref[...]
    for h in range(hpg):
        col = acg[:, h:h + 1]
        row = act[h:h + 1, :]
        a_last = act[h:h + 1, L - 1:L]
        decay = jnp.where(tril, jnp.exp(col - row), 0.0)
        xh = xa[:, h * P:(h + 1) * P]
        xdt = (xh * dtg[:, h:h + 1]).astype(BF16)
        y = jnp.dot((cb * decay).astype(BF16), xdt, preferred_element_type=F32)
        s0 = pl.multiple_of((g * hpg + h) * P, P)
        st = state_ref[pl.ds(s0, P), :]
        y = y + _nt_dot(cg, st.astype(BF16)) * jnp.exp(col)
        xw_t = (xat[h * P:(h + 1) * P, :] * (dtt[h:h + 1, :] * jnp.exp(a_last - row))).astype(BF16)
        state_ref[pl.ds(s0, P), :] = st * jnp.exp(a_last) + jnp.dot(xw_t, bg, preferred_element_type=F32)
        y_ref[:, h * P:(h + 1) * P] = y + dsk[:, h * P:(h + 1) * P] * xh

    yg = y_ref[...] * _silu(z_ref[...])
    ms = jnp.mean(yg * yg, axis=-1, keepdims=True)
    o_ref[...] = (yg * lax.rsqrt(ms + RMS_EPS) * nw_ref[...]).astype(o_ref.dtype)


def ssd_mixer(z, xbc, small, dt_t, conv_w, conv_b, dt_bias, a_log, d_skip, norm_w, batch, seq):
    T, Di = z.shape
    Gs, N, P, Hs = SSD_GROUPS, SSD_STATE, SSD_HEAD_DIM, SSD_HEADS
    hpg = Hs // Gs
    GW = hpg * P
    L = math.gcd(seq, SSD_CHUNK)
    nc = seq // L
    HALO = SUBLANES
    nxb = Di // GW
    assert GW % N == 0 and Hs <= LANES
    b_off = Di // N
    c_off = b_off + Gs

    def rows(b, c, g):
        return b * nc + c

    def halo_rows(b, c, g):
        return jnp.maximum((b * seq + c * L) // HALO - 1, 0)

    in_specs = [
        pl.BlockSpec((L, GW), lambda b, c, g: (rows(b, c, g), g)),
        pl.BlockSpec((HALO, GW), lambda b, c, g: (halo_rows(b, c, g), g)),
        pl.BlockSpec((L, N), lambda b, c, g: (rows(b, c, g), b_off + g)),
        pl.BlockSpec((HALO, N), lambda b, c, g: (halo_rows(b, c, g), b_off + g)),
        pl.BlockSpec((L, N), lambda b, c, g: (rows(b, c, g), c_off + g)),
        pl.BlockSpec((HALO, N), lambda b, c, g: (halo_rows(b, c, g), c_off + g)),
        pl.BlockSpec((L, GW), lambda b, c, g: (rows(b, c, g), g)),
        pl.BlockSpec((L, LANES), lambda b, c, g: (rows(b, c, g), 0)),
        pl.BlockSpec((None, Hs, L), lambda b, c, g: (b, 0, c)),
        pl.BlockSpec((SSD_CONV, GW), lambda b, c, g: (0, g)),
        pl.BlockSpec((SSD_CONV, N), lambda b, c, g: (0, b_off + g)),
        pl.BlockSpec((SSD_CONV, N), lambda b, c, g: (0, c_off + g)),
        pl.BlockSpec((1, GW), lambda b, c, g: (0, g)),
        pl.BlockSpec((1, N), lambda b, c, g: (0, b_off + g)),
        pl.BlockSpec((1, N), lambda b, c, g: (0, c_off + g)),
        pl.BlockSpec((1, Hs), lambda b, c, g: (0, 0)),
        pl.BlockSpec((Hs, 1), lambda b, c, g: (0, 0)),
        pl.BlockSpec((1, Hs), lambda b, c, g: (0, 0)),
        pl.BlockSpec((Hs, 1), lambda b, c, g: (0, 0)),
        pl.BlockSpec((1, GW), lambda b, c, g: (0, g)),
        pl.BlockSpec((1, GW), lambda b, c, g: (0, g)),
    ]
    cb2 = conv_b.reshape(1, -1)
    body = functools.partial(_ssd_body, L=L, hpg=hpg, n_groups=Gs)
    return pl.pallas_call(
        body,
        grid=(batch, nc, Gs),
        in_specs=in_specs,
        out_specs=pl.BlockSpec((L, GW), lambda b, c, g: (rows(b, c, g), g)),
        out_shape=jax.ShapeDtypeStruct((T, Di), BF16),
        scratch_shapes=[pltpu.VMEM((Hs * P, N), F32),
                        pltpu.VMEM((HALO + L, GW + 2 * N), F32),
                        pltpu.VMEM((L, GW), F32),
                        pltpu.VMEM((Gs, L, LANES), F32),
                        pltpu.VMEM((Gs, L, LANES), F32),
                        pltpu.VMEM((Hs, L), F32),
                        pltpu.VMEM((Hs, L), F32)],
        compiler_params=_params(("arbitrary", "arbitrary", "arbitrary"), 40),
        name="ssd",
    )(xbc, xbc, xbc, xbc, xbc, xbc, z, small, dt_t,
      conv_w, conv_w, conv_w, cb2, cb2, cb2,
      dt_bias.reshape(1, Hs), dt_bias.reshape(Hs, 1), a_log.reshape(1, Hs), a_log.reshape(Hs, 1),
      jnp.repeat(d_skip, P).reshape(1, Di), norm_w.reshape(1, Di))


def _merge_epilogue(accs, e_refs, o_refs):
    ga, gb = e_refs
    o_refs[0][...] = (ga[...].astype(F32) * accs[0] + gb[...].astype(F32) * accs[1]).astype(o_refs[0].dtype)


def _rope_lane_tables(pos):
    half = ROPE_DIM // 2
    inv_freq = jnp.float32(ROPE_THETA) ** (-jnp.arange(half, dtype=F32) / half)
    ang = pos.astype(F32)[..., None] * inv_freq
    cos, sin = jnp.cos(ang), jnp.sin(ang)
    pad = [(0, 0)] * (cos.ndim - 1)
    one = jnp.ones(cos.shape[:-1] + (LANES - ROPE_DIM,), F32)
    c = jnp.concatenate([cos, cos, one], axis=-1)
    s1 = jnp.pad(-sin, pad + [(0, LANES - half)])
    s2 = jnp.pad(sin, pad + [(half, LANES - ROPE_DIM)])
    return c, s1, s2


def _pad_cols(w, n):
    return jnp.pad(w, ((0, 0), (0, n - w.shape[1])))


def _layer(x, mod, positions, tabs, tabs_c, consts, p):
    B, S = positions.shape
    T, D = x.shape
    H, G, dh = NSA_HEADS, NSA_KV_GROUPS, NSA_HEAD_DIM
    hpg = H // G
    F = p["ffn1_w_gate"].shape[1]
    Fp = -(-F // FFN_PAD) * FFN_PAD
    tm = _tile(TM_MATMUL, S)
    tpb = S // tm

    def ffn(h, x_res, wg, wu, wd, slot_gate):
        a = ffn_up(h, _pad_cols(wg, Fp).astype(BF16), _pad_cols(wu, Fp).astype(BF16))
        wd_p = jnp.pad(wd, ((0, Fp - F), (0, 0))).astype(BF16)
        return ffn_down(a, wd_p, x_res, mod, S, slot_gate, 0.5)

    h = modulate(x, mod, S, 0, 1)
    r = ffn(h, x, p["ffn1_w_gate"], p["ffn1_w_up"], p["ffn1_w_down"], 2)
    x, h = ln_mod(r, p["ln1_g"], p["ln1_b"], mod, S, 3, 4)

    w_in = p["w_in"]
    cuts = np.cumsum(IN_SPLIT_SIZES)[:-1].tolist()
    (w_q, w_kc, w_vc, w_ks, w_vs, w_kw, w_vw, w_gn, w_z, w_xbc, w_dt, w_ga, w_gb) = jnp.split(w_in, cuts, axis=1)
    w_small = jnp.concatenate([_pad_cols(w_dt, LANES), _pad_cols(w_gn, LANES)], axis=1).astype(BF16)
    tab_specs = [pl.BlockSpec((tm, LANES), lambda i, j: (i, 0))] * 3

    tnq = hpg * dh
    q = fused_matmul([h], [w_q.astype(BF16)], [(0, 0)], functools.partial(_q_epilogue, hpg), tabs, tab_specs,
                     jax.ShapeDtypeStruct((B, H, S, dh), BF16),
                     pl.BlockSpec((None, hpg, tm, dh), lambda i, j: (i // tpb, j, i % tpb, 0)),
                     tm, tnq, 52, "in_proj_q")
    tnk = G * dh
    kv_spec = pl.BlockSpec((None, None, G, tm, dh), lambda i, j: (j, i // tpb, 0, i % tpb, 0))
    kvcmp = fused_matmul([h], [jnp.concatenate([w_kc, w_vc], axis=1).astype(BF16)], [(0, 0)],
                         functools.partial(_kv_plain_epilogue, G), [], [],
                         jax.ShapeDtypeStruct((2, B, G, S, dh), F32), kv_spec, tm, tnk, 52, "in_proj_kvcmp")
    sw = fused_matmul([h], [jnp.concatenate([w_ks, w_vs, w_kw, w_vw], axis=1).astype(BF16)], [(0, 0)],
                      functools.partial(_kv_rope_epilogue, G), tabs, tab_specs,
                      jax.ShapeDtypeStruct((4, B, G, S, dh), BF16), kv_spec, tm, tnk, 52, "in_proj_slcwin")
    plain_spec = lambda tn: pl.BlockSpec((tm, tn), lambda i, j: (i, j))
    tnz = _tile(1024, SSD_D_INNER)
    z = fused_matmul([h], [w_z.astype(BF16)], [(0, 0)], _plain_epilogue, [], [],
                     jax.ShapeDtypeStruct((T, SSD_D_INNER), F32), plain_spec(tnz), tm, tnz, 52, "in_proj_z")
    tnx = _tile(1024, SSD_CONV_CH)
    xbc = fused_matmul([h], [w_xbc.astype(BF16)], [(0, 0)], _plain_epilogue, [], [],
                       jax.ShapeDtypeStruct((T, SSD_CONV_CH), F32), plain_spec(tnx), tm, tnx, 52, "in_proj_xbc")
    tng = _tile(1024, 2 * D)
    gab = fused_matmul([h], [jnp.concatenate([w_ga, w_gb], axis=1).astype(BF16)], [(0, 0)], _sigmoid_epilogue,
                       [], [], jax.ShapeDtypeStruct((T, 2 * D), BF16), plain_spec(tng), tm, tng, 52, "in_proj_gates")
    small = fused_matmul([h], [w_small], [(0, 0)], _small_epilogue, [], [],
                         jax.ShapeDtypeStruct((T, 2 * LANES), F32), plain_spec(2 * LANES), tm, 2 * LANES, 52,
                         "in_proj_small")

    n_seg = S // NSA_CMP_STRIDE
    kvseg = kvcmp.reshape(2, B, G, n_seg, NSA_CMP_STRIDE * dh)
    w1 = jnp.stack([p["nsa_cmp_k_w1"], p["nsa_cmp_v_w1"]]).astype(BF16)
    w2 = jnp.stack([p["nsa_cmp_k_w2"], p["nsa_cmp_v_w2"]]).astype(BF16)
    pos8 = jnp.broadcast_to(p["nsa_cmp_pos"].reshape(1, -1), (SUBLANES, NSA_CMP_BLOCK * dh)).astype(BF16)
    kvc = nsa_compress(kvseg, w1, w2, pos8, tabs_c)
    e_blk, ones_col, overlap = consts
    kx = jnp.concatenate([sw[0], jnp.broadcast_to(e_blk, (B, G, S, dh))], axis=-1)
    vx = jnp.concatenate([sw[1], jnp.broadcast_to(ones_col, (B, G, S, dh))], axis=-1)
    g3 = small[:, LANES:LANES + 3 * H].reshape(B, S, 3, G, hpg)
    gates = jnp.transpose(g3, (0, 3, 1, 2, 4)).reshape(B, G, S, 3 * hpg)
    o_a = nsa_attention(q, kvc, kx, vx, sw, gates, overlap, S)

    dt_t = jnp.transpose(small[:, :SSD_HEADS].reshape(B, S, SSD_HEADS), (0, 2, 1))
    o_b = ssd_mixer(z, xbc, small, dt_t, p["ssd_conv_w"], p["ssd_conv_b"], p["ssd_dt_bias"], p["ssd_a_log"],
                    p["ssd_d"], p["ssd_norm_w"], B, S)

    tm2 = _tile(TM_DUAL, S)
    tn2 = _tile(512, D)
    nga = D // tn2
    merged = fused_matmul([o_a, o_b], [p["w_branch_a"].astype(BF16), p["w_branch_b"].astype(BF16)],
                          [(0, 0), (1, 1)], _merge_epilogue, [gab, gab],
                          [pl.BlockSpec((tm2, tn2), lambda i, j: (i, j)),
                           pl.BlockSpec((tm2, tn2), lambda i, j: (i, nga + j))],
                          jax.ShapeDtypeStruct((T, D), BF16), pl.BlockSpec((tm2, tn2), lambda i, j: (i, j)),
                          tm2, tn2, 52, "branch_merge")
    tno = _tile(512, D)
    r = fused_matmul([merged], [p["w_out"].astype(BF16)], [(0, 0)], functools.partial(_residual_epilogue, 1.0),
                     [x, mod],
                     [pl.BlockSpec((tm, tno), lambda i, j: (i, j)),
                      pl.BlockSpec((None, 1, tno), lambda i, j: ((i // tpb) * N_ADA + 5, 0, j))],
                     jax.ShapeDtypeStruct((T, D), F32), pl.BlockSpec((tm, tno), lambda i, j: (i, j)),
                     tm, tno, 52, "out_proj")
    x, h = ln_mod(r, p["ln2_g"], p["ln2_b"], mod, S, 6, 7)

    r = ffn(h, x, p["ffn2_w_gate"], p["ffn2_w_up"], p["ffn2_w_down"], 8)
    return layer_norm(r, p["ln3_g"], p["ln3_b"])


def kernel(x, c, positions, w_ada, b_ada, ffn1_w_gate, ffn1_w_up, ffn1_w_down, w_in, nsa_cmp_pos, nsa_cmp_k_w1, nsa_cmp_k_w2, nsa_cmp_v_w1, nsa_cmp_v_w2, ssd_conv_w, ssd_conv_b, ssd_dt_bias, ssd_a_log, ssd_d, ssd_norm_w, w_branch_a, w_branch_b, w_out, ffn2_w_gate, ffn2_w_up, ffn2_w_down, ln1_g, ln1_b, ln2_g, ln2_b, ln3_g, ln3_b):
    B, S, D = x.shape
    dh = NSA_HEAD_DIM
    per_layer = dict(ffn1_w_gate=ffn1_w_gate, ffn1_w_up=ffn1_w_up, ffn1_w_down=ffn1_w_down, w_in=w_in,
                     nsa_cmp_pos=nsa_cmp_pos, nsa_cmp_k_w1=nsa_cmp_k_w1, nsa_cmp_k_w2=nsa_cmp_k_w2,
                     nsa_cmp_v_w1=nsa_cmp_v_w1, nsa_cmp_v_w2=nsa_cmp_v_w2, ssd_conv_w=ssd_conv_w,
                     ssd_conv_b=ssd_conv_b, ssd_dt_bias=ssd_dt_bias, ssd_a_log=ssd_a_log, ssd_d=ssd_d,
                     ssd_norm_w=ssd_norm_w, w_branch_a=w_branch_a, w_branch_b=w_branch_b, w_out=w_out,
                     ffn2_w_gate=ffn2_w_gate, ffn2_w_up=ffn2_w_up, ffn2_w_down=ffn2_w_down,
                     ln1_g=ln1_g, ln1_b=ln1_b, ln2_g=ln2_g, ln2_b=ln2_b, ln3_g=ln3_g, ln3_b=ln3_b)

    tabs = [t.reshape(B * S, LANES) for t in _rope_lane_tables(positions)]
    n_seg = S // NSA_CMP_STRIDE
    c_end = jnp.minimum(NSA_CMP_STRIDE * jnp.arange(n_seg) + NSA_CMP_BLOCK - 1, S - 1)
    ck, s1k, s2k = _rope_lane_tables(positions[:, c_end])
    tabs_c = [jnp.stack([ck, jnp.ones_like(ck)]), jnp.stack([s1k, jnp.zeros_like(s1k)]),
              jnp.stack([s2k, jnp.zeros_like(s2k)])]

    key_blk = np.arange(S) // NSA_SEL_BLOCK
    e_blk = jnp.asarray(key_blk[:, None] == np.arange(LANES)[None, :], BF16)
    ones_col = jnp.asarray(np.arange(dh)[None, :] == 0, BF16) * jnp.ones((S, 1), BF16)
    n_cmp = (S - NSA_CMP_BLOCK) // NSA_CMP_STRIDE + 1
    c_start = NSA_CMP_STRIDE * np.arange(n_seg)
    sel_start = NSA_SEL_BLOCK * np.arange(LANES)
    ov = ((c_start[:, None] < sel_start[None, :] + NSA_SEL_BLOCK)
          & (c_start[:, None] + NSA_CMP_BLOCK - 1 >= sel_start[None, :])
          & (np.arange(n_seg)[:, None] < n_cmp) & (np.arange(LANES)[None, :] < S // NSA_SEL_BLOCK))
    consts = (e_blk, ones_col, jnp.asarray(ov, BF16))

    xt = x.reshape(B * S, D)
    for l in range(DEPTH):
        mod = ada_proj(c, w_ada[l], b_ada[l]).reshape(B * N_ADA, 1, D)
        xt = _layer(xt, mod, positions, tabs, tabs_c, consts, {k: v[l] for k, v in per_layer.items()})
    return xt.reshape(B, S, D)
```

```python
import functools
import math

import jax
import jax.numpy as jnp
import numpy as np
from jax import lax
from jax.experimental import pallas as pl
from jax.experimental.pallas import tpu as pltpu

D_MODEL = 4096
BATCH = 2
SEQ = 8192
DEPTH = 1

NSA_HEADS = 32
NSA_KV_GROUPS = 4
NSA_HEAD_DIM = 128
NSA_CMP_BLOCK = 32
NSA_CMP_STRIDE = 16
NSA_CMP_HIDDEN = 256
NSA_SEL_BLOCK = 64
NSA_SEL_TOPN = 16
NSA_WINDOW = 512
ROPE_THETA = 500000.0
ROPE_DIM = NSA_HEAD_DIM // 4
FORCE_BONUS = 1.0e4
NEG_INF = -1.0e30

SSD_D_INNER = D_MODEL
SSD_HEAD_DIM = 64
SSD_HEADS = SSD_D_INNER // SSD_HEAD_DIM
SSD_GROUPS = 8
SSD_STATE = 128
SSD_CONV = 4
SSD_CHUNK = 256
SSD_CONV_CH = SSD_D_INNER + 2 * SSD_GROUPS * SSD_STATE

FFN_DIM = 256 * ((8 * D_MODEL // 3 + 255) // 256)
N_ADA = 9
LN_EPS = 1e-5
RMS_EPS = 1e-5
DEEPNORM_ALPHA = (2 * DEPTH) ** 0.25

NSA_Q_WIDTH = NSA_HEADS * NSA_HEAD_DIM
NSA_KV_WIDTH = NSA_KV_GROUPS * NSA_HEAD_DIM
IN_SPLIT_SIZES = (NSA_Q_WIDTH,) + (NSA_KV_WIDTH,) * 6 + (3 * NSA_HEADS, SSD_D_INNER, SSD_CONV_CH, SSD_HEADS, D_MODEL, D_MODEL)

F32 = jnp.float32
BF16 = jnp.bfloat16
LANES = 128
SUBLANES = 8
LOG2E = 1.4426950408889634
MIB = 1024 * 1024

TM_MATMUL = 1024
TM_DUAL = 512
TM_ROWWISE = 256
ATTN_TQ = 128
ATTN_TK = 512
FFN_PAD = 512
FFN_KSPLIT = 4


def _tile(pref, dim):
    t = min(pref, dim)
    assert dim % t == 0, (pref, dim)
    return t


def _params(sem, vmem_mib):
    return pltpu.CompilerParams(dimension_semantics=sem, vmem_limit_bytes=vmem_mib * MIB)


def _sigmoid(x):
    return 1.0 / (1.0 + jnp.exp(-x))


def _silu(x):
    return x * _sigmoid(x)


def _nt_dot(a, b):
    return lax.dot_general(a, b, (((1,), (1,)), ((), ())), preferred_element_type=F32)


def _ada_body(c_ref, w_ref, b_ref, o_ref):
    c = c_ref[...]
    ca = _silu(c).astype(BF16)
    o_ref[...] = jnp.dot(ca, w_ref[...].astype(BF16), preferred_element_type=F32) + b_ref[...]


def ada_proj(c, w_ada, b_ada):
    B, D = c.shape
    N = w_ada.shape[1]
    tn = _tile(512, N)
    c8 = jnp.zeros((SUBLANES, D), F32).at[:B].set(c)
    out = pl.pallas_call(
        _ada_body,
        grid=(N // tn,),
        in_specs=[pl.BlockSpec((SUBLANES, D), lambda j: (0, 0)),
                  pl.BlockSpec((D, tn), lambda j: (0, j)),
                  pl.BlockSpec((1, tn), lambda j: (0, j))],
        out_specs=pl.BlockSpec((SUBLANES, tn), lambda j: (0, j)),
        out_shape=jax.ShapeDtypeStruct((SUBLANES, N), F32),
        compiler_params=_params(("arbitrary",), 40),
        name="ada_proj",
    )(c8, w_ada, b_ada.reshape(1, N))
    return out[:B]


def _mod_body(x_ref, shift_ref, scale_ref, h_ref):
    h_ref[...] = (x_ref[...] * (1.0 + scale_ref[...]) + shift_ref[...]).astype(h_ref.dtype)


def _ln_mod_body(r_ref, g_ref, b_ref, shift_ref, scale_ref, x_ref, h_ref):
    r = r_ref[...]
    mu = jnp.mean(r, axis=-1, keepdims=True)
    d = r - mu
    var = jnp.mean(d * d, axis=-1, keepdims=True)
    x = d * lax.rsqrt(var + LN_EPS) * g_ref[...] + b_ref[...]
    x_ref[...] = x
    h_ref[...] = (x * (1.0 + scale_ref[...]) + shift_ref[...]).astype(h_ref.dtype)


def _ln_body(r_ref, g_ref, b_ref, x_ref):
    r = r_ref[...]
    mu = jnp.mean(r, axis=-1, keepdims=True)
    d = r - mu
    var = jnp.mean(d * d, axis=-1, keepdims=True)
    x_ref[...] = d * lax.rsqrt(var + LN_EPS) * g_ref[...] + b_ref[...]


def _mod_spec(D, tiles_per_batch, slot):
    return pl.BlockSpec((None, 1, D), lambda i: ((i // tiles_per_batch) * N_ADA + slot, 0, 0))


def modulate(x, mod, seq, slot_shift, slot_scale):
    T, D = x.shape
    tm = _tile(TM_ROWWISE, seq)
    tpb = seq // tm
    return pl.pallas_call(
        _mod_body,
        grid=(T // tm,),
        in_specs=[pl.BlockSpec((tm, D), lambda i: (i, 0)),
                  _mod_spec(D, tpb, slot_shift), _mod_spec(D, tpb, slot_scale)],
        out_specs=pl.BlockSpec((tm, D), lambda i: (i, 0)),
        out_shape=jax.ShapeDtypeStruct((T, D), BF16),
        compiler_params=_params(("parallel",), 32),
        name="modulate",
    )(x, mod, mod)


def ln_mod(r, g, b, mod, seq, slot_shift, slot_scale):
    T, D = r.shape
    tm = _tile(TM_ROWWISE, seq)
    tpb = seq // tm
    row = pl.BlockSpec((tm, D), lambda i: (i, 0))
    vec = pl.BlockSpec((1, D), lambda i: (0, 0))
    return pl.pallas_call(
        _ln_mod_body,
        grid=(T // tm,),
        in_specs=[row, vec, vec, _mod_spec(D, tpb, slot_shift), _mod_spec(D, tpb, slot_scale)],
        out_specs=[row, row],
        out_shape=[jax.ShapeDtypeStruct((T, D), F32), jax.ShapeDtypeStruct((T, D), BF16)],
        compiler_params=_params(("parallel",), 40),
        name="ln_mod",
    )(r, g.reshape(1, D), b.reshape(1, D), mod, mod)


def layer_norm(r, g, b):
    T, D = r.shape
    tm = _tile(TM_ROWWISE, T)
    row = pl.BlockSpec((tm, D), lambda i: (i, 0))
    vec = pl.BlockSpec((1, D), lambda i: (0, 0))
    return pl.pallas_call(
        _ln_body,
        grid=(T // tm,),
        in_specs=[row, vec, vec],
        out_specs=row,
        out_shape=jax.ShapeDtypeStruct((T, D), F32),
        compiler_params=_params(("parallel",), 32),
        name="layer_norm",
    )(r, g.reshape(1, D), b.reshape(1, D))


def _fused_matmul_body(nx, nw, ne, pairs, epilogue, *refs):
    x_refs = refs[:nx]
    w_refs = refs[nx:nx + nw]
    e_refs = refs[nx + nw:nx + nw + ne]
    o_refs = refs[nx + nw + ne:]
    accs = [jnp.dot(x_refs[a][...], w_refs[b][...], preferred_element_type=F32) for a, b in pairs]
    epilogue(accs, e_refs, o_refs)


def fused_matmul(xs, ws, pairs, epilogue, extras, extra_specs, out_shapes, out_specs, tm, tn, vmem_mib, name):
    M = xs[0].shape[0]
    N = ws[0].shape[1]
    in_specs = ([pl.BlockSpec((tm, x.shape[1]), lambda i, j: (i, 0)) for x in xs]
                + [pl.BlockSpec((w.shape[0], tn), lambda i, j: (0, j)) for w in ws]
                + list(extra_specs))
    body = functools.partial(_fused_matmul_body, len(xs), len(ws), len(extras), tuple(pairs), epilogue)
    return pl.pallas_call(
        body,
        grid=(M // tm, N // tn),
        in_specs=in_specs,
        out_specs=out_specs,
        out_shape=out_shapes,
        compiler_params=_params(("parallel", "arbitrary"), vmem_mib),
        name=name,
    )(*xs, *ws, *extras)


def _swiglu_epilogue(accs, e_refs, o_refs):
    g, u = accs
    o_refs[0][...] = (_silu(g) * u).astype(o_refs[0].dtype)


def ffn_up(h, wg, wu):
    T, D = h.shape
    Fp = wg.shape[1]
    tm, tn = _tile(TM_MATMUL, T), _tile(FFN_PAD, Fp)
    return fused_matmul([h], [wg, wu], [(0, 0), (0, 1)], _swiglu_epilogue, [], [],
                        jax.ShapeDtypeStruct((T, Fp), BF16), pl.BlockSpec((tm, tn), lambda i, j: (i, j)),
                        tm, tn, 52, "ffn_up")


def _residual_epilogue(coef, accs, e_refs, o_refs):
    x_ref, gate_ref = e_refs
    o_refs[0][...] = DEEPNORM_ALPHA * x_ref[...] + (coef * gate_ref[...]) * accs[0]


def _ffn_down_body(coef, a_ref, w_ref, x_ref, gate_ref, o_ref, acc_ref):
    k = pl.program_id(2)
    part = jnp.dot(a_ref[...], w_ref[...], preferred_element_type=F32)

    @pl.when(k == 0)
    def _():
        acc_ref[...] = part

    @pl.when(k > 0)
    def _():
        acc_ref[...] += part

    @pl.when(k == pl.num_programs(2) - 1)
    def _():
        o_ref[...] = DEEPNORM_ALPHA * x_ref[...] + (coef * gate_ref[...]) * acc_ref[...]


def ffn_down(a, wd, x, mod, seq, slot_gate, coef):
    T, Fp = a.shape
    D = wd.shape[1]
    tm, tn = _tile(TM_MATMUL, seq), _tile(1024, D)
    tk = Fp // FFN_KSPLIT
    assert Fp % FFN_KSPLIT == 0 and tk % LANES == 0
    tpb = seq // tm
    return pl.pallas_call(
        functools.partial(_ffn_down_body, coef),
        grid=(T // tm, D // tn, FFN_KSPLIT),
        in_specs=[pl.BlockSpec((tm, tk), lambda i, j, k: (i, k)),
                  pl.BlockSpec((tk, tn), lambda i, j, k: (k, j)),
                  pl.BlockSpec((tm, tn), lambda i, j, k: (i, j)),
                  pl.BlockSpec((None, 1, tn), lambda i, j, k: ((i // tpb) * N_ADA + slot_gate, 0, j))],
        out_specs=pl.BlockSpec((tm, tn), lambda i, j, k: (i, j)),
        out_shape=jax.ShapeDtypeStruct((T, D), F32),
        scratch_shapes=[pltpu.VMEM((tm, tn), F32)],
        compiler_params=_params(("parallel", "arbitrary", "arbitrary"), 52),
        name="ffn_down",
    )(a, wd, x, mod)


def _rope(t, c, s1, s2):
    half = ROPE_DIM // 2
    return t * c + pltpu.roll(t, LANES - half, axis=1) * s1 + pltpu.roll(t, half, axis=1) * s2


def _q_epilogue(n_heads, accs, e_refs, o_refs):
    c, s1, s2 = (r[...] for r in e_refs)
    qscale = NSA_HEAD_DIM ** -0.5 * LOG2E
    for h in range(n_heads):
        t = accs[0][:, h * LANES:(h + 1) * LANES]
        o_refs[0][h] = (_rope(t, c, s1, s2) * qscale).astype(o_refs[0].dtype)


def _kv_plain_epilogue(n_groups, accs, e_refs, o_refs):
    for g in range(n_groups):
        o_refs[0][g] = accs[0][:, g * LANES:(g + 1) * LANES].astype(o_refs[0].dtype)


def _kv_rope_epilogue(n_groups, accs, e_refs, o_refs):
    c, s1, s2 = (r[...] for r in e_refs)
    is_key = pl.program_id(1) % 2 == 0

    @pl.when(is_key)
    def _():
        for g in range(n_groups):
            t = accs[0][:, g * LANES:(g + 1) * LANES]
            o_refs[0][g] = _rope(t, c, s1, s2).astype(o_refs[0].dtype)

    @pl.when(jnp.logical_not(is_key))
    def _():
        for g in range(n_groups):
            o_refs[0][g] = accs[0][:, g * LANES:(g + 1) * LANES].astype(o_refs[0].dtype)


def _plain_epilogue(accs, e_refs, o_refs):
    o_refs[0][...] = accs[0].astype(o_refs[0].dtype)


def _sigmoid_epilogue(accs, e_refs, o_refs):
    o_refs[0][...] = _sigmoid(accs[0]).astype(o_refs[0].dtype)


def _small_epilogue(accs, e_refs, o_refs):
    a = accs[0]
    lane = lax.broadcasted_iota(jnp.int32, a.shape, 1)
    o_refs[0][...] = jnp.where(lane >= LANES, _sigmoid(a), a)


def _compress_body(seg_ref, w1_ref, w2_ref, pos_ref, c_ref, s1_ref, s2_ref, o_ref):
    n_seg = seg_ref.shape[0]
    half_k = seg_ref.shape[1]
    seg = seg_ref[...].astype(BF16)
    top = jnp.dot(seg, w1_ref[0:half_k, :], preferred_element_type=F32)
    bot = jnp.dot(seg, w1_ref[half_k:2 * half_k, :], preferred_element_type=F32)
    cb = jnp.dot(pos_ref[...], w1_ref[...], preferred_element_type=F32)[0:1, :]
    hid = _silu(top + pltpu.roll(bot, n_seg - 1, axis=0) + cb).astype(BF16)
    out = jnp.dot(hid, w2_ref[...], preferred_element_type=F32)
    out = _rope(out, c_ref[...], s1_ref[...], s2_ref[...])
    rowi = lax.broadcasted_iota(jnp.int32, out.shape, 0)
    o_ref[...] = jnp.where(rowi < n_seg - 1, out, 0.0).astype(o_ref.dtype)


def nsa_compress(kvseg, w1, w2, pos8, tabs):
    _, B, G, n_seg, K2 = kvseg.shape
    dh = NSA_HEAD_DIM
    hid = w1.shape[-1]
    tab_spec = pl.BlockSpec((None, None, n_seg, dh), lambda kv, b, g: (kv, b, 0, 0))
    return pl.pallas_call(
        _compress_body,
        grid=(2, B, G),
        in_specs=[pl.BlockSpec((None, None, None, n_seg, K2), lambda kv, b, g: (kv, b, g, 0, 0)),
                  pl.BlockSpec((None, 2 * K2, hid), lambda kv, b, g: (kv, 0, 0)),
                  pl.BlockSpec((None, hid, dh), lambda kv, b, g: (kv, 0, 0)),
                  pl.BlockSpec((SUBLANES, 2 * K2), lambda kv, b, g: (0, 0)),
                  tab_spec, tab_spec, tab_spec],
        out_specs=pl.BlockSpec((None, None, None, n_seg, dh), lambda kv, b, g: (kv, b, g, 0, 0)),
        out_shape=jax.ShapeDtypeStruct((2, B, G, n_seg, dh), BF16),
        compiler_params=_params(("arbitrary", "arbitrary", "arbitrary"), 40),
        name="nsa_compress",
    )(kvseg, w1, w2, pos8, *tabs)


def _pair(x):
    return jnp.concatenate([x, x], axis=1)


def _attn_body(q_ref, kc_ref, vcxt_ref, kx_ref, vxt_ref, kw_ref, vwxt_ref, gate_ref, o_ref,
               qxt_ref, s_ref, mt_ref, m_ref, acc_ref, out_ref, *, tq, tk, wk, hpg, top_n):
    dh = NSA_HEAD_DIM
    pw = 2 * tq
    n_pair = hpg // 2
    n_cp = kc_ref.shape[0]
    t0 = pl.program_id(2) * tq
    t_q = t0 + lax.broadcasted_iota(jnp.int32, (1, tq), 1)
    for h in range(hpg):
        qxt_ref[0:dh, h * tq:(h + 1) * tq] = q_ref[h].astype(F32).T.astype(BF16)
    qt = qxt_ref[0:dh, :]
    gates = gate_ref[...]

    def gate_pair(branch, j):
        r0 = branch * hpg + 2 * j
        return jnp.concatenate([gates[r0:r0 + 1, :], gates[r0 + 1:r0 + 2, :]], axis=1)

    c_end = lax.broadcasted_iota(jnp.int32, (n_cp, 1), 0) * NSA_CMP_STRIDE + (NSA_CMP_BLOCK - 1)
    c_bias = _pair(jnp.where(c_end <= t_q, 0.0, NEG_INF))
    kc = kc_ref[...]
    vcxt = vcxt_ref[...]

    def cmp_scores(j):
        s = jnp.dot(kc, qt[:, j * pw:(j + 1) * pw], preferred_element_type=F32) + c_bias
        return s, jnp.max(s, axis=0, keepdims=True)

    imp_t = jnp.zeros((LANES, tq), F32)
    staged = cmp_scores(0)
    for j in range(n_pair):
        cols = slice(j * pw, (j + 1) * pw)
        s, mx = staged
        if j + 1 < n_pair:
            staged = cmp_scores(j + 1)
        e = jnp.exp2(s - mx)
        inv = jnp.where(mx > 0.5 * NEG_INF, 1.0 / jnp.sum(e, axis=0, keepdims=True), 0.0)
        r = jnp.dot(vcxt, e.astype(BF16), preferred_element_type=F32)
        out_ref[:, cols] = r[0:dh, :] * (gate_pair(0, j) * inv)
        ri = r[dh:2 * dh, :] * inv
        imp_t = imp_t + ri[:, 0:tq] + ri[:, tq:pw]

    j_i = lax.broadcasted_iota(jnp.int32, (LANES, 1), 0)
    j_f = j_i.astype(F32)
    cur = t_q // NSA_SEL_BLOCK
    causal_blk = j_i * NSA_SEL_BLOCK <= t_q
    forced = (j_i == 0) | (j_i == cur) | (j_i == cur - 1)
    work = jnp.where(causal_blk, imp_t + jnp.where(forced, FORCE_BONUS, 0.0), NEG_INF)
    sel = jnp.zeros(work.shape, jnp.bool_)
    for _ in range(top_n):
        mx = jnp.max(work, axis=0, keepdims=True)
        first = jnp.min(jnp.where(work == mx, j_f, float(LANES)), axis=0, keepdims=True)
        pick = j_f == first
        sel = sel | pick
        work = jnp.where(pick, -jnp.inf, work)
    bias_t = jnp.where(sel & causal_blk, 0.0, NEG_INF).astype(BF16)
    for h in range(hpg):
        qxt_ref[dh:2 * dh, h * tq:(h + 1) * tq] = bias_t

    def score_tile(kt, slot):
        k0 = pl.multiple_of(kt * tk, tk)
        kpos = k0 + lax.broadcasted_iota(jnp.int32, (tk, 1), 0)
        causal_bias = jnp.where(kpos <= t_q, 0.0, NEG_INF)
        s = jnp.dot(kx_ref[pl.ds(k0, tk), :], qxt_ref[...], preferred_element_type=F32)
        s = s + jnp.concatenate([causal_bias] * hpg, axis=1)
        s_ref[slot] = s
        mt_ref[slot] = jnp.max(s, axis=0, keepdims=True)

    def softmax_pv(kt, slot):
        m_old = m_ref[...]
        m_new = jnp.maximum(m_old, mt_ref[slot])
        p = jnp.exp2(s_ref[slot] - m_new).astype(BF16)
        pv = jnp.dot(vxt_ref[kt], p, preferred_element_type=F32)
        acc_ref[...] = jnp.exp2(m_old - m_new) * acc_ref[...] + pv
        m_ref[...] = m_new

    score_tile(0, 0)

    w0 = pl.multiple_of(jnp.maximum(t0 + tq - wk, 0), tq)
    c0 = w0 // tq
    kpos_w = w0 + lax.broadcasted_iota(jnp.int32, (wk, 1), 0)
    w_bias = _pair(jnp.where((kpos_w <= t_q) & (kpos_w > t_q - NSA_WINDOW), 0.0, NEG_INF))
    kw = kw_ref[pl.ds(w0, wk), :]
    vwt = jnp.concatenate([vwxt_ref[c0 + i] for i in range(wk // tq)], axis=1)

    def window_scores(j):
        s = jnp.dot(kw, qt[:, j * pw:(j + 1) * pw], preferred_element_type=F32) + w_bias
        return s, jnp.max(s, axis=0, keepdims=True)

    staged = window_scores(0)
    for j in range(n_pair):
        cols = slice(j * pw, (j + 1) * pw)
        s, mx = staged
        if j + 1 < n_pair:
            staged = window_scores(j + 1)
        r = jnp.dot(vwt, jnp.exp2(s - mx).astype(BF16), preferred_element_type=F32)
        inv = jnp.where(mx > 0.5 * NEG_INF, 1.0 / r[dh:dh + 1, :], 0.0)
        out_ref[:, cols] += r[0:dh, :] * (gate_pair(2, j) * inv)

    m_ref[...] = jnp.full(m_ref.shape, NEG_INF, F32)
    acc_ref[...] = jnp.zeros(acc_ref.shape, F32)
    n_full = t0 // tk

    def full_step(kt, carry):
        score_tile(kt + 1, (kt + 1) & 1)
        softmax_pv(kt, kt & 1)
        return carry

    lax.fori_loop(0, n_full, full_step, 0)
    softmax_pv(n_full, n_full & 1)

    for h in range(hpg):
        cols = slice(h * tq, (h + 1) * tq)
        acc = acc_ref[:, cols]
        l_s = acc[dh:dh + 1, :]
        ok = (m_ref[:, cols] > 0.5 * NEG_INF) & (l_s > 0.0)
        o_t = out_ref[:, cols] + acc[0:dh, :] * (gates[hpg + h:hpg + h + 1, :] * jnp.where(ok, 1.0 / l_s, 0.0))
        o_ref[:, h * dh:(h + 1) * dh] = o_t.T.astype(o_ref.dtype)


def nsa_attention(q, kvc, vcxt, kx, vxt, sw, vwxt, gates_t, seq):
    B, H, S, dh = q.shape
    G = kx.shape[1]
    hpg = H // G
    n_cp = kvc.shape[3]
    tq = _tile(ATTN_TQ, S)
    tk = _tile(ATTN_TK, S)
    wk = min(NSA_WINDOW + tq, S)
    assert tk % tq == 0 and NSA_WINDOW % tq == 0 and tq % NSA_SEL_BLOCK == 0 and S // NSA_SEL_BLOCK <= LANES
    assert hpg % 2 == 0 and wk % tq == 0 and tq == LANES
    top_n = min(NSA_SEL_TOPN, S // NSA_SEL_BLOCK)
    R = hpg * tq
    nq = S // tq
    body = functools.partial(_attn_body, tq=tq, tk=tk, wk=wk, hpg=hpg, top_n=top_n)
    per_group = lambda *blk: pl.BlockSpec((None, None) + blk, lambda b, g, i: (b, g) + (0,) * len(blk))
    return pl.pallas_call(
        body,
        grid=(B, G, nq),
        in_specs=[pl.BlockSpec((None, hpg, tq, dh), lambda b, g, i: (b, g, i, 0)),
                  pl.BlockSpec((None, None, None, n_cp, dh), lambda b, g, i: (0, b, g, 0, 0)),
                  per_group(2 * dh, n_cp),
                  per_group(S, 2 * dh), per_group(S // tk, 2 * dh, tk),
                  pl.BlockSpec((None, None, None, S, dh), lambda b, g, i: (2, b, g, 0, 0)),
                  per_group(S // tq, 2 * dh, tq),
                  pl.BlockSpec((None, None, 3 * hpg, tq), lambda b, g, i: (b, g, 0, i))],
        out_specs=pl.BlockSpec((tq, hpg * dh), lambda b, g, i: (b * nq + i, g)),
        out_shape=jax.ShapeDtypeStruct((B * S, H * dh), BF16),
        scratch_shapes=[pltpu.VMEM((2 * dh, R), BF16),
                        pltpu.VMEM((2, tk, R), F32),
                        pltpu.VMEM((2, 1, R), F32),
                        pltpu.VMEM((1, R), F32),
                        pltpu.VMEM((2 * dh, R), F32),
                        pltpu.VMEM((dh, R), F32)],
        compiler_params=_params(("parallel", "parallel", "arbitrary"), 56),
        name="nsa_attention",
    )(q, kvc, vcxt, kx, vxt, sw, vwxt, gates_t)


def _split3(x):
    hi = x.astype(BF16)
    r = x - hi.astype(F32)
    mid = r.astype(BF16)
    lo = (r - mid.astype(F32)).astype(BF16)
    return hi, mid, lo


def _softplus(x):
    return jnp.maximum(x, 0.0) + jnp.log1p(jnp.exp(-jnp.abs(x)))


def _ssd_body(x_ref, xh_ref, b_ref, bh_ref, c_ref, ch_ref, z_ref, dt_ref, dtt_ref,
              wx_ref, wb_ref, wc_ref, cbx_ref, cbb_ref, cbc_ref,
              dtb_ref, dtbt_ref, alog_ref, alogt_ref, dsk_ref, nw_ref, o_ref,
              state_ref, pad_ref, y_ref, acum_ref, dtg_ref, acumt_ref, dtt_s_ref, *, L, hpg, n_groups):
    P, N = SSD_HEAD_DIM, SSD_STATE
    GW = hpg * P
    HALO = SUBLANES
    c = pl.program_id(1)
    g = pl.program_id(2)
    n_heads = hpg * n_groups

    @pl.when((c == 0) & (g == 0))
    def _():
        state_ref[...] = jnp.zeros(state_ref.shape, F32)

    ri = lax.broadcasted_iota(jnp.int32, (L, L), 0)
    ci = lax.broadcasted_iota(jnp.int32, (L, L), 1)
    tril = ri >= ci

    @pl.when(g == 0)
    def _():
        dt = _softplus(dt_ref[:, 0:n_heads] + dtb_ref[...])
        da = dt * (-jnp.exp(alog_ref[...]))
        dtt = _softplus(dtt_ref[...] + dtbt_ref[...])
        dat = dtt * (-jnp.exp(alogt_ref[...]))
        lower = tril.astype(BF16)
        upper = (ri <= ci).astype(BF16)
        acum = sum(jnp.dot(lower, part, preferred_element_type=F32) for part in _split3(da))
        acumt_ref[...] = sum(jnp.dot(part, upper, preferred_element_type=F32) for part in _split3(dat))
        dtt_s_ref[...] = dtt
        for gg in range(n_groups):
            acum_ref[gg, :, 0:hpg] = acum[:, gg * hpg:(gg + 1) * hpg]
            dtg_ref[gg, :, 0:hpg] = dt[:, gg * hpg:(gg + 1) * hpg]

    first = c == 0
    pad_ref[0:HALO, 0:GW] = jnp.where(first, 0.0, xh_ref[...])
    pad_ref[0:HALO, GW:GW + N] = jnp.where(first, 0.0, bh_ref[...])
    pad_ref[0:HALO, GW + N:GW + 2 * N] = jnp.where(first, 0.0, ch_ref[...])
    pad_ref[HALO:HALO + L, 0:GW] = x_ref[...]
    pad_ref[HALO:HALO + L, GW:GW + N] = b_ref[...]
    pad_ref[HALO:HALO + L, GW + N:GW + 2 * N] = c_ref[...]

    def conv(lo, hi, w_ref, bias_ref):
        out = bias_ref[...]
        for k in range(SSD_CONV):
            off = HALO - (SSD_CONV - 1) + k
            out = out + w_ref[k:k + 1, :] * pad_ref[off:off + L, lo:hi]
        return _silu(out)

    xa = conv(0, GW, wx_ref, cbx_ref)
    bg = conv(GW, GW + N, wb_ref, cbb_ref).astype(BF16)
    cg = conv(GW + N, GW + 2 * N, wc_ref, cbc_ref).astype(BF16)
    xat = xa.T
    cb = _nt_dot(cg, bg)
    acg = acum_ref[g]
    dtg = dtg_ref[g]
    h0 = pl.multiple_of(g * hpg, hpg)
    act = acumt_ref[pl.ds(h0, hpg), :]
    dtt = dtt_s_ref[pl.ds(h0, hpg), :]
    dsk = dsk_ref[...]
    for h in range(hpg):
        col = acg[:, h:h + 1]
        row = act[h:h + 1, :]
        a_last = act[h:h + 1, L - 1:L]
        decay = jnp.where(tril, jnp.exp(col - row), 0.0)
        xh = xa[:, h * P:(h + 1) * P]
        xdt = (xh * dtg[:, h:h + 1]).astype(BF16)
        y = jnp.dot((cb * decay).astype(BF16), xdt, preferred_element_type=F32)
        s0 = pl.multiple_of((g * hpg + h) * P, P)
        st = state_ref[pl.ds(s0, P), :]
        y = y + _nt_dot(cg, st.astype(BF16)) * jnp.exp(col)
        xw_t = (xat[h * P:(h + 1) * P, :] * (dtt[h:h + 1, :] * jnp.exp(a_last - row))).astype(BF16)
        state_ref[pl.ds(s0, P), :] = st * jnp.exp(a_last) + jnp.dot(xw_t, bg, preferred_element_type=F32)
        y_ref[:, h * P:(h + 1) * P] = y + dsk[:, h * P:(h + 1) * P] * xh

    yg = y_ref[...] * _silu(z_ref[...])
    ms = jnp.mean(yg * yg, axis=-1, keepdims=True)
    o_ref[...] = (yg * lax.rsqrt(ms + RMS_EPS) * nw_ref[...]).astype(o_ref.dtype)


def ssd_mixer(z, xbc, small, dt_t, conv_w, conv_b, dt_bias, a_log, d_skip, norm_w, batch, seq):
    T, Di = z.shape
    Gs, N, P, Hs = SSD_GROUPS, SSD_STATE, SSD_HEAD_DIM, SSD_HEADS
    hpg = Hs // Gs
    GW = hpg * P
    L = math.gcd(seq, SSD_CHUNK)
    nc = seq // L
    HALO = SUBLANES
    nxb = Di // GW
    assert GW % N == 0 and Hs <= LANES
    b_off = Di // N
    c_off = b_off + Gs

    def rows(b, c, g):
        return b * nc + c

    def halo_rows(b, c, g):
        return jnp.maximum((b * seq + c * L) // HALO - 1, 0)

    in_specs = [
        pl.BlockSpec((L, GW), lambda b, c, g: (rows(b, c, g), g)),
        pl.BlockSpec((HALO, GW), lambda b, c, g: (halo_rows(b, c, g), g)),
        pl.BlockSpec((L, N), lambda b, c, g: (rows(b, c, g), b_off + g)),
        pl.BlockSpec((HALO, N), lambda b, c, g: (halo_rows(b, c, g), b_off + g)),
        pl.BlockSpec((L, N), lambda b, c, g: (rows(b, c, g), c_off + g)),
        pl.BlockSpec((HALO, N), lambda b, c, g: (halo_rows(b, c, g), c_off + g)),
        pl.BlockSpec((L, GW), lambda b, c, g: (rows(b, c, g), g)),
        pl.BlockSpec((L, LANES), lambda b, c, g: (rows(b, c, g), 0)),
        pl.BlockSpec((None, Hs, L), lambda b, c, g: (b, 0, c)),
        pl.BlockSpec((SSD_CONV, GW), lambda b, c, g: (0, g)),
        pl.BlockSpec((SSD_CONV, N), lambda b, c, g: (0, b_off + g)),
        pl.BlockSpec((SSD_CONV, N), lambda b, c, g: (0, c_off + g)),
        pl.BlockSpec((1, GW), lambda b, c, g: (0, g)),
        pl.BlockSpec((1, N), lambda b, c, g: (0, b_off + g)),
        pl.BlockSpec((1, N), lambda b, c, g: (0, c_off + g)),
        pl.BlockSpec((1, Hs), lambda b, c, g: (0, 0)),
        pl.BlockSpec((Hs, 1), lambda b, c, g: (0, 0)),
        pl.BlockSpec((1, Hs), lambda b, c, g: (0, 0)),
        pl.BlockSpec((Hs, 1), lambda b, c, g: (0, 0)),
        pl.BlockSpec((1, GW), lambda b, c, g: (0, g)),
        pl.BlockSpec((1, GW), lambda b, c, g: (0, g)),
    ]
    cb2 = conv_b.reshape(1, -1)
    body = functools.partial(_ssd_body, L=L, hpg=hpg, n_groups=Gs)
    return pl.pallas_call(
        body,
        grid=(batch, nc, Gs),
        in_specs=in_specs,
        out_specs=pl.BlockSpec((L, GW), lambda b, c, g: (rows(b, c, g), g)),
        out_shape=jax.ShapeDtypeStruct((T, Di), BF16),
        scratch_shapes=[pltpu.VMEM((Hs * P, N), F32),
                        pltpu.VMEM((HALO + L, GW + 2 * N), F32),
                        pltpu.VMEM((L, GW), F32),
                        pltpu.VMEM((Gs, L, LANES), F32),
                        pltpu.VMEM((Gs, L, LANES), F32),
                        pltpu.VMEM((Hs, L), F32),
                        pltpu.VMEM((Hs, L), F32)],
        compiler_params=_params(("arbitrary", "arbitrary", "arbitrary"), 40),
        name="ssd",
    )(xbc, xbc, xbc, xbc, xbc, xbc, z, small, dt_t,
      conv_w, conv_w, conv_w, cb2, cb2, cb2,
      dt_bias.reshape(1, Hs), dt_bias.reshape(Hs, 1), a_log.reshape(1, Hs), a_log.reshape(Hs, 1),
      jnp.repeat(d_skip, P).reshape(1, Di), norm_w.reshape(1, Di))


def _merge_epilogue(accs, e_refs, o_refs):
    ga, gb = e_refs
    o_refs[0][...] = (ga[...].astype(F32) * accs[0] + gb[...].astype(F32) * accs[1]).astype(o_refs[0].dtype)


def _rope_lane_tables(pos):
    half = ROPE_DIM // 2
    inv_freq = jnp.float32(ROPE_THETA) ** (-jnp.arange(half, dtype=F32) / half)
    ang = pos.astype(F32)[..., None] * inv_freq
    cos, sin = jnp.cos(ang), jnp.sin(ang)
    pad = [(0, 0)] * (cos.ndim - 1)
    one = jnp.ones(cos.shape[:-1] + (LANES - ROPE_DIM,), F32)
    c = jnp.concatenate([cos, cos, one], axis=-1)
    s1 = jnp.pad(-sin, pad + [(0, LANES - half)])
    s2 = jnp.pad(sin, pad + [(half, LANES - ROPE_DIM)])
    return c, s1, s2


def _pad_cols(w, n):
    return jnp.pad(w, ((0, 0), (0, n - w.shape[1])))


def _layer(x, mod, positions, tabs, tabs_c, consts, p):
    B, S = positions.shape
    T, D = x.shape
    H, G, dh = NSA_HEADS, NSA_KV_GROUPS, NSA_HEAD_DIM
    hpg = H // G
    F = p["ffn1_w_gate"].shape[1]
    Fp = -(-F // FFN_PAD) * FFN_PAD
    tm = _tile(TM_MATMUL, S)
    tpb = S // tm

    def ffn(h, x_res, wg, wu, wd, slot_gate):
        a = ffn_up(h, _pad_cols(wg, Fp).astype(BF16), _pad_cols(wu, Fp).astype(BF16))
        wd_p = jnp.pad(wd, ((0, Fp - F), (0, 0))).astype(BF16)
        return ffn_down(a, wd_p, x_res, mod, S, slot_gate, 0.5)

    h = modulate(x, mod, S, 0, 1)
    r = ffn(h, x, p["ffn1_w_gate"], p["ffn1_w_up"], p["ffn1_w_down"], 2)
    x, h = ln_mod(r, p["ln1_g"], p["ln1_b"], mod, S, 3, 4)

    w_in = p["w_in"]
    cuts = np.cumsum(IN_SPLIT_SIZES)[:-1].tolist()
    (w_q, w_kc, w_vc, w_ks, w_vs, w_kw, w_vw, w_gn, w_z, w_xbc, w_dt, w_ga, w_gb) = jnp.split(w_in, cuts, axis=1)
    w_small = jnp.concatenate([_pad_cols(w_dt, LANES), _pad_cols(w_gn, LANES)], axis=1).astype(BF16)
    tab_specs = [pl.BlockSpec((tm, LANES), lambda i, j: (i, 0))] * 3

    tnq = hpg * dh
    q = fused_matmul([h], [w_q.astype(BF16)], [(0, 0)], functools.partial(_q_epilogue, hpg), tabs, tab_specs,
                     jax.ShapeDtypeStruct((B, H, S, dh), BF16),
                     pl.BlockSpec((None, hpg, tm, dh), lambda i, j: (i // tpb, j, i % tpb, 0)),
                     tm, tnq, 52, "in_proj_q")
    tnk = G * dh
    kv_spec = pl.BlockSpec((None, None, G, tm, dh), lambda i, j: (j, i // tpb, 0, i % tpb, 0))
    kvcmp = fused_matmul([h], [jnp.concatenate([w_kc, w_vc], axis=1).astype(BF16)], [(0, 0)],
                         functools.partial(_kv_plain_epilogue, G), [], [],
                         jax.ShapeDtypeStruct((2, B, G, S, dh), F32), kv_spec, tm, tnk, 52, "in_proj_kvcmp")
    sw = fused_matmul([h], [jnp.concatenate([w_ks, w_vs, w_kw, w_vw], axis=1).astype(BF16)], [(0, 0)],
                      functools.partial(_kv_rope_epilogue, G), tabs, tab_specs,
                      jax.ShapeDtypeStruct((4, B, G, S, dh), BF16), kv_spec, tm, tnk, 52, "in_proj_slcwin")
    plain_spec = lambda tn: pl.BlockSpec((tm, tn), lambda i, j: (i, j))
    tnz = _tile(1024, SSD_D_INNER)
    z = fused_matmul([h], [w_z.astype(BF16)], [(0, 0)], _plain_epilogue, [], [],
                     jax.ShapeDtypeStruct((T, SSD_D_INNER), F32), plain_spec(tnz), tm, tnz, 52, "in_proj_z")
    tnx = _tile(1024, SSD_CONV_CH)
    xbc = fused_matmul([h], [w_xbc.astype(BF16)], [(0, 0)], _plain_epilogue, [], [],
                       jax.ShapeDtypeStruct((T, SSD_CONV_CH), F32), plain_spec(tnx), tm, tnx, 52, "in_proj_xbc")
    tng = _tile(1024, 2 * D)
    gab = fused_matmul([h], [jnp.concatenate([w_ga, w_gb], axis=1).astype(BF16)], [(0, 0)], _sigmoid_epilogue,
                       [], [], jax.ShapeDtypeStruct((T, 2 * D), BF16), plain_spec(tng), tm, tng, 52, "in_proj_gates")
    small = fused_matmul([h], [w_small], [(0, 0)], _small_epilogue, [], [],
                         jax.ShapeDtypeStruct((T, 2 * LANES), F32), plain_spec(2 * LANES), tm, 2 * LANES, 52,
                         "in_proj_small")

    n_seg = S // NSA_CMP_STRIDE
    kvseg = kvcmp.reshape(2, B, G, n_seg, NSA_CMP_STRIDE * dh)
    w1 = jnp.stack([p["nsa_cmp_k_w1"], p["nsa_cmp_v_w1"]]).astype(BF16)
    w2 = jnp.stack([p["nsa_cmp_k_w2"], p["nsa_cmp_v_w2"]]).astype(BF16)
    pos8 = jnp.broadcast_to(p["nsa_cmp_pos"].reshape(1, -1), (SUBLANES, NSA_CMP_BLOCK * dh)).astype(BF16)
    kvc = nsa_compress(kvseg, w1, w2, pos8, tabs_c)
    e_blk, ones_col, overlap = consts
    ones4 = jnp.broadcast_to(ones_col, (B, G, S, dh))
    kx = jnp.concatenate([sw[0], jnp.broadcast_to(e_blk, (B, G, S, dh))], axis=-1)

    def chunked_t(v, chunk):
        return jnp.swapaxes(v.reshape(B, G, S // chunk, chunk, 2 * dh), -1, -2)

    vxt = chunked_t(jnp.concatenate([sw[1], ones4], axis=-1), _tile(ATTN_TK, S))
    vwxt = chunked_t(jnp.concatenate([sw[3], ones4], axis=-1), _tile(ATTN_TQ, S))
    vcxt = jnp.swapaxes(jnp.concatenate([kvc[1], jnp.broadcast_to(overlap, (B, G, n_seg, LANES))], axis=-1), -1, -2)
    g3 = small[:, LANES:LANES + 3 * H].reshape(B, S, 3, G, hpg)
    gates_t = jnp.transpose(g3, (0, 3, 2, 4, 1)).reshape(B, G, 3 * hpg, S)
    o_a = nsa_attention(q, kvc, vcxt, kx, vxt, sw, vwxt, gates_t, S)

    dt_t = jnp.transpose(small[:, :SSD_HEADS].reshape(B, S, SSD_HEADS), (0, 2, 1))
    o_b = ssd_mixer(z, xbc, small, dt_t, p["ssd_conv_w"], p["ssd_conv_b"], p["ssd_dt_bias"], p["ssd_a_log"],
                    p["ssd_d"], p["ssd_norm_w"], B, S)

    tm2 = _tile(TM_DUAL, S)
    tn2 = _tile(512, D)
    nga = D // tn2
    merged = fused_matmul([o_a, o_b], [p["w_branch_a"].astype(BF16), p["w_branch_b"].astype(BF16)],
                          [(0, 0), (1, 1)], _merge_epilogue, [gab, gab],
                          [pl.BlockSpec((tm2, tn2), lambda i, j: (i, j)),
                           pl.BlockSpec((tm2, tn2), lambda i, j: (i, nga + j))],
                          jax.ShapeDtypeStruct((T, D), BF16), pl.BlockSpec((tm2, tn2), lambda i, j: (i, j)),
                          tm2, tn2, 52, "branch_merge")
    tno = _tile(512, D)
    r = fused_matmul([merged], [p["w_out"].astype(BF16)], [(0, 0)], functools.partial(_residual_epilogue, 1.0),
                     [x, mod],
                     [pl.BlockSpec((tm, tno), lambda i, j: (i, j)),
                      pl.BlockSpec((None, 1, tno), lambda i, j: ((i // tpb) * N_ADA + 5, 0, j))],
                     jax.ShapeDtypeStruct((T, D), F32), pl.BlockSpec((tm, tno), lambda i, j: (i, j)),
                     tm, tno, 52, "out_proj")
    x, h = ln_mod(r, p["ln2_g"], p["ln2_b"], mod, S, 6, 7)

    r = ffn(h, x, p["ffn2_w_gate"], p["ffn2_w_up"], p["ffn2_w_down"], 8)
    return layer_norm(r, p["ln3_g"], p["ln3_b"])


def kernel(x, c, positions, w_ada, b_ada, ffn1_w_gate, ffn1_w_up, ffn1_w_down, w_in, nsa_cmp_pos, nsa_cmp_k_w1, nsa_cmp_k_w2, nsa_cmp_v_w1, nsa_cmp_v_w2, ssd_conv_w, ssd_conv_b, ssd_dt_bias, ssd_a_log, ssd_d, ssd_norm_w, w_branch_a, w_branch_b, w_out, ffn2_w_gate, ffn2_w_up, ffn2_w_down, ln1_g, ln1_b, ln2_g, ln2_b, ln3_g, ln3_b):
    B, S, D = x.shape
    dh = NSA_HEAD_DIM
    per_layer = dict(ffn1_w_gate=ffn1_w_gate, ffn1_w_up=ffn1_w_up, ffn1_w_down=ffn1_w_down, w_in=w_in,
                     nsa_cmp_pos=nsa_cmp_pos, nsa_cmp_k_w1=nsa_cmp_k_w1, nsa_cmp_k_w2=nsa_cmp_k_w2,
                     nsa_cmp_v_w1=nsa_cmp_v_w1, nsa_cmp_v_w2=nsa_cmp_v_w2, ssd_conv_w=ssd_conv_w,
                     ssd_conv_b=ssd_conv_b, ssd_dt_bias=ssd_dt_bias, ssd_a_log=ssd_a_log, ssd_d=ssd_d,
                     ssd_norm_w=ssd_norm_w, w_branch_a=w_branch_a, w_branch_b=w_branch_b, w_out=w_out,
                     ffn2_w_gate=ffn2_w_gate, ffn2_w_up=ffn2_w_up, ffn2_w_down=ffn2_w_down,
                     ln1_g=ln1_g, ln1_b=ln1_b, ln2_g=ln2_g, ln2_b=ln2_b, ln3_g=ln3_g, ln3_b=ln3_b)

    tabs = [t.reshape(B * S, LANES) for t in _rope_lane_tables(positions)]
    n_seg = S // NSA_CMP_STRIDE
    c_end = jnp.minimum(NSA_CMP_STRIDE * jnp.arange(n_seg) + NSA_CMP_BLOCK - 1, S - 1)
    ck, s1k, s2k = _rope_lane_tables(positions[:, c_end])
    tabs_c = [jnp.stack([ck, jnp.ones_like(ck)]), jnp.stack([s1k, jnp.zeros_like(s1k)]),
              jnp.stack([s2k, jnp.zeros_like(s2k)])]

    key_blk = np.arange(S) // NSA_SEL_BLOCK
    e_blk = jnp.asarray(key_blk[:, None] == np.arange(LANES)[None, :], BF16)
    ones_col = jnp.asarray(np.arange(dh)[None, :] == 0, BF16) * jnp.ones((S, 1), BF16)
    n_cmp = (S - NSA_CMP_BLOCK) // NSA_CMP_STRIDE + 1
    c_start = NSA_CMP_STRIDE * np.arange(n_seg)
    sel_start = NSA_SEL_BLOCK * np.arange(LANES)
    ov = ((c_start[:, None] < sel_start[None, :] + NSA_SEL_BLOCK)
          & (c_start[:, None] + NSA_CMP_BLOCK - 1 >= sel_start[None, :])
          & (np.arange(n_seg)[:, None] < n_cmp) & (np.arange(LANES)[None, :] < S // NSA_SEL_BLOCK))
    consts = (e_blk, ones_col, jnp.asarray(ov, BF16))

    xt = x.reshape(B * S, D)
    for l in range(DEPTH):
        mod = ada_proj(c, w_ada[l], b_ada[l]).reshape(B * N_ADA, 1, D)
        xt = _layer(xt, mod, positions, tabs, tabs_c, consts, {k: v[l] for k, v in per_layer.items()})
    return xt.reshape(B, S, D)
```

```python
import functools
import math

import jax
import jax.numpy as jnp
import numpy as np
from jax import lax
from jax.experimental import pallas as pl
from jax.experimental.pallas import tpu as pltpu

D_MODEL = 4096
BATCH = 2
SEQ = 8192
DEPTH = 1

NSA_HEADS = 32
NSA_KV_GROUPS = 4
NSA_HEAD_DIM = 128
NSA_CMP_BLOCK = 32
NSA_CMP_STRIDE = 16
NSA_CMP_HIDDEN = 256
NSA_SEL_BLOCK = 64
NSA_SEL_TOPN = 16
NSA_WINDOW = 512
ROPE_THETA = 500000.0
ROPE_DIM = NSA_HEAD_DIM // 4
FORCE_BONUS = 1.0e4
NEG_INF = -1.0e30

SSD_D_INNER = D_MODEL
SSD_HEAD_DIM = 64
SSD_HEADS = SSD_D_INNER // SSD_HEAD_DIM
SSD_GROUPS = 8
SSD_STATE = 128
SSD_CONV = 4
SSD_CHUNK = 256
SSD_CONV_CH = SSD_D_INNER + 2 * SSD_GROUPS * SSD_STATE

FFN_DIM = 256 * ((8 * D_MODEL // 3 + 255) // 256)
N_ADA = 9
LN_EPS = 1e-5
RMS_EPS = 1e-5
DEEPNORM_ALPHA = (2 * DEPTH) ** 0.25

NSA_Q_WIDTH = NSA_HEADS * NSA_HEAD_DIM
NSA_KV_WIDTH = NSA_KV_GROUPS * NSA_HEAD_DIM
IN_SPLIT_SIZES = (NSA_Q_WIDTH,) + (NSA_KV_WIDTH,) * 6 + (3 * NSA_HEADS, SSD_D_INNER, SSD_CONV_CH, SSD_HEADS, D_MODEL, D_MODEL)

F32 = jnp.float32
BF16 = jnp.bfloat16
LANES = 128
SUBLANES = 8
LOG2E = 1.4426950408889634
MIB = 1024 * 1024

TM_MATMUL = 1024
TM_DUAL = 512
TM_ROWWISE = 256
ATTN_TQ = 128
ATTN_TK = 512
FFN_PAD = 512
FFN_KSPLIT = 4


def _tile(pref, dim):
    t = min(pref, dim)
    assert dim % t == 0, (pref, dim)
    return t


def _params(sem, vmem_mib):
    return pltpu.CompilerParams(dimension_semantics=sem, vmem_limit_bytes=vmem_mib * MIB)


def _sigmoid(x):
    return 1.0 / (1.0 + jnp.exp(-x))


def _silu(x):
    return x * _sigmoid(x)


def _nt_dot(a, b):
    return lax.dot_general(a, b, (((1,), (1,)), ((), ())), preferred_element_type=F32)


def _ada_body(c_ref, w_ref, b_ref, o_ref):
    c = c_ref[...]
    ca = _silu(c).astype(BF16)
    o_ref[...] = jnp.dot(ca, w_ref[...].astype(BF16), preferred_element_type=F32) + b_ref[...]


def ada_proj(c, w_ada, b_ada):
    B, D = c.shape
    N = w_ada.shape[1]
    tn = _tile(512, N)
    c8 = jnp.zeros((SUBLANES, D), F32).at[:B].set(c)
    out = pl.pallas_call(
        _ada_body,
        grid=(N // tn,),
        in_specs=[pl.BlockSpec((SUBLANES, D), lambda j: (0, 0)),
                  pl.BlockSpec((D, tn), lambda j: (0, j)),
                  pl.BlockSpec((1, tn), lambda j: (0, j))],
        out_specs=pl.BlockSpec((SUBLANES, tn), lambda j: (0, j)),
        out_shape=jax.ShapeDtypeStruct((SUBLANES, N), F32),
        compiler_params=_params(("arbitrary",), 40),
        name="ada_proj",
    )(c8, w_ada, b_ada.reshape(1, N))
    return out[:B]


def _mod_body(x_ref, shift_ref, scale_ref, h_ref):
    h_ref[...] = (x_ref[...] * (1.0 + scale_ref[...]) + shift_ref[...]).astype(h_ref.dtype)


def _ln_mod_body(r_ref, g_ref, b_ref, shift_ref, scale_ref, x_ref, h_ref):
    r = r_ref[...]
    mu = jnp.mean(r, axis=-1, keepdims=True)
    d = r - mu
    var = jnp.mean(d * d, axis=-1, keepdims=True)
    x = d * lax.rsqrt(var + LN_EPS) * g_ref[...] + b_ref[...]
    x_ref[...] = x
    h_ref[...] = (x * (1.0 + scale_ref[...]) + shift_ref[...]).astype(h_ref.dtype)


def _ln_body(r_ref, g_ref, b_ref, x_ref):
    r = r_ref[...]
    mu = jnp.mean(r, axis=-1, keepdims=True)
    d = r - mu
    var = jnp.mean(d * d, axis=-1, keepdims=True)
    x_ref[...] = d * lax.rsqrt(var + LN_EPS) * g_ref[...] + b_ref[...]


def _mod_spec(D, tiles_per_batch, slot):
    return pl.BlockSpec((None, 1, D), lambda i: ((i // tiles_per_batch) * N_ADA + slot, 0, 0))


def modulate(x, mod, seq, slot_shift, slot_scale):
    T, D = x.shape
    tm = _tile(TM_ROWWISE, seq)
    tpb = seq // tm
    return pl.pallas_call(
        _mod_body,
        grid=(T // tm,),
        in_specs=[pl.BlockSpec((tm, D), lambda i: (i, 0)),
                  _mod_spec(D, tpb, slot_shift), _mod_spec(D, tpb, slot_scale)],
        out_specs=pl.BlockSpec((tm, D), lambda i: (i, 0)),
        out_shape=jax.ShapeDtypeStruct((T, D), BF16),
        compiler_params=_params(("parallel",), 32),
        name="modulate",
    )(x, mod, mod)


def ln_mod(r, g, b, mod, seq, slot_shift, slot_scale):
    T, D = r.shape
    tm = _tile(TM_ROWWISE, seq)
    tpb = seq // tm
    row = pl.BlockSpec((tm, D), lambda i: (i, 0))
    vec = pl.BlockSpec((1, D), lambda i: (0, 0))
    return pl.pallas_call(
        _ln_mod_body,
        grid=(T // tm,),
        in_specs=[row, vec, vec, _mod_spec(D, tpb, slot_shift), _mod_spec(D, tpb, slot_scale)],
        out_specs=[row, row],
        out_shape=[jax.ShapeDtypeStruct((T, D), F32), jax.ShapeDtypeStruct((T, D), BF16)],
        compiler_params=_params(("parallel",), 40),
        name="ln_mod",
    )(r, g.reshape(1, D), b.reshape(1, D), mod, mod)


def layer_norm(r, g, b):
    T, D = r.shape
    tm = _tile(TM_ROWWISE, T)
    row = pl.BlockSpec((tm, D), lambda i: (i, 0))
    vec = pl.BlockSpec((1, D), lambda i: (0, 0))
    return pl.pallas_call(
        _ln_body,
        grid=(T // tm,),
        in_specs=[row, vec, vec],
        out_specs=row,
        out_shape=jax.ShapeDtypeStruct((T, D), F32),
        compiler_params=_params(("parallel",), 32),
        name="layer_norm",
    )(r, g.reshape(1, D), b.reshape(1, D))


def _fused_matmul_body(nx, nw, ne, pairs, epilogue, *refs):
    x_refs = refs[:nx]
    w_refs = refs[nx:nx + nw]
    e_refs = refs[nx + nw:nx + nw + ne]
    o_refs = refs[nx + nw + ne:]
    accs = [jnp.dot(x_refs[a][...], w_refs[b][...], preferred_element_type=F32) for a, b in pairs]
    epilogue(accs, e_refs, o_refs)


def fused_matmul(xs, ws, pairs, epilogue, extras, extra_specs, out_shapes, out_specs, tm, tn, vmem_mib, name,
                 w_cols=None):
    M = xs[0].shape[0]
    col0, N = w_cols if w_cols is not None else (0, ws[0].shape[1])
    assert col0 % tn == 0 and N % tn == 0 and M % tm == 0
    j0 = col0 // tn
    in_specs = ([pl.BlockSpec((tm, x.shape[1]), lambda i, j: (i, 0)) for x in xs]
                + [pl.BlockSpec((w.shape[0], tn), lambda i, j: (0, j0 + j)) for w in ws]
                + list(extra_specs))
    body = functools.partial(_fused_matmul_body, len(xs), len(ws), len(extras), tuple(pairs), epilogue)
    return pl.pallas_call(
        body,
        grid=(M // tm, N // tn),
        in_specs=in_specs,
        out_specs=out_specs,
        out_shape=out_shapes,
        compiler_params=_params(("parallel", "arbitrary"), vmem_mib),
        name=name,
    )(*xs, *ws, *extras)


def _swiglu_epilogue(accs, e_refs, o_refs):
    g, u = accs
    o_refs[0][...] = (_silu(g) * u).astype(o_refs[0].dtype)


def ffn_up(h, wg, wu):
    T, D = h.shape
    Fp = wg.shape[1]
    tm, tn = _tile(TM_MATMUL, T), _tile(FFN_PAD, Fp)
    return fused_matmul([h], [wg, wu], [(0, 0), (0, 1)], _swiglu_epilogue, [], [],
                        jax.ShapeDtypeStruct((T, Fp), BF16), pl.BlockSpec((tm, tn), lambda i, j: (i, j)),
                        tm, tn, 52, "ffn_up")


def _residual_epilogue(coef, accs, e_refs, o_refs):
    x_ref, gate_ref = e_refs
    o_refs[0][...] = DEEPNORM_ALPHA * x_ref[...] + (coef * gate_ref[...]) * accs[0]


def _ffn_down_body(coef, a_ref, w_ref, x_ref, gate_ref, o_ref, acc_ref):
    k = pl.program_id(2)
    part = jnp.dot(a_ref[...], w_ref[...], preferred_element_type=F32)

    @pl.when(k == 0)
    def _():
        acc_ref[...] = part

    @pl.when(k > 0)
    def _():
        acc_ref[...] += part

    @pl.when(k == pl.num_programs(2) - 1)
    def _():
        o_ref[...] = DEEPNORM_ALPHA * x_ref[...] + (coef * gate_ref[...]) * acc_ref[...]


def ffn_down(a, wd, x, mod, seq, slot_gate, coef):
    T, Fp = a.shape
    D = wd.shape[1]
    tm, tn = _tile(TM_MATMUL, seq), _tile(1024, D)
    tk = Fp // FFN_KSPLIT
    assert Fp % FFN_KSPLIT == 0 and tk % LANES == 0
    tpb = seq // tm
    return pl.pallas_call(
        functools.partial(_ffn_down_body, coef),
        grid=(T // tm, D // tn, FFN_KSPLIT),
        in_specs=[pl.BlockSpec((tm, tk), lambda i, j, k: (i, k)),
                  pl.BlockSpec((tk, tn), lambda i, j, k: (k, j)),
                  pl.BlockSpec((tm, tn), lambda i, j, k: (i, j)),
                  pl.BlockSpec((None, 1, tn), lambda i, j, k: ((i // tpb) * N_ADA + slot_gate, 0, j))],
        out_specs=pl.BlockSpec((tm, tn), lambda i, j, k: (i, j)),
        out_shape=jax.ShapeDtypeStruct((T, D), F32),
        scratch_shapes=[pltpu.VMEM((tm, tn), F32)],
        compiler_params=_params(("parallel", "arbitrary", "arbitrary"), 52),
        name="ffn_down",
    )(a, wd, x, mod)


def _rope(t, c, s1, s2):
    half = ROPE_DIM // 2
    return t * c + pltpu.roll(t, LANES - half, axis=1) * s1 + pltpu.roll(t, half, axis=1) * s2


def _q_epilogue(n_heads, accs, e_refs, o_refs):
    c, s1, s2 = (r[...] for r in e_refs)
    qscale = NSA_HEAD_DIM ** -0.5 * LOG2E
    for h in range(n_heads):
        t = accs[0][:, h * LANES:(h + 1) * LANES]
        o_refs[0][h] = (_rope(t, c, s1, s2) * qscale).astype(o_refs[0].dtype)


def _kv_plain_epilogue(n_groups, accs, e_refs, o_refs):
    for g in range(n_groups):
        o_refs[0][g] = accs[0][:, g * LANES:(g + 1) * LANES].astype(o_refs[0].dtype)


def _kv_rope_epilogue(n_groups, accs, e_refs, o_refs):
    c, s1, s2 = (r[...] for r in e_refs)
    is_key = pl.program_id(1) % 2 == 0

    @pl.when(is_key)
    def _():
        for g in range(n_groups):
            t = accs[0][:, g * LANES:(g + 1) * LANES]
            o_refs[0][g] = _rope(t, c, s1, s2).astype(o_refs[0].dtype)

    @pl.when(jnp.logical_not(is_key))
    def _():
        for g in range(n_groups):
            o_refs[0][g] = accs[0][:, g * LANES:(g + 1) * LANES].astype(o_refs[0].dtype)


def _plain_epilogue(accs, e_refs, o_refs):
    o_refs[0][...] = accs[0].astype(o_refs[0].dtype)


def _sigmoid_epilogue(accs, e_refs, o_refs):
    o_refs[0][...] = _sigmoid(accs[0]).astype(o_refs[0].dtype)


def _small_epilogue(accs, e_refs, o_refs):
    a = accs[0]
    lane = lax.broadcasted_iota(jnp.int32, a.shape, 1)
    o_refs[0][...] = jnp.where(lane >= LANES, _sigmoid(a), a)


def _compress_body(seg_ref, w1_ref, w2_ref, pos_ref, c_ref, s1_ref, s2_ref, o_ref):
    n_seg = seg_ref.shape[0]
    half_k = seg_ref.shape[1]
    seg = seg_ref[...].astype(BF16)
    top = jnp.dot(seg, w1_ref[0:half_k, :], preferred_element_type=F32)
    bot = jnp.dot(seg, w1_ref[half_k:2 * half_k, :], preferred_element_type=F32)
    cb = jnp.dot(pos_ref[...], w1_ref[...], preferred_element_type=F32)[0:1, :]
    hid = _silu(top + pltpu.roll(bot, n_seg - 1, axis=0) + cb).astype(BF16)
    out = jnp.dot(hid, w2_ref[...], preferred_element_type=F32)
    out = _rope(out, c_ref[...], s1_ref[...], s2_ref[...])
    rowi = lax.broadcasted_iota(jnp.int32, out.shape, 0)
    o_ref[...] = jnp.where(rowi < n_seg - 1, out, 0.0).astype(o_ref.dtype)


def nsa_compress(kvseg, w1, w2, pos8, tabs):
    _, B, G, n_seg, K2 = kvseg.shape
    dh = NSA_HEAD_DIM
    hid = w1.shape[-1]
    tab_spec = pl.BlockSpec((None, None, n_seg, dh), lambda kv, b, g: (kv, b, 0, 0))
    return pl.pallas_call(
        _compress_body,
        grid=(2, B, G),
        in_specs=[pl.BlockSpec((None, None, None, n_seg, K2), lambda kv, b, g: (kv, b, g, 0, 0)),
                  pl.BlockSpec((None, 2 * K2, hid), lambda kv, b, g: (kv, 0, 0)),
                  pl.BlockSpec((None, hid, dh), lambda kv, b, g: (kv, 0, 0)),
                  pl.BlockSpec((SUBLANES, 2 * K2), lambda kv, b, g: (0, 0)),
                  tab_spec, tab_spec, tab_spec],
        out_specs=pl.BlockSpec((None, None, None, n_seg, dh), lambda kv, b, g: (kv, b, g, 0, 0)),
        out_shape=jax.ShapeDtypeStruct((2, B, G, n_seg, dh), BF16),
        compiler_params=_params(("arbitrary", "arbitrary", "arbitrary"), 40),
        name="nsa_compress",
    )(kvseg, w1, w2, pos8, *tabs)


def _attn_body(q_ref, kc_ref, vcxt_ref, kx_ref, vxt_ref, kw_ref, vwxt_ref, gate_ref, o_ref,
               qxt_ref, s0_ref, s1_ref, mt0_ref, mt1_ref, m_ref, acc_ref, out_ref, *, tq, tk, wk, hpg, top_n):
    dh = NSA_HEAD_DIM
    R = hpg * tq
    hw = R // 2
    n_cp = kc_ref.shape[0]
    t0 = pl.program_id(2) * tq
    t_q = t0 + lax.broadcasted_iota(jnp.int32, (1, tq), 1)
    for h in range(hpg):
        qxt_ref[0:dh, h * tq:(h + 1) * tq] = q_ref[h].astype(F32).T.astype(BF16)
    qt = qxt_ref[0:dh, :]
    gates = gate_ref[...]

    def gate_row(branch):
        return jnp.concatenate([gates[branch * hpg + h:branch * hpg + h + 1, :] for h in range(hpg)], axis=1)

    def all_heads(x):
        return jnp.concatenate([x] * hpg, axis=1)

    c_end = lax.broadcasted_iota(jnp.int32, (n_cp, 1), 0) * NSA_CMP_STRIDE + (NSA_CMP_BLOCK - 1)
    s_c = jnp.dot(kc_ref[...], qt, preferred_element_type=F32) + all_heads(jnp.where(c_end <= t_q, 0.0, NEG_INF))
    mx_c = jnp.max(s_c, axis=0, keepdims=True)
    vcxt = vcxt_ref[...]
    g_c = gate_row(0)
    imp_t = jnp.zeros((LANES, tq), F32)
    for half in range(2):
        cols = slice(half * hw, (half + 1) * hw)
        e = jnp.exp2(s_c[:, cols] - mx_c[:, cols])
        inv = jnp.where(mx_c[:, cols] > 0.5 * NEG_INF, 1.0 / jnp.sum(e, axis=0, keepdims=True), 0.0)
        r = jnp.dot(vcxt, e.astype(BF16), preferred_element_type=F32)
        out_ref[:, cols] = r[0:dh, :] * (g_c[:, cols] * inv)
        ri = r[dh:2 * dh, :] * inv
        for h in range(hpg // 2):
            imp_t = imp_t + ri[:, h * tq:(h + 1) * tq]

    w0 = pl.multiple_of(jnp.maximum(t0 + tq - wk, 0), tq)
    c0 = w0 // tq
    kpos_w = w0 + lax.broadcasted_iota(jnp.int32, (wk, 1), 0)
    w_bias = jnp.where((kpos_w <= t_q) & (kpos_w > t_q - NSA_WINDOW), 0.0, NEG_INF)
    s_w = jnp.dot(kw_ref[pl.ds(w0, wk), :], qt, preferred_element_type=F32) + all_heads(w_bias)
    mx_w = jnp.max(s_w, axis=0, keepdims=True)

    j_i = lax.broadcasted_iota(jnp.int32, (LANES, 1), 0)
    j_f = j_i.astype(F32)
    cur = t_q // NSA_SEL_BLOCK
    causal_blk = j_i * NSA_SEL_BLOCK <= t_q
    forced = (j_i == 0) | (j_i == cur) | (j_i == cur - 1)
    work = jnp.where(causal_blk, imp_t + jnp.where(forced, FORCE_BONUS, 0.0), NEG_INF)
    sel = jnp.zeros(work.shape, jnp.bool_)
    for _ in range(top_n):
        mx = jnp.max(work, axis=0, keepdims=True)
        first = jnp.min(jnp.where(work == mx, j_f, float(LANES)), axis=0, keepdims=True)
        pick = j_f == first
        sel = sel | pick
        work = jnp.where(pick, -jnp.inf, work)
    bias_t = jnp.where(sel & causal_blk, 0.0, NEG_INF).astype(BF16)
    for h in range(hpg):
        qxt_ref[dh:2 * dh, h * tq:(h + 1) * tq] = bias_t

    s_refs = (s0_ref, s1_ref)
    mt_refs = (mt0_ref, mt1_ref)

    def score_tile(kt, slot):
        k0 = pl.multiple_of(kt * tk, tk)
        kpos = k0 + lax.broadcasted_iota(jnp.int32, (tk, 1), 0)
        s = jnp.dot(kx_ref[pl.ds(k0, tk), :], qxt_ref[...], preferred_element_type=F32)
        s = s + all_heads(jnp.where(kpos <= t_q, 0.0, NEG_INF))
        s_refs[slot][...] = s
        mt_refs[slot][...] = jnp.max(s, axis=0, keepdims=True)

    def softmax_pv(kt, slot):
        vt = vxt_ref[kt]
        for half in range(2):
            cols = slice(half * hw, (half + 1) * hw)
            m_old = m_ref[:, cols]
            m_new = jnp.maximum(m_old, mt_refs[slot][:, cols])
            p = jnp.exp2(s_refs[slot][:, cols] - m_new).astype(BF16)
            pv = jnp.dot(vt, p, preferred_element_type=F32)
            acc_ref[:, cols] = jnp.exp2(m_old - m_new) * acc_ref[:, cols] + pv
            m_ref[:, cols] = m_new

    score_tile(0, 0)

    vwt = jnp.concatenate([vwxt_ref[c0 + i] for i in range(wk // tq)], axis=1)
    g_w = gate_row(2)
    for half in range(2):
        cols = slice(half * hw, (half + 1) * hw)
        e = jnp.exp2(s_w[:, cols] - mx_w[:, cols]).astype(BF16)
        r = jnp.dot(vwt, e, preferred_element_type=F32)
        inv = jnp.where(mx_w[:, cols] > 0.5 * NEG_INF, 1.0 / r[dh:dh + 1, :], 0.0)
        out_ref[:, cols] += r[0:dh, :] * (g_w[:, cols] * inv)

    m_ref[...] = jnp.full(m_ref.shape, NEG_INF, F32)
    acc_ref[...] = jnp.zeros(acc_ref.shape, F32)
    n_full = t0 // tk

    def two_tiles(i, carry):
        score_tile(2 * i + 1, 1)
        softmax_pv(2 * i, 0)
        score_tile(2 * i + 2, 0)
        softmax_pv(2 * i + 1, 1)
        return carry

    lax.fori_loop(0, n_full // 2, two_tiles, 0)

    @pl.when(n_full % 2 == 0)
    def _():
        softmax_pv(n_full, 0)

    @pl.when(n_full % 2 == 1)
    def _():
        score_tile(n_full, 1)
        softmax_pv(n_full - 1, 0)
        softmax_pv(n_full, 1)

    for h in range(hpg):
        cols = slice(h * tq, (h + 1) * tq)
        acc = acc_ref[:, cols]
        l_s = acc[dh:dh + 1, :]
        ok = (m_ref[:, cols] > 0.5 * NEG_INF) & (l_s > 0.0)
        o_t = out_ref[:, cols] + acc[0:dh, :] * (gates[hpg + h:hpg + h + 1, :] * jnp.where(ok, 1.0 / l_s, 0.0))
        o_ref[:, h * dh:(h + 1) * dh] = o_t.T.astype(o_ref.dtype)


def nsa_attention(q, kvc, vcxt, kx, vxt, sw, vwxt, gates_t, seq):
    B, H, S, dh = q.shape
    G = kx.shape[1]
    hpg = H // G
    n_cp = kvc.shape[3]
    tq = _tile(ATTN_TQ, S)
    tk = _tile(ATTN_TK, S)
    wk = min(NSA_WINDOW + tq, S)
    assert tk % tq == 0 and NSA_WINDOW % tq == 0 and tq % NSA_SEL_BLOCK == 0 and S // NSA_SEL_BLOCK <= LANES
    assert hpg % 2 == 0 and wk % tq == 0 and tq == LANES
    top_n = min(NSA_SEL_TOPN, S // NSA_SEL_BLOCK)
    R = hpg * tq
    nq = S // tq
    body = functools.partial(_attn_body, tq=tq, tk=tk, wk=wk, hpg=hpg, top_n=top_n)
    per_group = lambda *blk: pl.BlockSpec((None, None) + blk, lambda b, g, i: (b, g) + (0,) * len(blk))
    return pl.pallas_call(
        body,
        grid=(B, G, nq),
        in_specs=[pl.BlockSpec((None, hpg, tq, dh), lambda b, g, i: (b, g, i, 0)),
                  pl.BlockSpec((None, None, None, n_cp, dh), lambda b, g, i: (0, b, g, 0, 0)),
                  per_group(2 * dh, n_cp),
                  per_group(S, 2 * dh), per_group(S // tk, 2 * dh, tk),
                  pl.BlockSpec((None, None, None, S, dh), lambda b, g, i: (2, b, g, 0, 0)),
                  per_group(S // tq, 2 * dh, tq),
                  pl.BlockSpec((None, None, 3 * hpg, tq), lambda b, g, i: (b, g, 0, i))],
        out_specs=pl.BlockSpec((tq, hpg * dh), lambda b, g, i: (b * nq + i, g)),
        out_shape=jax.ShapeDtypeStruct((B * S, H * dh), BF16),
        scratch_shapes=[pltpu.VMEM((2 * dh, R), BF16),
                        pltpu.VMEM((tk, R), F32),
                        pltpu.VMEM((tk, R), F32),
                        pltpu.VMEM((1, R), F32),
                        pltpu.VMEM((1, R), F32),
                        pltpu.VMEM((1, R), F32),
                        pltpu.VMEM((2 * dh, R), F32),
                        pltpu.VMEM((dh, R), F32)],
        compiler_params=_params(("parallel", "parallel", "arbitrary"), 56),
        name="nsa_attention",
    )(q, kvc, vcxt, kx, vxt, sw, vwxt, gates_t)


def _split3(x):
    hi = x.astype(BF16)
    r = x - hi.astype(F32)
    mid = r.astype(BF16)
    lo = (r - mid.astype(F32)).astype(BF16)
    return hi, mid, lo


def _softplus(x):
    return jnp.maximum(x, 0.0) + jnp.log1p(jnp.exp(-jnp.abs(x)))


def _ssd_body(x_ref, xh_ref, b_ref, bh_ref, c_ref, ch_ref, z_ref, dt_ref, dtt_ref,
              wx_ref, wb_ref, wc_ref, cbx_ref, cbb_ref, cbc_ref,
              dtb_ref, dtbt_ref, alog_ref, alogt_ref, dsk_ref, nw_ref, o_ref,
              state_ref, pad_ref, y_ref, acum_ref, dtg_ref, acumt_ref, dtt_s_ref, *, L, hpg, n_groups):
    P, N = SSD_HEAD_DIM, SSD_STATE
    GW = hpg * P
    HALO = SUBLANES
    c = pl.program_id(1)
    g = pl.program_id(2)
    n_heads = hpg * n_groups

    @pl.when((c == 0) & (g == 0))
    def _():
        state_ref[...] = jnp.zeros(state_ref.shape, F32)

    ri = lax.broadcasted_iota(jnp.int32, (L, L), 0)
    ci = lax.broadcasted_iota(jnp.int32, (L, L), 1)
    tril = ri >= ci

    @pl.when(g == 0)
    def _():
        dt = _softplus(dt_ref[:, 0:n_heads] + dtb_ref[...])
        da = dt * (-jnp.exp(alog_ref[...]))
        dtt = _softplus(dtt_ref[...] + dtbt_ref[...])
        dat = dtt * (-jnp.exp(alogt_ref[...]))
        lower = tril.astype(BF16)
        upper = (ri <= ci).astype(BF16)
        acum = sum(jnp.dot(lower, part, preferred_element_type=F32) for part in _split3(da))
        acumt_ref[...] = sum(jnp.dot(part, upper, preferred_element_type=F32) for part in _split3(dat))
        dtt_s_ref[...] = dtt
        for gg in range(n_groups):
            acum_ref[gg, :, 0:hpg] = acum[:, gg * hpg:(gg + 1) * hpg]
            dtg_ref[gg, :, 0:hpg] = dt[:, gg * hpg:(gg + 1) * hpg]

    first = c == 0
    pad_ref[0:HALO, 0:GW] = jnp.where(first, 0.0, xh_ref[...])
    pad_ref[0:HALO, GW:GW + N] = jnp.where(first, 0.0, bh_ref[...])
    pad_ref[0:HALO, GW + N:GW + 2 * N] = jnp.where(first, 0.0, ch_ref[...])
    pad_ref[HALO:HALO + L, 0:GW] = x_ref[...]
    pad_ref[HALO:HALO + L, GW:GW + N] = b_ref[...]
    pad_ref[HALO:HALO + L, GW + N:GW + 2 * N] = c_ref[...]

    def conv(lo, hi, w_ref, bias_ref):
        out = bias_ref[...]
        for k in range(SSD_CONV):
            off = HALO - (SSD_CONV - 1) + k
            out = out + w_ref[k:k + 1, :] * pad_ref[off:off + L, lo:hi]
        return _silu(out)

    xa = conv(0, GW, wx_ref, cbx_ref)
    bg = conv(GW, GW + N, wb_ref, cbb_ref).astype(BF16)
    cg = conv(GW + N, GW + 2 * N, wc_ref, cbc_ref).astype(BF16)
    xat = xa.T
    cb = _nt_dot(cg, bg)
    head_lane = lax.broadcasted_iota(jnp.int32, (1, LANES), 1) < hpg
    acg = jnp.where(head_lane, acum_ref[g], 0.0)
    dtg = jnp.where(head_lane, dtg_ref[g], 0.0)
    h0 = pl.multiple_of(g * hpg, hpg)
    act = acumt_ref[pl.ds(h0, hpg), :]
    dtt = dtt_s_ref[pl.ds(h0, hpg), :]
    dsk = dsk_ref[...]
    expand = (lax.broadcasted_iota(jnp.int32, (LANES, GW), 1) // P
              == lax.broadcasted_iota(jnp.int32, (LANES, GW), 0)).astype(BF16)
    dtx = sum(jnp.dot(part, expand, preferred_element_type=F32) for part in _split3(dtg))
    ecx = sum(jnp.dot(part, expand, preferred_element_type=F32) for part in _split3(jnp.exp(acg)))
    xdt_all = xa * dtx
    first_lane = lax.broadcasted_iota(jnp.int32, (1, 2 * P), 1) < P
    first_row = lax.broadcasted_iota(jnp.int32, (2 * P, 1), 0) < P
    for hp in range(hpg // 2):
        cols = slice(hp * 2 * P, (hp + 1) * 2 * P)
        xdt = xdt_all[:, cols]
        y = jnp.zeros((L, 2 * P), F32)
        w_rows = []
        for k in range(2):
            h = 2 * hp + k
            col = acg[:, h:h + 1]
            row = act[h:h + 1, :]
            decay = jnp.where(tril, jnp.exp(col - row), 0.0)
            own = first_lane if k == 0 else jnp.logical_not(first_lane)
            y = y + jnp.dot((cb * decay).astype(BF16), jnp.where(own, xdt, 0.0).astype(BF16),
                            preferred_element_type=F32)
            a_last = act[h:h + 1, L - 1:L]
            w_rows.append((dtt[h:h + 1, :] * jnp.exp(a_last - row), jnp.exp(a_last)))
        s0 = pl.multiple_of((g * hpg + 2 * hp) * P, 2 * P)
        st = state_ref[pl.ds(s0, 2 * P), :]
        y = y + _nt_dot(cg, st.astype(BF16)) * ecx[:, cols]
        xw_t = (xat[hp * 2 * P:(hp + 1) * 2 * P, :] * jnp.where(first_row, w_rows[0][0], w_rows[1][0])).astype(BF16)
        state_ref[pl.ds(s0, 2 * P), :] = (st * jnp.where(first_row, w_rows[0][1], w_rows[1][1])
                                          + jnp.dot(xw_t, bg, preferred_element_type=F32))
        y_ref[:, cols] = y + dsk[:, cols] * xa[:, cols]

    yg = y_ref[...] * _silu(z_ref[...])
    ms = jnp.mean(yg * yg, axis=-1, keepdims=True)
    o_ref[...] = (yg * lax.rsqrt(ms + RMS_EPS) * nw_ref[...]).astype(o_ref.dtype)


def ssd_mixer(z, xbc, small, dt_t, conv_w, conv_b, dt_bias, a_log, d_skip, norm_w, batch, seq):
    T, Di = z.shape
    Gs, N, P, Hs = SSD_GROUPS, SSD_STATE, SSD_HEAD_DIM, SSD_HEADS
    hpg = Hs // Gs
    GW = hpg * P
    L = math.gcd(seq, SSD_CHUNK)
    nc = seq // L
    HALO = SUBLANES
    assert GW % N == 0 and Hs <= LANES and hpg % 2 == 0 and 2 * P == LANES
    b_off = Di // N
    c_off = b_off + Gs

    def rows(b, c, g):
        return b * nc + c

    def halo_rows(b, c, g):
        return jnp.maximum((b * seq + c * L) // HALO - 1, 0)

    in_specs = [
        pl.BlockSpec((L, GW), lambda b, c, g: (rows(b, c, g), g)),
        pl.BlockSpec((HALO, GW), lambda b, c, g: (halo_rows(b, c, g), g)),
        pl.BlockSpec((L, N), lambda b, c, g: (rows(b, c, g), b_off + g)),
        pl.BlockSpec((HALO, N), lambda b, c, g: (halo_rows(b, c, g), b_off + g)),
        pl.BlockSpec((L, N), lambda b, c, g: (rows(b, c, g), c_off + g)),
        pl.BlockSpec((HALO, N), lambda b, c, g: (halo_rows(b, c, g), c_off + g)),
        pl.BlockSpec((L, GW), lambda b, c, g: (rows(b, c, g), g)),
        pl.BlockSpec((L, LANES), lambda b, c, g: (rows(b, c, g), 0)),
        pl.BlockSpec((None, Hs, L), lambda b, c, g: (b, 0, c)),
        pl.BlockSpec((SSD_CONV, GW), lambda b, c, g: (0, g)),
        pl.BlockSpec((SSD_CONV, N), lambda b, c, g: (0, b_off + g)),
        pl.BlockSpec((SSD_CONV, N), lambda b, c, g: (0, c_off + g)),
        pl.BlockSpec((1, GW), lambda b, c, g: (0, g)),
        pl.BlockSpec((1, N), lambda b, c, g: (0, b_off + g)),
        pl.BlockSpec((1, N), lambda b, c, g: (0, c_off + g)),
        pl.BlockSpec((1, Hs), lambda b, c, g: (0, 0)),
        pl.BlockSpec((Hs, 1), lambda b, c, g: (0, 0)),
        pl.BlockSpec((1, Hs), lambda b, c, g: (0, 0)),
        pl.BlockSpec((Hs, 1), lambda b, c, g: (0, 0)),
        pl.BlockSpec((1, GW), lambda b, c, g: (0, g)),
        pl.BlockSpec((1, GW), lambda b, c, g: (0, g)),
    ]
    cb2 = conv_b.reshape(1, -1)
    body = functools.partial(_ssd_body, L=L, hpg=hpg, n_groups=Gs)
    return pl.pallas_call(
        body,
        grid=(batch, nc, Gs),
        in_specs=in_specs,
        out_specs=pl.BlockSpec((L, GW), lambda b, c, g: (rows(b, c, g), g)),
        out_shape=jax.ShapeDtypeStruct((T, Di), BF16),
        scratch_shapes=[pltpu.VMEM((Hs * P, N), F32),
                        pltpu.VMEM((HALO + L, GW + 2 * N), F32),
                        pltpu.VMEM((L, GW), F32),
                        pltpu.VMEM((Gs, L, LANES), F32),
                        pltpu.VMEM((Gs, L, LANES), F32),
                        pltpu.VMEM((Hs, L), F32),
                        pltpu.VMEM((Hs, L), F32)],
        compiler_params=_params(("arbitrary", "arbitrary", "arbitrary"), 40),
        name="ssd",
    )(xbc, xbc, xbc, xbc, xbc, xbc, z, small, dt_t,
      conv_w, conv_w, conv_w, cb2, cb2, cb2,
      dt_bias.reshape(1, Hs), dt_bias.reshape(Hs, 1), a_log.reshape(1, Hs), a_log.reshape(Hs, 1),
      jnp.repeat(d_skip, P).reshape(1, Di), norm_w.reshape(1, Di))


def _merge_epilogue(accs, e_refs, o_refs):
    ga, gb = e_refs
    o_refs[0][...] = (ga[...].astype(F32) * accs[0] + gb[...].astype(F32) * accs[1]).astype(o_refs[0].dtype)


def _rope_lane_tables(pos):
    half = ROPE_DIM // 2
    inv_freq = jnp.float32(ROPE_THETA) ** (-jnp.arange(half, dtype=F32) / half)
    ang = pos.astype(F32)[..., None] * inv_freq
    cos, sin = jnp.cos(ang), jnp.sin(ang)
    pad = [(0, 0)] * (cos.ndim - 1)
    one = jnp.ones(cos.shape[:-1] + (LANES - ROPE_DIM,), F32)
    c = jnp.concatenate([cos, cos, one], axis=-1)
    s1 = jnp.pad(-sin, pad + [(0, LANES - half)])
    s2 = jnp.pad(sin, pad + [(half, LANES - ROPE_DIM)])
    return c, s1, s2


def _pad_cols(w, n):
    return jnp.pad(w, ((0, 0), (0, n - w.shape[1])))


def _layer(x, mod, positions, tabs, tabs_c, consts, p):
    B, S = positions.shape
    T, D = x.shape
    H, G, dh = NSA_HEADS, NSA_KV_GROUPS, NSA_HEAD_DIM
    hpg = H // G
    F = p["ffn1_w_gate"].shape[1]
    Fp = -(-F // FFN_PAD) * FFN_PAD
    tm = _tile(TM_MATMUL, S)
    tpb = S // tm

    def ffn(h, x_res, wg, wu, wd, slot_gate):
        a = ffn_up(h, _pad_cols(wg.astype(BF16), Fp), _pad_cols(wu.astype(BF16), Fp))
        wd_p = jnp.pad(wd.astype(BF16), ((0, Fp - F), (0, 0)))
        return ffn_down(a, wd_p, x_res, mod, S, slot_gate, 0.5)

    h = modulate(x, mod, S, 0, 1)
    r = ffn(h, x, p["ffn1_w_gate"], p["ffn1_w_up"], p["ffn1_w_down"], 2)
    x, h = ln_mod(r, p["ln1_g"], p["ln1_b"], mod, S, 3, 4)

    w_in = p["w_in"]
    qkv_w = NSA_Q_WIDTH + 6 * NSA_KV_WIDTH
    w_head = w_in[:, :qkv_w].astype(BF16)
    cuts = np.cumsum(IN_SPLIT_SIZES[7:])[:-1].tolist()
    w_gn, w_z, w_xbc, w_dt, w_ga, w_gb = jnp.split(w_in[:, qkv_w:], cuts, axis=1)
    w_tail = jnp.concatenate([w_z, w_xbc, w_ga, w_gb, _pad_cols(w_dt, LANES), _pad_cols(w_gn, LANES)],
                             axis=1).astype(BF16)
    tab_specs = [pl.BlockSpec((tm, LANES), lambda i, j: (i, 0))] * 3

    tnq = hpg * dh
    q = fused_matmul([h], [w_head], [(0, 0)], functools.partial(_q_epilogue, hpg), tabs, tab_specs,
                     jax.ShapeDtypeStruct((B, H, S, dh), BF16),
                     pl.BlockSpec((None, hpg, tm, dh), lambda i, j: (i // tpb, j, i % tpb, 0)),
                     tm, tnq, 52, "in_proj_q", w_cols=(0, NSA_Q_WIDTH))
    tnk = G * dh
    kv_spec = pl.BlockSpec((None, None, G, tm, dh), lambda i, j: (j, i // tpb, 0, i % tpb, 0))
    kvcmp = fused_matmul([h], [w_head], [(0, 0)], functools.partial(_kv_plain_epilogue, G), [], [],
                         jax.ShapeDtypeStruct((2, B, G, S, dh), F32), kv_spec, tm, tnk, 52, "in_proj_kvcmp",
                         w_cols=(NSA_Q_WIDTH, 2 * tnk))
    sw = fused_matmul([h], [w_head], [(0, 0)], functools.partial(_kv_rope_epilogue, G), tabs, tab_specs,
                      jax.ShapeDtypeStruct((4, B, G, S, dh), BF16), kv_spec, tm, tnk, 52, "in_proj_slcwin",
                      w_cols=(NSA_Q_WIDTH + 2 * tnk, 4 * tnk))
    plain_spec = lambda tn: pl.BlockSpec((tm, tn), lambda i, j: (i, j))
    tnt = math.gcd(1024, SSD_D_INNER, SSD_CONV_CH, 2 * D)
    z = fused_matmul([h], [w_tail], [(0, 0)], _plain_epilogue, [], [],
                     jax.ShapeDtypeStruct((T, SSD_D_INNER), F32), plain_spec(tnt), tm, tnt, 52, "in_proj_z",
                     w_cols=(0, SSD_D_INNER))
    xbc = fused_matmul([h], [w_tail], [(0, 0)], _plain_epilogue, [], [],
                       jax.ShapeDtypeStruct((T, SSD_CONV_CH), F32), plain_spec(tnt), tm, tnt, 52, "in_proj_xbc",
                       w_cols=(SSD_D_INNER, SSD_CONV_CH))
    gab = fused_matmul([h], [w_tail], [(0, 0)], _sigmoid_epilogue, [], [],
                       jax.ShapeDtypeStruct((T, 2 * D), BF16), plain_spec(tnt), tm, tnt, 52, "in_proj_gates",
                       w_cols=(SSD_D_INNER + SSD_CONV_CH, 2 * D))
    small = fused_matmul([h], [w_tail], [(0, 0)], _small_epilogue, [], [],
                         jax.ShapeDtypeStruct((T, 2 * LANES), F32), plain_spec(2 * LANES), tm, 2 * LANES, 52,
                         "in_proj_small", w_cols=(SSD_D_INNER + SSD_CONV_CH + 2 * D, 2 * LANES))

    n_seg = S // NSA_CMP_STRIDE
    kvseg = kvcmp.reshape(2, B, G, n_seg, NSA_CMP_STRIDE * dh)
    w1 = jnp.stack([p["nsa_cmp_k_w1"], p["nsa_cmp_v_w1"]]).astype(BF16)
    w2 = jnp.stack([p["nsa_cmp_k_w2"], p["nsa_cmp_v_w2"]]).astype(BF16)
    pos8 = jnp.broadcast_to(p["nsa_cmp_pos"].reshape(1, -1), (SUBLANES, NSA_CMP_BLOCK * dh)).astype(BF16)
    kvc = nsa_compress(kvseg, w1, w2, pos8, tabs_c)
    e_blk, ones_col, overlap = consts
    ones4 = jnp.broadcast_to(ones_col, (B, G, S, dh))
    kx = jnp.concatenate([sw[0], jnp.broadcast_to(e_blk, (B, G, S, dh))], axis=-1)

    def chunked_t(v, chunk):
        return jnp.swapaxes(v.reshape(B, G, S // chunk, chunk, 2 * dh), -1, -2)

    vxt = chunked_t(jnp.concatenate([sw[1], ones4], axis=-1), _tile(ATTN_TK, S))
    vwxt = chunked_t(jnp.concatenate([sw[3], ones4], axis=-1), _tile(ATTN_TQ, S))
    vcxt = jnp.swapaxes(jnp.concatenate([kvc[1], jnp.broadcast_to(overlap, (B, G, n_seg, LANES))], axis=-1), -1, -2)
    g3 = small[:, LANES:LANES + 3 * H].reshape(B, S, 3, G, hpg)
    gates_t = jnp.transpose(g3, (0, 3, 2, 4, 1)).reshape(B, G, 3 * hpg, S)
    o_a = nsa_attention(q, kvc, vcxt, kx, vxt, sw, vwxt, gates_t, S)

    dt_t = jnp.transpose(small[:, :SSD_HEADS].reshape(B, S, SSD_HEADS), (0, 2, 1))
    o_b = ssd_mixer(z, xbc, small, dt_t, p["ssd_conv_w"], p["ssd_conv_b"], p["ssd_dt_bias"], p["ssd_a_log"],
                    p["ssd_d"], p["ssd_norm_w"], B, S)

    tm2 = _tile(TM_DUAL, S)
    tn2 = _tile(512, D)
    nga = D // tn2
    merged = fused_matmul([o_a, o_b], [p["w_branch_a"].astype(BF16), p["w_branch_b"].astype(BF16)],
                          [(0, 0), (1, 1)], _merge_epilogue, [gab, gab],
                          [pl.BlockSpec((tm2, tn2), lambda i, j: (i, j)),
                           pl.BlockSpec((tm2, tn2), lambda i, j: (i, nga + j))],
                          jax.ShapeDtypeStruct((T, D), BF16), pl.BlockSpec((tm2, tn2), lambda i, j: (i, j)),
                          tm2, tn2, 52, "branch_merge")
    tno = _tile(512, D)
    r = fused_matmul([merged], [p["w_out"].astype(BF16)], [(0, 0)], functools.partial(_residual_epilogue, 1.0),
                     [x, mod],
                     [pl.BlockSpec((tm, tno), lambda i, j: (i, j)),
                      pl.BlockSpec((None, 1, tno), lambda i, j: ((i // tpb) * N_ADA + 5, 0, j))],
                     jax.ShapeDtypeStruct((T, D), F32), pl.BlockSpec((tm, tno), lambda i, j: (i, j)),
                     tm, tno, 52, "out_proj")
    x, h = ln_mod(r, p["ln2_g"], p["ln2_b"], mod, S, 6, 7)

    r = ffn(h, x, p["ffn2_w_gate"], p["ffn2_w_up"], p["ffn2_w_down"], 8)
    return layer_norm(r, p["ln3_g"], p["ln3_b"])


def kernel(x, c, positions, w_ada, b_ada, ffn1_w_gate, ffn1_w_up, ffn1_w_down, w_in, nsa_cmp_pos, nsa_cmp_k_w1, nsa_cmp_k_w2, nsa_cmp_v_w1, nsa_cmp_v_w2, ssd_conv_w, ssd_conv_b, ssd_dt_bias, ssd_a_log, ssd_d, ssd_norm_w, w_branch_a, w_branch_b, w_out, ffn2_w_gate, ffn2_w_up, ffn2_w_down, ln1_g, ln1_b, ln2_g, ln2_b, ln3_g, ln3_b):
    B, S, D = x.shape
    dh = NSA_HEAD_DIM
    per_layer = dict(ffn1_w_gate=ffn1_w_gate, ffn1_w_up=ffn1_w_up, ffn1_w_down=ffn1_w_down, w_in=w_in,
                     nsa_cmp_pos=nsa_cmp_pos, nsa_cmp_k_w1=nsa_cmp_k_w1, nsa_cmp_k_w2=nsa_cmp_k_w2,
                     nsa_cmp_v_w1=nsa_cmp_v_w1, nsa_cmp_v_w2=nsa_cmp_v_w2, ssd_conv_w=ssd_conv_w,
                     ssd_conv_b=ssd_conv_b, ssd_dt_bias=ssd_dt_bias, ssd_a_log=ssd_a_log, ssd_d=ssd_d,
                     ssd_norm_w=ssd_norm_w, w_branch_a=w_branch_a, w_branch_b=w_branch_b, w_out=w_out,
                     ffn2_w_gate=ffn2_w_gate, ffn2_w_up=ffn2_w_up, ffn2_w_down=ffn2_w_down,
                     ln1_g=ln1_g, ln1_b=ln1_b, ln2_g=ln2_g, ln2_b=ln2_b, ln3_g=ln3_g, ln3_b=ln3_b)

    tabs = [t.reshape(B * S, LANES) for t in _rope_lane_tables(positions)]
    n_seg = S // NSA_CMP_STRIDE
    c_end = jnp.minimum(NSA_CMP_STRIDE * jnp.arange(n_seg) + NSA_CMP_BLOCK - 1, S - 1)
    ck, s1k, s2k = _rope_lane_tables(positions[:, c_end])
    tabs_c = [jnp.stack([ck, jnp.ones_like(ck)]), jnp.stack([s1k, jnp.zeros_like(s1k)]),
              jnp.stack([s2k, jnp.zeros_like(s2k)])]

    key_blk = np.arange(S) // NSA_SEL_BLOCK
    e_blk = jnp.asarray(key_blk[:, None] == np.arange(LANES)[None, :], BF16)
    ones_col = jnp.asarray(np.arange(dh)[None, :] == 0, BF16) * jnp.ones((S, 1), BF16)
    n_cmp = (S - NSA_CMP_BLOCK) // NSA_CMP_STRIDE + 1
    c_start = NSA_CMP_STRIDE * np.arange(n_seg)
    sel_start = NSA_SEL_BLOCK * np.arange(LANES)
    ov = ((c_start[:, None] < sel_start[None, :] + NSA_SEL_BLOCK)
          & (c_start[:, None] + NSA_CMP_BLOCK - 1 >= sel_start[None, :])
          & (np.arange(n_seg)[:, None] < n_cmp) & (np.arange(LANES)[None, :] < S // NSA_SEL_BLOCK))
    consts = (e_blk, ones_col, jnp.asarray(ov, BF16))

    xt = x.reshape(B * S, D)
    for l in range(DEPTH):
        mod = ada_proj(c, w_ada[l], b_ada[l]).reshape(B * N_ADA, 1, D)
        xt = _layer(xt, mod, positions, tabs, tabs_c, consts, {k: v[l] for k, v in per_layer.items()})
    return xt.reshape(B, S, D)
```

```python
import functools
import math

import jax
import jax.numpy as jnp
import numpy as np
from jax import lax
from jax.experimental import pallas as pl
from jax.experimental.pallas import tpu as pltpu

D_MODEL = 4096
BATCH = 2
SEQ = 8192
DEPTH = 1

NSA_HEADS = 32
NSA_KV_GROUPS = 4
NSA_HEAD_DIM = 128
NSA_CMP_BLOCK = 32
NSA_CMP_STRIDE = 16
NSA_CMP_HIDDEN = 256
NSA_SEL_BLOCK = 64
NSA_SEL_TOPN = 16
NSA_WINDOW = 512
ROPE_THETA = 500000.0
ROPE_DIM = NSA_HEAD_DIM // 4
FORCE_BONUS = 1.0e4
NEG_INF = -1.0e30

SSD_D_INNER = D_MODEL
SSD_HEAD_DIM = 64
SSD_HEADS = SSD_D_INNER // SSD_HEAD_DIM
SSD_GROUPS = 8
SSD_STATE = 128
SSD_CONV = 4
SSD_CHUNK = 256
SSD_CONV_CH = SSD_D_INNER + 2 * SSD_GROUPS * SSD_STATE

FFN_DIM = 256 * ((8 * D_MODEL // 3 + 255) // 256)
N_ADA = 9
LN_EPS = 1e-5
RMS_EPS = 1e-5
DEEPNORM_ALPHA = (2 * DEPTH) ** 0.25

NSA_Q_WIDTH = NSA_HEADS * NSA_HEAD_DIM
NSA_KV_WIDTH = NSA_KV_GROUPS * NSA_HEAD_DIM
IN_SPLIT_SIZES = (NSA_Q_WIDTH,) + (NSA_KV_WIDTH,) * 6 + (3 * NSA_HEADS, SSD_D_INNER, SSD_CONV_CH, SSD_HEADS, D_MODEL, D_MODEL)

F32 = jnp.float32
BF16 = jnp.bfloat16
LANES = 128
SUBLANES = 8
LOG2E = 1.4426950408889634
MIB = 1024 * 1024

TM_MATMUL = 1024
TM_DUAL = 512
TM_ROWWISE = 256
ATTN_TQ = 128
ATTN_TK = 512
TM_FFN_UP = 2048
FFN_GU = 256


def _tile(pref, dim):
    t = min(pref, dim)
    assert dim % t == 0, (pref, dim)
    return t


def _params(sem, vmem_mib):
    return pltpu.CompilerParams(dimension_semantics=sem, vmem_limit_bytes=vmem_mib * MIB)


def _sigmoid(x):
    return 1.0 / (1.0 + jnp.exp(-x))


def _silu(x):
    return x * _sigmoid(x)


def _nt_dot(a, b):
    return lax.dot_general(a, b, (((1,), (1,)), ((), ())), preferred_element_type=F32)


def _ada_body(c_ref, w_ref, b_ref, o_ref):
    c = c_ref[...]
    ca = _silu(c).astype(BF16)
    o_ref[...] = jnp.dot(ca, w_ref[...].astype(BF16), preferred_element_type=F32) + b_ref[...]


def ada_proj(c, w_ada, b_ada):
    B, D = c.shape
    N = w_ada.shape[1]
    tn = _tile(512, N)
    c8 = jnp.zeros((SUBLANES, D), F32).at[:B].set(c)
    out = pl.pallas_call(
        _ada_body,
        grid=(N // tn,),
        in_specs=[pl.BlockSpec((SUBLANES, D), lambda j: (0, 0)),
                  pl.BlockSpec((D, tn), lambda j: (0, j)),
                  pl.BlockSpec((1, tn), lambda j: (0, j))],
        out_specs=pl.BlockSpec((SUBLANES, tn), lambda j: (0, j)),
        out_shape=jax.ShapeDtypeStruct((SUBLANES, N), F32),
        compiler_params=_params(("arbitrary",), 40),
        name="ada_proj",
    )(c8, w_ada, b_ada.reshape(1, N))
    return out[:B]


def _mod_body(x_ref, shift_ref, scale_ref, h_ref):
    h_ref[...] = (x_ref[...] * (1.0 + scale_ref[...]) + shift_ref[...]).astype(h_ref.dtype)


def _ln_mod_body(r_ref, g_ref, b_ref, shift_ref, scale_ref, x_ref, h_ref):
    r = r_ref[...]
    mu = jnp.mean(r, axis=-1, keepdims=True)
    d = r - mu
    var = jnp.mean(d * d, axis=-1, keepdims=True)
    x = d * lax.rsqrt(var + LN_EPS) * g_ref[...] + b_ref[...]
    x_ref[...] = x
    h_ref[...] = (x * (1.0 + scale_ref[...]) + shift_ref[...]).astype(h_ref.dtype)


def _ln_body(r_ref, g_ref, b_ref, x_ref):
    r = r_ref[...]
    mu = jnp.mean(r, axis=-1, keepdims=True)
    d = r - mu
    var = jnp.mean(d * d, axis=-1, keepdims=True)
    x_ref[...] = d * lax.rsqrt(var + LN_EPS) * g_ref[...] + b_ref[...]


def _mod_spec(D, tiles_per_batch, slot):
    return pl.BlockSpec((None, 1, D), lambda i: ((i // tiles_per_batch) * N_ADA + slot, 0, 0))


def modulate(x, mod, seq, slot_shift, slot_scale):
    T, D = x.shape
    tm = _tile(TM_ROWWISE, seq)
    tpb = seq // tm
    return pl.pallas_call(
        _mod_body,
        grid=(T // tm,),
        in_specs=[pl.BlockSpec((tm, D), lambda i: (i, 0)),
                  _mod_spec(D, tpb, slot_shift), _mod_spec(D, tpb, slot_scale)],
        out_specs=pl.BlockSpec((tm, D), lambda i: (i, 0)),
        out_shape=jax.ShapeDtypeStruct((T, D), BF16),
        compiler_params=_params(("parallel",), 32),
        name="modulate",
    )(x, mod, mod)


def ln_mod(r, g, b, mod, seq, slot_shift, slot_scale):
    T, D = r.shape
    tm = _tile(TM_ROWWISE, seq)
    tpb = seq // tm
    row = pl.BlockSpec((tm, D), lambda i: (i, 0))
    vec = pl.BlockSpec((1, D), lambda i: (0, 0))
    return pl.pallas_call(
        _ln_mod_body,
        grid=(T // tm,),
        in_specs=[row, vec, vec, _mod_spec(D, tpb, slot_shift), _mod_spec(D, tpb, slot_scale)],
        out_specs=[row, row],
        out_shape=[jax.ShapeDtypeStruct((T, D), F32), jax.ShapeDtypeStruct((T, D), BF16)],
        compiler_params=_params(("parallel",), 40),
        name="ln_mod",
    )(r, g.reshape(1, D), b.reshape(1, D), mod, mod)


def layer_norm(r, g, b):
    T, D = r.shape
    tm = _tile(TM_ROWWISE, T)
    row = pl.BlockSpec((tm, D), lambda i: (i, 0))
    vec = pl.BlockSpec((1, D), lambda i: (0, 0))
    return pl.pallas_call(
        _ln_body,
        grid=(T // tm,),
        in_specs=[row, vec, vec],
        out_specs=row,
        out_shape=jax.ShapeDtypeStruct((T, D), F32),
        compiler_params=_params(("parallel",), 32),
        name="layer_norm",
    )(r, g.reshape(1, D), b.reshape(1, D))


def _fused_matmul_body(nx, nw, ne, pairs, epilogue, *refs):
    x_refs = refs[:nx]
    w_refs = refs[nx:nx + nw]
    e_refs = refs[nx + nw:nx + nw + ne]
    o_refs = refs[nx + nw + ne:]
    accs = [jnp.dot(x_refs[a][...], w_refs[b][...], preferred_element_type=F32) for a, b in pairs]
    epilogue(accs, e_refs, o_refs)


def fused_matmul(xs, ws, pairs, epilogue, extras, extra_specs, out_shapes, out_specs, tm, tn, vmem_mib, name,
                 w_cols=None):
    M = xs[0].shape[0]
    col0, N = w_cols if w_cols is not None else (0, ws[0].shape[1])
    assert col0 % tn == 0 and N % tn == 0 and M % tm == 0
    j0 = col0 // tn
    in_specs = ([pl.BlockSpec((tm, x.shape[1]), lambda i, j: (i, 0)) for x in xs]
                + [pl.BlockSpec((w.shape[0], tn), lambda i, j: (0, j0 + j)) for w in ws]
                + list(extra_specs))
    body = functools.partial(_fused_matmul_body, len(xs), len(ws), len(extras), tuple(pairs), epilogue)
    return pl.pallas_call(
        body,
        grid=(M // tm, N // tn),
        in_specs=in_specs,
        out_specs=out_specs,
        out_shape=out_shapes,
        compiler_params=_params(("parallel", "arbitrary"), vmem_mib),
        name=name,
    )(*xs, *ws, *extras)


def _swiglu_epilogue(accs, e_refs, o_refs):
    gu = accs[0]
    half = gu.shape[1] // 2
    o_refs[0][...] = (_silu(gu[:, :half]) * gu[:, half:]).astype(o_refs[0].dtype)


def pack_gate_up(wg, wu):
    D, F = wg.shape
    assert F % FFN_GU == 0
    gu = jnp.stack([wg.reshape(D, F // FFN_GU, FFN_GU), wu.reshape(D, F // FFN_GU, FFN_GU)], axis=2)
    return gu.reshape(D, 2 * F).astype(BF16)


def ffn_up(h, w_gu):
    T, D = h.shape
    F = w_gu.shape[1] // 2
    tm = _tile(TM_FFN_UP, T)
    return fused_matmul([h], [w_gu], [(0, 0)], _swiglu_epilogue, [], [],
                        jax.ShapeDtypeStruct((T, F), BF16), pl.BlockSpec((tm, FFN_GU), lambda i, j: (i, j)),
                        tm, 2 * FFN_GU, 58, "ffn_up")


def _residual_epilogue(coef, accs, e_refs, o_refs):
    x_ref, gate_ref = e_refs
    o_refs[0][...] = DEEPNORM_ALPHA * x_ref[...] + (coef * gate_ref[...]) * accs[0]


def ffn_down(a, wd, x, mod, seq, slot_gate, coef):
    T, D = x.shape
    tm, tn = _tile(TM_DUAL, seq), _tile(512, D)
    tpb = seq // tm
    return fused_matmul([a], [wd], [(0, 0)], functools.partial(_residual_epilogue, coef), [x, mod],
                        [pl.BlockSpec((tm, tn), lambda i, j: (i, j)),
                         pl.BlockSpec((None, 1, tn), lambda i, j: ((i // tpb) * N_ADA + slot_gate, 0, j))],
                        jax.ShapeDtypeStruct((T, D), F32), pl.BlockSpec((tm, tn), lambda i, j: (i, j)),
                        tm, tn, 58, "ffn_down")


def _rope(t, c, s1, s2):
    half = ROPE_DIM // 2
    return t * c + pltpu.roll(t, LANES - half, axis=1) * s1 + pltpu.roll(t, half, axis=1) * s2


def _q_epilogue(n_heads, accs, e_refs, o_refs):
    c, s1, s2 = (r[...] for r in e_refs)
    qscale = NSA_HEAD_DIM ** -0.5 * LOG2E
    for h in range(n_heads):
        t = accs[0][:, h * LANES:(h + 1) * LANES]
        o_refs[0][h] = (_rope(t, c, s1, s2) * qscale).astype(o_refs[0].dtype)


def _kv_plain_epilogue(n_groups, accs, e_refs, o_refs):
    for g in range(n_groups):
        o_refs[0][g] = accs[0][:, g * LANES:(g + 1) * LANES].astype(o_refs[0].dtype)


def _kv_rope_epilogue(n_groups, accs, e_refs, o_refs):
    c, s1, s2 = (r[...] for r in e_refs)
    is_key = pl.program_id(1) % 2 == 0

    @pl.when(is_key)
    def _():
        for g in range(n_groups):
            t = accs[0][:, g * LANES:(g + 1) * LANES]
            o_refs[0][g] = _rope(t, c, s1, s2).astype(o_refs[0].dtype)

    @pl.when(jnp.logical_not(is_key))
    def _():
        for g in range(n_groups):
            o_refs[0][g] = accs[0][:, g * LANES:(g + 1) * LANES].astype(o_refs[0].dtype)


def _plain_epilogue(accs, e_refs, o_refs):
    o_refs[0][...] = accs[0].astype(o_refs[0].dtype)


def _sigmoid_epilogue(accs, e_refs, o_refs):
    o_refs[0][...] = _sigmoid(accs[0]).astype(o_refs[0].dtype)


def _small_epilogue(accs, e_refs, o_refs):
    a = accs[0]
    lane = lax.broadcasted_iota(jnp.int32, a.shape, 1)
    o_refs[0][...] = jnp.where(lane >= LANES, _sigmoid(a), a)


def _compress_body(seg_ref, w1_ref, w2_ref, pos_ref, c_ref, s1_ref, s2_ref, o_ref):
    n_seg = seg_ref.shape[0]
    half_k = seg_ref.shape[1]
    seg = seg_ref[...].astype(BF16)
    top = jnp.dot(seg, w1_ref[0:half_k, :], preferred_element_type=F32)
    bot = jnp.dot(seg, w1_ref[half_k:2 * half_k, :], preferred_element_type=F32)
    cb = jnp.dot(pos_ref[...], w1_ref[...], preferred_element_type=F32)[0:1, :]
    hid = _silu(top + pltpu.roll(bot, n_seg - 1, axis=0) + cb).astype(BF16)
    out = jnp.dot(hid, w2_ref[...], preferred_element_type=F32)
    out = _rope(out, c_ref[...], s1_ref[...], s2_ref[...])
    rowi = lax.broadcasted_iota(jnp.int32, out.shape, 0)
    o_ref[...] = jnp.where(rowi < n_seg - 1, out, 0.0).astype(o_ref.dtype)


def nsa_compress(kvseg, w1, w2, pos8, tabs):
    _, B, G, n_seg, K2 = kvseg.shape
    dh = NSA_HEAD_DIM
    hid = w1.shape[-1]
    tab_spec = pl.BlockSpec((None, None, n_seg, dh), lambda kv, b, g: (kv, b, 0, 0))
    return pl.pallas_call(
        _compress_body,
        grid=(2, B, G),
        in_specs=[pl.BlockSpec((None, None, None, n_seg, K2), lambda kv, b, g: (kv, b, g, 0, 0)),
                  pl.BlockSpec((None, 2 * K2, hid), lambda kv, b, g: (kv, 0, 0)),
                  pl.BlockSpec((None, hid, dh), lambda kv, b, g: (kv, 0, 0)),
                  pl.BlockSpec((SUBLANES, 2 * K2), lambda kv, b, g: (0, 0)),
                  tab_spec, tab_spec, tab_spec],
        out_specs=pl.BlockSpec((None, None, None, n_seg, dh), lambda kv, b, g: (kv, b, g, 0, 0)),
        out_shape=jax.ShapeDtypeStruct((2, B, G, n_seg, dh), BF16),
        compiler_params=_params(("arbitrary", "arbitrary", "arbitrary"), 40),
        name="nsa_compress",
    )(kvseg, w1, w2, pos8, *tabs)


def _attn_body(q_ref, kc_ref, vcxt_ref, kx_ref, vxt_ref, kw_ref, vwxt_ref, gate_ref, o_ref,
               qxt_ref, s0_ref, s1_ref, mt0_ref, mt1_ref, m_ref, acc_ref, out_ref, *, tq, tk, wk, hpg, top_n):
    dh = NSA_HEAD_DIM
    R = hpg * tq
    hw = R // 2
    n_cp = kc_ref.shape[0]
    t0 = pl.program_id(2) * tq
    t_q = t0 + lax.broadcasted_iota(jnp.int32, (1, tq), 1)
    for h in range(hpg):
        qxt_ref[0:dh, h * tq:(h + 1) * tq] = q_ref[h].astype(F32).T.astype(BF16)
    qt = qxt_ref[0:dh, :]
    gates = gate_ref[...]

    def gate_row(branch):
        return jnp.concatenate([gates[branch * hpg + h:branch * hpg + h + 1, :] for h in range(hpg)], axis=1)

    def all_heads(x):
        return jnp.concatenate([x] * hpg, axis=1)

    c_end = lax.broadcasted_iota(jnp.int32, (n_cp, 1), 0) * NSA_CMP_STRIDE + (NSA_CMP_BLOCK - 1)
    s_c = jnp.dot(kc_ref[...], qt, preferred_element_type=F32) + all_heads(jnp.where(c_end <= t_q, 0.0, NEG_INF))
    mx_c = jnp.max(s_c, axis=0, keepdims=True)
    vcxt = vcxt_ref[...]
    g_c = gate_row(0)
    imp_t = jnp.zeros((LANES, tq), F32)
    for half in range(2):
        cols = slice(half * hw, (half + 1) * hw)
        e = jnp.exp2(s_c[:, cols] - mx_c[:, cols])
        inv = jnp.where(mx_c[:, cols] > 0.5 * NEG_INF, 1.0 / jnp.sum(e, axis=0, keepdims=True), 0.0)
        r = jnp.dot(vcxt, e.astype(BF16), preferred_element_type=F32)
        out_ref[:, cols] = r[0:dh, :] * (g_c[:, cols] * inv)
        ri = r[dh:2 * dh, :] * inv
        for h in range(hpg // 2):
            imp_t = imp_t + ri[:, h * tq:(h + 1) * tq]

    w0 = pl.multiple_of(jnp.maximum(t0 + tq - wk, 0), tq)
    c0 = w0 // tq
    kpos_w = w0 + lax.broadcasted_iota(jnp.int32, (wk, 1), 0)
    w_bias = jnp.where((kpos_w <= t_q) & (kpos_w > t_q - NSA_WINDOW), 0.0, NEG_INF)
    s_w = jnp.dot(kw_ref[pl.ds(w0, wk), :], qt, preferred_element_type=F32) + all_heads(w_bias)
    mx_w = jnp.max(s_w, axis=0, keepdims=True)

    j_i = lax.broadcasted_iota(jnp.int32, (LANES, 1), 0)
    j_f = j_i.astype(F32)
    cur = t_q // NSA_SEL_BLOCK
    causal_blk = j_i * NSA_SEL_BLOCK <= t_q
    forced = (j_i == 0) | (j_i == cur) | (j_i == cur - 1)
    work = jnp.where(causal_blk & jnp.logical_not(forced), imp_t, NEG_INF)
    sel = forced
    for _ in range(top_n - 3):
        mx = jnp.max(work, axis=0, keepdims=True)
        first = jnp.min(jnp.where(work == mx, j_f, float(LANES)), axis=0, keepdims=True)
        pick = j_f == first
        sel = sel | pick
        work = jnp.where(pick, -jnp.inf, work)
    bias_t = jnp.where(sel & causal_blk, 0.0, NEG_INF).astype(BF16)
    for h in range(hpg):
        qxt_ref[dh:2 * dh, h * tq:(h + 1) * tq] = bias_t

    s_refs = (s0_ref, s1_ref)
    mt_refs = (mt0_ref, mt1_ref)

    def score_tile(kt, slot):
        k0 = pl.multiple_of(kt * tk, tk)
        kpos = k0 + lax.broadcasted_iota(jnp.int32, (tk, 1), 0)
        s = jnp.dot(kx_ref[pl.ds(k0, tk), :], qxt_ref[...], preferred_element_type=F32)
        s = s + all_heads(jnp.where(kpos <= t_q, 0.0, NEG_INF))
        s_refs[slot][...] = s
        mt_refs[slot][...] = jnp.max(s, axis=0, keepdims=True)

    def softmax_pv(kt, slot):
        vt = vxt_ref[kt]
        for half in range(2):
            cols = slice(half * hw, (half + 1) * hw)
            m_old = m_ref[:, cols]
            m_new = jnp.maximum(m_old, mt_refs[slot][:, cols])
            p = jnp.exp2(s_refs[slot][:, cols] - m_new).astype(BF16)
            pv = jnp.dot(vt, p, preferred_element_type=F32)
            acc_ref[:, cols] = jnp.exp2(m_old - m_new) * acc_ref[:, cols] + pv
            m_ref[:, cols] = m_new

    score_tile(0, 0)

    vwt = jnp.concatenate([vwxt_ref[c0 + i] for i in range(wk // tq)], axis=1)
    g_w = gate_row(2)
    for half in range(2):
        cols = slice(half * hw, (half + 1) * hw)
        e = jnp.exp2(s_w[:, cols] - mx_w[:, cols]).astype(BF16)
        r = jnp.dot(vwt, e, preferred_element_type=F32)
        inv = jnp.where(mx_w[:, cols] > 0.5 * NEG_INF, 1.0 / r[dh:dh + 1, :], 0.0)
        out_ref[:, cols] += r[0:dh, :] * (g_w[:, cols] * inv)

    m_ref[...] = jnp.full(m_ref.shape, NEG_INF, F32)
    acc_ref[...] = jnp.zeros(acc_ref.shape, F32)
    n_full = t0 // tk

    def two_tiles(i, carry):
        score_tile(2 * i + 1, 1)
        softmax_pv(2 * i, 0)
        score_tile(2 * i + 2, 0)
        softmax_pv(2 * i + 1, 1)
        return carry

    lax.fori_loop(0, n_full // 2, two_tiles, 0)

    @pl.when(n_full % 2 == 0)
    def _():
        softmax_pv(n_full, 0)

    @pl.when(n_full % 2 == 1)
    def _():
        score_tile(n_full, 1)
        softmax_pv(n_full - 1, 0)
        softmax_pv(n_full, 1)

    for h in range(hpg):
        cols = slice(h * tq, (h + 1) * tq)
        acc = acc_ref[:, cols]
        l_s = acc[dh:dh + 1, :]
        ok = (m_ref[:, cols] > 0.5 * NEG_INF) & (l_s > 0.0)
        o_t = out_ref[:, cols] + acc[0:dh, :] * (gates[hpg + h:hpg + h + 1, :] * jnp.where(ok, 1.0 / l_s, 0.0))
        o_ref[:, h * dh:(h + 1) * dh] = o_t.T.astype(o_ref.dtype)


def nsa_attention(q, kvc, vcxt, kx, vxt, sw, vwxt, gates_t, seq):
    B, H, S, dh = q.shape
    G = kx.shape[1]
    hpg = H // G
    n_cp = kvc.shape[3]
    tq = _tile(ATTN_TQ, S)
    tk = _tile(ATTN_TK, S)
    wk = min(NSA_WINDOW + tq, S)
    assert tk % tq == 0 and NSA_WINDOW % tq == 0 and tq % NSA_SEL_BLOCK == 0 and S // NSA_SEL_BLOCK <= LANES
    assert hpg % 2 == 0 and wk % tq == 0 and tq == LANES
    top_n = min(NSA_SEL_TOPN, S // NSA_SEL_BLOCK)
    assert top_n > 3 and hpg < FORCE_BONUS
    R = hpg * tq
    nq = S // tq
    vr = vxt.shape[3]
    body = functools.partial(_attn_body, tq=tq, tk=tk, wk=wk, hpg=hpg, top_n=top_n)
    per_group = lambda *blk: pl.BlockSpec((None, None) + blk, lambda b, g, i: (b, g) + (0,) * len(blk))
    return pl.pallas_call(
        body,
        grid=(B, G, nq),
        in_specs=[pl.BlockSpec((None, hpg, tq, dh), lambda b, g, i: (b, g, i, 0)),
                  pl.BlockSpec((None, None, None, n_cp, dh), lambda b, g, i: (0, b, g, 0, 0)),
                  per_group(2 * dh, n_cp),
                  per_group(S, 2 * dh), per_group(S // tk, vr, tk),
                  pl.BlockSpec((None, None, None, S, dh), lambda b, g, i: (2, b, g, 0, 0)),
                  per_group(S // tq, vr, tq),
                  pl.BlockSpec((None, None, 3 * hpg, tq), lambda b, g, i: (b, g, 0, i))],
        out_specs=pl.BlockSpec((tq, hpg * dh), lambda b, g, i: (b * nq + i, g)),
        out_shape=jax.ShapeDtypeStruct((B * S, H * dh), BF16),
        scratch_shapes=[pltpu.VMEM((2 * dh, R), BF16),
                        pltpu.VMEM((tk, R), F32),
                        pltpu.VMEM((tk, R), F32),
                        pltpu.VMEM((1, R), F32),
                        pltpu.VMEM((1, R), F32),
                        pltpu.VMEM((1, R), F32),
                        pltpu.VMEM((vr, R), F32),
                        pltpu.VMEM((dh, R), F32)],
        compiler_params=_params(("parallel", "parallel", "arbitrary"), 56),
        name="nsa_attention",
    )(q, kvc, vcxt, kx, vxt, sw, vwxt, gates_t)


def _split3(x):
    hi = x.astype(BF16)
    r = x - hi.astype(F32)
    mid = r.astype(BF16)
    lo = (r - mid.astype(F32)).astype(BF16)
    return hi, mid, lo


def _softplus(x):
    return jnp.maximum(x, 0.0) + jnp.log1p(jnp.exp(-jnp.abs(x)))


def _ssd_body(x_ref, xh_ref, b_ref, bh_ref, c_ref, ch_ref, z_ref, dt_ref, dtt_ref,
              wx_ref, wb_ref, wc_ref, cbx_ref, cbb_ref, cbc_ref,
              dtb_ref, dtbt_ref, alog_ref, alogt_ref, dsk_ref, nw_ref, o_ref,
              state_ref, pad_ref, y_ref, acum_ref, dtg_ref, acumt_ref, dtt_s_ref, *, L, hpg, n_groups):
    P, N = SSD_HEAD_DIM, SSD_STATE
    GW = hpg * P
    HALO = SUBLANES
    c = pl.program_id(1)
    g = pl.program_id(2)
    n_heads = hpg * n_groups

    @pl.when((c == 0) & (g == 0))
    def _():
        state_ref[...] = jnp.zeros(state_ref.shape, F32)

    ri = lax.broadcasted_iota(jnp.int32, (L, L), 0)
    ci = lax.broadcasted_iota(jnp.int32, (L, L), 1)
    tril = ri >= ci

    @pl.when(g == 0)
    def _():
        dt = _softplus(dt_ref[:, 0:n_heads] + dtb_ref[...])
        da = dt * (-jnp.exp(alog_ref[...]))
        dtt = _softplus(dtt_ref[...] + dtbt_ref[...])
        dat = dtt * (-jnp.exp(alogt_ref[...]))
        lower = tril.astype(BF16)
        upper = (ri <= ci).astype(BF16)
        acum = sum(jnp.dot(lower, part, preferred_element_type=F32) for part in _split3(da))
        acumt_ref[...] = sum(jnp.dot(part, upper, preferred_element_type=F32) for part in _split3(dat))
        dtt_s_ref[...] = dtt
        for gg in range(n_groups):
            acum_ref[gg, :, 0:hpg] = acum[:, gg * hpg:(gg + 1) * hpg]
            dtg_ref[gg, :, 0:hpg] = dt[:, gg * hpg:(gg + 1) * hpg]

    first = c == 0
    pad_ref[0:HALO, 0:GW] = jnp.where(first, 0.0, xh_ref[...])
    pad_ref[0:HALO, GW:GW + N] = jnp.where(first, 0.0, bh_ref[...])
    pad_ref[0:HALO, GW + N:GW + 2 * N] = jnp.where(first, 0.0, ch_ref[...])
    pad_ref[HALO:HALO + L, 0:GW] = x_ref[...]
    pad_ref[HALO:HALO + L, GW:GW + N] = b_ref[...]
    pad_ref[HALO:HALO + L, GW + N:GW + 2 * N] = c_ref[...]

    def conv(lo, hi, w_ref, bias_ref):
        out = bias_ref[...]
        for k in range(SSD_CONV):
            off = HALO - (SSD_CONV - 1) + k
            out = out + w_ref[k:k + 1, :] * pad_ref[off:off + L, lo:hi]
        return _silu(out)

    xa = conv(0, GW, wx_ref, cbx_ref)
    bg = conv(GW, GW + N, wb_ref, cbb_ref).astype(BF16)
    cg = conv(GW + N, GW + 2 * N, wc_ref, cbc_ref).astype(BF16)
    xat = xa.T
    cb = _nt_dot(cg, bg)
    head_lane = lax.broadcasted_iota(jnp.int32, (1, LANES), 1) < hpg
    acg = jnp.where(head_lane, acum_ref[g], 0.0)
    dtg = jnp.where(head_lane, dtg_ref[g], 0.0)
    h0 = pl.multiple_of(g * hpg, hpg)
    act = acumt_ref[pl.ds(h0, hpg), :]
    dtt = dtt_s_ref[pl.ds(h0, hpg), :]
    dsk = dsk_ref[...]
    expand = (lax.broadcasted_iota(jnp.int32, (LANES, GW), 1) // P
              == lax.broadcasted_iota(jnp.int32, (LANES, GW), 0)).astype(BF16)
    dtx = sum(jnp.dot(part, expand, preferred_element_type=F32) for part in _split3(dtg))
    ecx = sum(jnp.dot(part, expand, preferred_element_type=F32) for part in _split3(jnp.exp(acg)))
    xdt_all = xa * dtx
    first_lane = lax.broadcasted_iota(jnp.int32, (1, 2 * P), 1) < P
    first_row = lax.broadcasted_iota(jnp.int32, (2 * P, 1), 0) < P
    for hp in range(hpg // 2):
        cols = slice(hp * 2 * P, (hp + 1) * 2 * P)
        xdt = xdt_all[:, cols]
        y = jnp.zeros((L, 2 * P), F32)
        w_rows = []
        for k in range(2):
            h = 2 * hp + k
            col = acg[:, h:h + 1]
            row = act[h:h + 1, :]
            decay = jnp.where(tril, jnp.exp(col - row), 0.0)
            own = first_lane if k == 0 else jnp.logical_not(first_lane)
            y = y + jnp.dot((cb * decay).astype(BF16), jnp.where(own, xdt, 0.0).astype(BF16),
                            preferred_element_type=F32)
            a_last = act[h:h + 1, L - 1:L]
            w_rows.append((dtt[h:h + 1, :] * jnp.exp(a_last - row), jnp.exp(a_last)))
        s0 = pl.multiple_of((g * hpg + 2 * hp) * P, 2 * P)
        st = state_ref[pl.ds(s0, 2 * P), :]
        y = y + _nt_dot(cg, st.astype(BF16)) * ecx[:, cols]
        xw_t = (xat[hp * 2 * P:(hp + 1) * 2 * P, :] * jnp.where(first_row, w_rows[0][0], w_rows[1][0])).astype(BF16)
        state_ref[pl.ds(s0, 2 * P), :] = (st * jnp.where(first_row, w_rows[0][1], w_rows[1][1])
                                          + jnp.dot(xw_t, bg, preferred_element_type=F32))
        y_ref[:, cols] = y + dsk[:, cols] * xa[:, cols]

    yg = y_ref[...] * _silu(z_ref[...])
    ms = jnp.mean(yg * yg, axis=-1, keepdims=True)
    o_ref[...] = (yg * lax.rsqrt(ms + RMS_EPS) * nw_ref[...]).astype(o_ref.dtype)


def ssd_mixer(z, xbc, small, dt_t, conv_w, conv_b, dt_bias, a_log, d_skip, norm_w, batch, seq):
    T, Di = z.shape
    Gs, N, P, Hs = SSD_GROUPS, SSD_STATE, SSD_HEAD_DIM, SSD_HEADS
    hpg = Hs // Gs
    GW = hpg * P
    L = math.gcd(seq, SSD_CHUNK)
    nc = seq // L
    HALO = SUBLANES
    assert GW % N == 0 and Hs <= LANES and hpg % 2 == 0 and 2 * P == LANES
    b_off = Di // N
    c_off = b_off + Gs

    def rows(b, c, g):
        return b * nc + c

    def halo_rows(b, c, g):
        return jnp.maximum((b * seq + c * L) // HALO - 1, 0)

    in_specs = [
        pl.BlockSpec((L, GW), lambda b, c, g: (rows(b, c, g), g)),
        pl.BlockSpec((HALO, GW), lambda b, c, g: (halo_rows(b, c, g), g)),
        pl.BlockSpec((L, N), lambda b, c, g: (rows(b, c, g), b_off + g)),
        pl.BlockSpec((HALO, N), lambda b, c, g: (halo_rows(b, c, g), b_off + g)),
        pl.BlockSpec((L, N), lambda b, c, g: (rows(b, c, g), c_off + g)),
        pl.BlockSpec((HALO, N), lambda b, c, g: (halo_rows(b, c, g), c_off + g)),
        pl.BlockSpec((L, GW), lambda b, c, g: (rows(b, c, g), g)),
        pl.BlockSpec((L, LANES), lambda b, c, g: (rows(b, c, g), 0)),
        pl.BlockSpec((None, Hs, L), lambda b, c, g: (b, 0, c)),
        pl.BlockSpec((SSD_CONV, GW), lambda b, c, g: (0, g)),
        pl.BlockSpec((SSD_CONV, N), lambda b, c, g: (0, b_off + g)),
        pl.BlockSpec((SSD_CONV, N), lambda b, c, g: (0, c_off + g)),
        pl.BlockSpec((1, GW), lambda b, c, g: (0, g)),
        pl.BlockSpec((1, N), lambda b, c, g: (0, b_off + g)),
        pl.BlockSpec((1, N), lambda b, c, g: (0, c_off + g)),
        pl.BlockSpec((1, Hs), lambda b, c, g: (0, 0)),
        pl.BlockSpec((Hs, 1), lambda b, c, g: (0, 0)),
        pl.BlockSpec((1, Hs), lambda b, c, g: (0, 0)),
        pl.BlockSpec((Hs, 1), lambda b, c, g: (0, 0)),
        pl.BlockSpec((1, GW), lambda b, c, g: (0, g)),
        pl.BlockSpec((1, GW), lambda b, c, g: (0, g)),
    ]
    cb2 = conv_b.reshape(1, -1)
    body = functools.partial(_ssd_body, L=L, hpg=hpg, n_groups=Gs)
    return pl.pallas_call(
        body,
        grid=(batch, nc, Gs),
        in_specs=in_specs,
        out_specs=pl.BlockSpec((L, GW), lambda b, c, g: (rows(b, c, g), g)),
        out_shape=jax.ShapeDtypeStruct((T, Di), BF16),
        scratch_shapes=[pltpu.VMEM((Hs * P, N), F32),
                        pltpu.VMEM((HALO + L, GW + 2 * N), F32),
                        pltpu.VMEM((L, GW), F32),
                        pltpu.VMEM((Gs, L, LANES), F32),
                        pltpu.VMEM((Gs, L, LANES), F32),
                        pltpu.VMEM((Hs, L), F32),
                        pltpu.VMEM((Hs, L), F32)],
        compiler_params=_params(("arbitrary", "arbitrary", "arbitrary"), 40),
        name="ssd",
    )(xbc, xbc, xbc, xbc, xbc, xbc, z, small, dt_t,
      conv_w, conv_w, conv_w, cb2, cb2, cb2,
      dt_bias.reshape(1, Hs), dt_bias.reshape(Hs, 1), a_log.reshape(1, Hs), a_log.reshape(Hs, 1),
      jnp.repeat(d_skip, P).reshape(1, Di), norm_w.reshape(1, Di))


def _merge_epilogue(accs, e_refs, o_refs):
    ga, gb = e_refs
    o_refs[0][...] = (ga[...].astype(F32) * accs[0] + gb[...].astype(F32) * accs[1]).astype(o_refs[0].dtype)


def _rope_lane_tables(pos):
    half = ROPE_DIM // 2
    inv_freq = jnp.float32(ROPE_THETA) ** (-jnp.arange(half, dtype=F32) / half)
    ang = pos.astype(F32)[..., None] * inv_freq
    cos, sin = jnp.cos(ang), jnp.sin(ang)
    pad = [(0, 0)] * (cos.ndim - 1)
    one = jnp.ones(cos.shape[:-1] + (LANES - ROPE_DIM,), F32)
    c = jnp.concatenate([cos, cos, one], axis=-1)
    s1 = jnp.pad(-sin, pad + [(0, LANES - half)])
    s2 = jnp.pad(sin, pad + [(half, LANES - ROPE_DIM)])
    return c, s1, s2


def _pad_cols(w, n):
    return jnp.pad(w, ((0, 0), (0, n - w.shape[1])))


def _layer(x, mod, positions, tabs, tabs_c, consts, p):
    B, S = positions.shape
    T, D = x.shape
    H, G, dh = NSA_HEADS, NSA_KV_GROUPS, NSA_HEAD_DIM
    hpg = H // G
    tm = _tile(TM_MATMUL, S)
    tpb = S // tm

    def ffn(h, x_res, wg, wu, wd, slot_gate):
        a = ffn_up(h, pack_gate_up(wg, wu))
        return ffn_down(a, wd.astype(BF16), x_res, mod, S, slot_gate, 0.5)

    h = modulate(x, mod, S, 0, 1)
    r = ffn(h, x, p["ffn1_w_gate"], p["ffn1_w_up"], p["ffn1_w_down"], 2)
    x, h = ln_mod(r, p["ln1_g"], p["ln1_b"], mod, S, 3, 4)

    w_in = p["w_in"]
    qkv_w = NSA_Q_WIDTH + 6 * NSA_KV_WIDTH
    w_head = w_in[:, :qkv_w].astype(BF16)
    cuts = np.cumsum(IN_SPLIT_SIZES[7:])[:-1].tolist()
    w_gn, w_z, w_xbc, w_dt, w_ga, w_gb = jnp.split(w_in[:, qkv_w:], cuts, axis=1)
    w_tail = jnp.concatenate([w_z, w_xbc, w_ga, w_gb, _pad_cols(w_dt, LANES), _pad_cols(w_gn, LANES)],
                             axis=1).astype(BF16)
    tab_specs = [pl.BlockSpec((tm, LANES), lambda i, j: (i, 0))] * 3

    tnq = hpg * dh
    q = fused_matmul([h], [w_head], [(0, 0)], functools.partial(_q_epilogue, hpg), tabs, tab_specs,
                     jax.ShapeDtypeStruct((B, H, S, dh), BF16),
                     pl.BlockSpec((None, hpg, tm, dh), lambda i, j: (i // tpb, j, i % tpb, 0)),
                     tm, tnq, 52, "in_proj_q", w_cols=(0, NSA_Q_WIDTH))
    tnk = G * dh
    kv_spec = pl.BlockSpec((None, None, G, tm, dh), lambda i, j: (j, i // tpb, 0, i % tpb, 0))
    kvcmp = fused_matmul([h], [w_head], [(0, 0)], functools.partial(_kv_plain_epilogue, G), [], [],
                         jax.ShapeDtypeStruct((2, B, G, S, dh), F32), kv_spec, tm, tnk, 52, "in_proj_kvcmp",
                         w_cols=(NSA_Q_WIDTH, 2 * tnk))
    sw = fused_matmul([h], [w_head], [(0, 0)], functools.partial(_kv_rope_epilogue, G), tabs, tab_specs,
                      jax.ShapeDtypeStruct((4, B, G, S, dh), BF16), kv_spec, tm, tnk, 52, "in_proj_slcwin",
                      w_cols=(NSA_Q_WIDTH + 2 * tnk, 4 * tnk))
    plain_spec = lambda tn: pl.BlockSpec((tm, tn), lambda i, j: (i, j))
    tnt = math.gcd(1024, SSD_D_INNER, SSD_CONV_CH, 2 * D)
    z = fused_matmul([h], [w_tail], [(0, 0)], _plain_epilogue, [], [],
                     jax.ShapeDtypeStruct((T, SSD_D_INNER), F32), plain_spec(tnt), tm, tnt, 52, "in_proj_z",
                     w_cols=(0, SSD_D_INNER))
    xbc = fused_matmul([h], [w_tail], [(0, 0)], _plain_epilogue, [], [],
                       jax.ShapeDtypeStruct((T, SSD_CONV_CH), F32), plain_spec(tnt), tm, tnt, 52, "in_proj_xbc",
                       w_cols=(SSD_D_INNER, SSD_CONV_CH))
    gab = fused_matmul([h], [w_tail], [(0, 0)], _sigmoid_epilogue, [], [],
                       jax.ShapeDtypeStruct((T, 2 * D), BF16), plain_spec(tnt), tm, tnt, 52, "in_proj_gates",
                       w_cols=(SSD_D_INNER + SSD_CONV_CH, 2 * D))
    small = fused_matmul([h], [w_tail], [(0, 0)], _small_epilogue, [], [],
                         jax.ShapeDtypeStruct((T, 2 * LANES), F32), plain_spec(2 * LANES), tm, 2 * LANES, 52,
                         "in_proj_small", w_cols=(SSD_D_INNER + SSD_CONV_CH + 2 * D, 2 * LANES))

    n_seg = S // NSA_CMP_STRIDE
    kvseg = kvcmp.reshape(2, B, G, n_seg, NSA_CMP_STRIDE * dh)
    w1 = jnp.stack([p["nsa_cmp_k_w1"], p["nsa_cmp_v_w1"]]).astype(BF16)
    w2 = jnp.stack([p["nsa_cmp_k_w2"], p["nsa_cmp_v_w2"]]).astype(BF16)
    pos8 = jnp.broadcast_to(p["nsa_cmp_pos"].reshape(1, -1), (SUBLANES, NSA_CMP_BLOCK * dh)).astype(BF16)
    kvc = nsa_compress(kvseg, w1, w2, pos8, tabs_c)
    e_blk, ones_col, overlap = consts
    ones4 = jnp.broadcast_to(ones_col, (B, G, S, ones_col.shape[-1]))
    kx = jnp.concatenate([sw[0], jnp.broadcast_to(e_blk, (B, G, S, dh))], axis=-1)

    def chunked_t(v, chunk):
        return jnp.swapaxes(v.reshape(B, G, S // chunk, chunk, v.shape[-1]), -1, -2)

    vxt = chunked_t(jnp.concatenate([sw[1], ones4], axis=-1), _tile(ATTN_TK, S))
    vwxt = chunked_t(jnp.concatenate([sw[3], ones4], axis=-1), _tile(ATTN_TQ, S))
    vcxt = jnp.swapaxes(jnp.concatenate([kvc[1], jnp.broadcast_to(overlap, (B, G, n_seg, LANES))], axis=-1), -1, -2)
    g3 = small[:, LANES:LANES + 3 * H].reshape(B, S, 3, G, hpg)
    gates_t = jnp.transpose(g3, (0, 3, 2, 4, 1)).reshape(B, G, 3 * hpg, S)
    o_a = nsa_attention(q, kvc, vcxt, kx, vxt, sw, vwxt, gates_t, S)

    dt_t = jnp.transpose(small[:, :SSD_HEADS].reshape(B, S, SSD_HEADS), (0, 2, 1))
    o_b = ssd_mixer(z, xbc, small, dt_t, p["ssd_conv_w"], p["ssd_conv_b"], p["ssd_dt_bias"], p["ssd_a_log"],
                    p["ssd_d"], p["ssd_norm_w"], B, S)

    tm2 = _tile(TM_DUAL, S)
    tn2 = _tile(512, D)
    nga = D // tn2
    merged = fused_matmul([o_a, o_b], [p["w_branch_a"].astype(BF16), p["w_branch_b"].astype(BF16)],
                          [(0, 0), (1, 1)], _merge_epilogue, [gab, gab],
                          [pl.BlockSpec((tm2, tn2), lambda i, j: (i, j)),
                           pl.BlockSpec((tm2, tn2), lambda i, j: (i, nga + j))],
                          jax.ShapeDtypeStruct((T, D), BF16), pl.BlockSpec((tm2, tn2), lambda i, j: (i, j)),
                          tm2, tn2, 52, "branch_merge")
    tno = _tile(512, D)
    r = fused_matmul([merged], [p["w_out"].astype(BF16)], [(0, 0)], functools.partial(_residual_epilogue, 1.0),
                     [x, mod],
                     [pl.BlockSpec((tm, tno), lambda i, j: (i, j)),
                      pl.BlockSpec((None, 1, tno), lambda i, j: ((i // tpb) * N_ADA + 5, 0, j))],
                     jax.ShapeDtypeStruct((T, D), F32), pl.BlockSpec((tm, tno), lambda i, j: (i, j)),
                     tm, tno, 52, "out_proj")
    x, h = ln_mod(r, p["ln2_g"], p["ln2_b"], mod, S, 6, 7)

    r = ffn(h, x, p["ffn2_w_gate"], p["ffn2_w_up"], p["ffn2_w_down"], 8)
    return layer_norm(r, p["ln3_g"], p["ln3_b"])


def kernel(x, c, positions, w_ada, b_ada, ffn1_w_gate, ffn1_w_up, ffn1_w_down, w_in, nsa_cmp_pos, nsa_cmp_k_w1, nsa_cmp_k_w2, nsa_cmp_v_w1, nsa_cmp_v_w2, ssd_conv_w, ssd_conv_b, ssd_dt_bias, ssd_a_log, ssd_d, ssd_norm_w, w_branch_a, w_branch_b, w_out, ffn2_w_gate, ffn2_w_up, ffn2_w_down, ln1_g, ln1_b, ln2_g, ln2_b, ln3_g, ln3_b):
    B, S, D = x.shape
    dh = NSA_HEAD_DIM
    per_layer = dict(ffn1_w_gate=ffn1_w_gate, ffn1_w_up=ffn1_w_up, ffn1_w_down=ffn1_w_down, w_in=w_in,
                     nsa_cmp_pos=nsa_cmp_pos, nsa_cmp_k_w1=nsa_cmp_k_w1, nsa_cmp_k_w2=nsa_cmp_k_w2,
                     nsa_cmp_v_w1=nsa_cmp_v_w1, nsa_cmp_v_w2=nsa_cmp_v_w2, ssd_conv_w=ssd_conv_w,
                     ssd_conv_b=ssd_conv_b, ssd_dt_bias=ssd_dt_bias, ssd_a_log=ssd_a_log, ssd_d=ssd_d,
                     ssd_norm_w=ssd_norm_w, w_branch_a=w_branch_a, w_branch_b=w_branch_b, w_out=w_out,
                     ffn2_w_gate=ffn2_w_gate, ffn2_w_up=ffn2_w_up, ffn2_w_down=ffn2_w_down,
                     ln1_g=ln1_g, ln1_b=ln1_b, ln2_g=ln2_g, ln2_b=ln2_b, ln3_g=ln3_g, ln3_b=ln3_b)

    tabs = [t.reshape(B * S, LANES) for t in _rope_lane_tables(positions)]
    n_seg = S // NSA_CMP_STRIDE
    c_end = jnp.minimum(NSA_CMP_STRIDE * jnp.arange(n_seg) + NSA_CMP_BLOCK - 1, S - 1)
    ck, s1k, s2k = _rope_lane_tables(positions[:, c_end])
    tabs_c = [jnp.stack([ck, jnp.ones_like(ck)]), jnp.stack([s1k, jnp.zeros_like(s1k)]),
              jnp.stack([s2k, jnp.zeros_like(s2k)])]

    key_blk = np.arange(S) // NSA_SEL_BLOCK
    e_blk = jnp.asarray(key_blk[:, None] == np.arange(LANES)[None, :], BF16)
    ones_col = jnp.asarray(np.arange(2 * SUBLANES)[None, :] == 0, BF16) * jnp.ones((S, 1), BF16)
    n_cmp = (S - NSA_CMP_BLOCK) // NSA_CMP_STRIDE + 1
    c_start = NSA_CMP_STRIDE * np.arange(n_seg)
    sel_start = NSA_SEL_BLOCK * np.arange(LANES)
    ov = ((c_start[:, None] < sel_start[None, :] + NSA_SEL_BLOCK)
          & (c_start[:, None] + NSA_CMP_BLOCK - 1 >= sel_start[None, :])
          & (np.arange(n_seg)[:, None] < n_cmp) & (np.arange(LANES)[None, :] < S // NSA_SEL_BLOCK))
    consts = (e_blk, ones_col, jnp.asarray(ov, BF16))

    xt = x.reshape(B * S, D)
    for l in range(DEPTH):
        mod = ada_proj(c, w_ada[l], b_ada[l]).reshape(B * N_ADA, 1, D)
        xt = _layer(xt, mod, positions, tabs, tabs_c, consts, {k: v[l] for k, v in per_layer.items()})
    return xt.reshape(B, S, D)
```

```python
import functools
import math

import jax
import jax.numpy as jnp
import numpy as np
from jax import lax
from jax.experimental import pallas as pl
from jax.experimental.pallas import tpu as pltpu

D_MODEL = 4096
BATCH = 2
SEQ = 8192
DEPTH = 1

NSA_HEADS = 32
NSA_KV_GROUPS = 4
NSA_HEAD_DIM = 128
NSA_CMP_BLOCK = 32
NSA_CMP_STRIDE = 16
NSA_CMP_HIDDEN = 256
NSA_SEL_BLOCK = 64
NSA_SEL_TOPN = 16
NSA_WINDOW = 512
ROPE_THETA = 500000.0
ROPE_DIM = NSA_HEAD_DIM // 4
FORCE_BONUS = 1.0e4
NEG_INF = -1.0e30

SSD_D_INNER = D_MODEL
SSD_HEAD_DIM = 64
SSD_HEADS = SSD_D_INNER // SSD_HEAD_DIM
SSD_GROUPS = 8
SSD_STATE = 128
SSD_CONV = 4
SSD_CHUNK = 256
SSD_CONV_CH = SSD_D_INNER + 2 * SSD_GROUPS * SSD_STATE

FFN_DIM = 256 * ((8 * D_MODEL // 3 + 255) // 256)
N_ADA = 9
LN_EPS = 1e-5
RMS_EPS = 1e-5
DEEPNORM_ALPHA = (2 * DEPTH) ** 0.25

NSA_Q_WIDTH = NSA_HEADS * NSA_HEAD_DIM
NSA_KV_WIDTH = NSA_KV_GROUPS * NSA_HEAD_DIM
IN_SPLIT_SIZES = (NSA_Q_WIDTH,) + (NSA_KV_WIDTH,) * 6 + (3 * NSA_HEADS, SSD_D_INNER, SSD_CONV_CH, SSD_HEADS, D_MODEL, D_MODEL)

F32 = jnp.float32
BF16 = jnp.bfloat16
LANES = 128
SUBLANES = 8
LOG2E = 1.4426950408889634
MIB = 1024 * 1024

TM_MATMUL = 1024
TM_DUAL = 512
TM_ROWWISE = 256
ATTN_TQ = 128
ATTN_TK = 512
TM_FFN_UP = 2048
FFN_GU = 256


def _tile(pref, dim):
    t = min(pref, dim)
    assert dim % t == 0, (pref, dim)
    return t


def _params(sem, vmem_mib):
    return pltpu.CompilerParams(dimension_semantics=sem, vmem_limit_bytes=vmem_mib * MIB)


def _sigmoid(x):
    return 1.0 / (1.0 + jnp.exp(-x))


def _silu(x):
    return x * _sigmoid(x)


def _nt_dot(a, b):
    return lax.dot_general(a, b, (((1,), (1,)), ((), ())), preferred_element_type=F32)


def _ada_body(c_ref, w_ref, b_ref, o_ref):
    c = c_ref[...]
    ca = _silu(c).astype(BF16)
    o_ref[...] = jnp.dot(ca, w_ref[...].astype(BF16), preferred_element_type=F32) + b_ref[...]


def ada_proj(c, w_ada, b_ada):
    B, D = c.shape
    N = w_ada.shape[1]
    tn = _tile(512, N)
    c8 = jnp.zeros((SUBLANES, D), F32).at[:B].set(c)
    out = pl.pallas_call(
        _ada_body,
        grid=(N // tn,),
        in_specs=[pl.BlockSpec((SUBLANES, D), lambda j: (0, 0)),
                  pl.BlockSpec((D, tn), lambda j: (0, j)),
                  pl.BlockSpec((1, tn), lambda j: (0, j))],
        out_specs=pl.BlockSpec((SUBLANES, tn), lambda j: (0, j)),
        out_shape=jax.ShapeDtypeStruct((SUBLANES, N), F32),
        compiler_params=_params(("arbitrary",), 40),
        name="ada_proj",
    )(c8, w_ada, b_ada.reshape(1, N))
    return out[:B]


def _mod_body(x_ref, shift_ref, scale_ref, h_ref):
    h_ref[...] = (x_ref[...] * (1.0 + scale_ref[...]) + shift_ref[...]).astype(h_ref.dtype)


def _ln_mod_body(r_ref, g_ref, b_ref, shift_ref, scale_ref, x_ref, h_ref):
    r = r_ref[...]
    mu = jnp.mean(r, axis=-1, keepdims=True)
    d = r - mu
    var = jnp.mean(d * d, axis=-1, keepdims=True)
    x = d * lax.rsqrt(var + LN_EPS) * g_ref[...] + b_ref[...]
    x_ref[...] = x
    h_ref[...] = (x * (1.0 + scale_ref[...]) + shift_ref[...]).astype(h_ref.dtype)


def _ln_body(r_ref, g_ref, b_ref, x_ref):
    r = r_ref[...]
    mu = jnp.mean(r, axis=-1, keepdims=True)
    d = r - mu
    var = jnp.mean(d * d, axis=-1, keepdims=True)
    x_ref[...] = d * lax.rsqrt(var + LN_EPS) * g_ref[...] + b_ref[...]


def _mod_spec(D, tiles_per_batch, slot):
    return pl.BlockSpec((None, 1, D), lambda i: ((i // tiles_per_batch) * N_ADA + slot, 0, 0))


def modulate(x, mod, seq, slot_shift, slot_scale):
    T, D = x.shape
    tm = _tile(TM_ROWWISE, seq)
    tpb = seq // tm
    return pl.pallas_call(
        _mod_body,
        grid=(T // tm,),
        in_specs=[pl.BlockSpec((tm, D), lambda i: (i, 0)),
                  _mod_spec(D, tpb, slot_shift), _mod_spec(D, tpb, slot_scale)],
        out_specs=pl.BlockSpec((tm, D), lambda i: (i, 0)),
        out_shape=jax.ShapeDtypeStruct((T, D), BF16),
        compiler_params=_params(("parallel",), 32),
        name="modulate",
    )(x, mod, mod)


def ln_mod(r, g, b, mod, seq, slot_shift, slot_scale):
    T, D = r.shape
    tm = _tile(TM_ROWWISE, seq)
    tpb = seq // tm
    row = pl.BlockSpec((tm, D), lambda i: (i, 0))
    vec = pl.BlockSpec((1, D), lambda i: (0, 0))
    return pl.pallas_call(
        _ln_mod_body,
        grid=(T // tm,),
        in_specs=[row, vec, vec, _mod_spec(D, tpb, slot_shift), _mod_spec(D, tpb, slot_scale)],
        out_specs=[row, row],
        out_shape=[jax.ShapeDtypeStruct((T, D), F32), jax.ShapeDtypeStruct((T, D), BF16)],
        compiler_params=_params(("parallel",), 40),
        name="ln_mod",
    )(r, g.reshape(1, D), b.reshape(1, D), mod, mod)


def layer_norm(r, g, b):
    T, D = r.shape
    tm = _tile(TM_ROWWISE, T)
    row = pl.BlockSpec((tm, D), lambda i: (i, 0))
    vec = pl.BlockSpec((1, D), lambda i: (0, 0))
    return pl.pallas_call(
        _ln_body,
        grid=(T // tm,),
        in_specs=[row, vec, vec],
        out_specs=row,
        out_shape=jax.ShapeDtypeStruct((T, D), F32),
        compiler_params=_params(("parallel",), 32),
        name="layer_norm",
    )(r, g.reshape(1, D), b.reshape(1, D))


def _fused_matmul_body(nx, nw, ne, pairs, epilogue, *refs):
    x_refs = refs[:nx]
    w_refs = refs[nx:nx + nw]
    e_refs = refs[nx + nw:nx + nw + ne]
    o_refs = refs[nx + nw + ne:]
    accs = [jnp.dot(x_refs[a][...], w_refs[b][...], preferred_element_type=F32) for a, b in pairs]
    epilogue(accs, e_refs, o_refs)


def fused_matmul(xs, ws, pairs, epilogue, extras, extra_specs, out_shapes, out_specs, tm, tn, vmem_mib, name,
                 w_cols=None):
    M = xs[0].shape[0]
    col0, N = w_cols if w_cols is not None else (0, ws[0].shape[1])
    assert col0 % tn == 0 and N % tn == 0 and M % tm == 0
    j0 = col0 // tn
    in_specs = ([pl.BlockSpec((tm, x.shape[1]), lambda i, j: (i, 0)) for x in xs]
                + [pl.BlockSpec((w.shape[0], tn), lambda i, j: (0, j0 + j)) for w in ws]
                + list(extra_specs))
    body = functools.partial(_fused_matmul_body, len(xs), len(ws), len(extras), tuple(pairs), epilogue)
    return pl.pallas_call(
        body,
        grid=(M // tm, N // tn),
        in_specs=in_specs,
        out_specs=out_specs,
        out_shape=out_shapes,
        compiler_params=_params(("parallel", "arbitrary"), vmem_mib),
        name=name,
    )(*xs, *ws, *extras)


def _swiglu_epilogue(accs, e_refs, o_refs):
    g, u = accs
    o_refs[0][...] = (_silu(g) * u).astype(o_refs[0].dtype)


def ffn_up(h, wg, wu):
    T, D = h.shape
    F = wg.shape[1]
    tm = _tile(TM_FFN_UP, T)
    return fused_matmul([h], [wg, wu], [(0, 0), (0, 1)], _swiglu_epilogue, [], [],
                        jax.ShapeDtypeStruct((T, F), BF16), pl.BlockSpec((tm, FFN_GU), lambda i, j: (i, j)),
                        tm, FFN_GU, 58, "ffn_up")


def _residual_epilogue(coef, accs, e_refs, o_refs):
    x_ref, gate_ref = e_refs
    o_refs[0][...] = DEEPNORM_ALPHA * x_ref[...] + (coef * gate_ref[...]) * accs[0]


def ffn_down(a, wd, x, mod, seq, slot_gate, coef):
    T, D = x.shape
    tm, tn = _tile(TM_DUAL, seq), _tile(512, D)
    tpb = seq // tm
    return fused_matmul([a], [wd], [(0, 0)], functools.partial(_residual_epilogue, coef), [x, mod],
                        [pl.BlockSpec((tm, tn), lambda i, j: (i, j)),
                         pl.BlockSpec((None, 1, tn), lambda i, j: ((i // tpb) * N_ADA + slot_gate, 0, j))],
                        jax.ShapeDtypeStruct((T, D), F32), pl.BlockSpec((tm, tn), lambda i, j: (i, j)),
                        tm, tn, 58, "ffn_down")


def _rope(t, c, s1, s2):
    half = ROPE_DIM // 2
    return t * c + pltpu.roll(t, LANES - half, axis=1) * s1 + pltpu.roll(t, half, axis=1) * s2


def _q_epilogue(n_heads, accs, e_refs, o_refs):
    c, s1, s2 = (r[...] for r in e_refs)
    qscale = NSA_HEAD_DIM ** -0.5 * LOG2E
    for h in range(n_heads):
        t = accs[0][:, h * LANES:(h + 1) * LANES]
        o_refs[0][h] = (_rope(t, c, s1, s2) * qscale).astype(o_refs[0].dtype)


def _kv_plain_epilogue(n_groups, accs, e_refs, o_refs):
    for g in range(n_groups):
        o_refs[0][g] = accs[0][:, g * LANES:(g + 1) * LANES].astype(o_refs[0].dtype)


def _kv_rope_epilogue(n_groups, accs, e_refs, o_refs):
    c, s1, s2 = (r[...] for r in e_refs)
    for g in range(n_groups):
        t = accs[0][:, g * LANES:(g + 1) * LANES]
        o_refs[0][g] = _rope(t, c, s1, s2).astype(o_refs[0].dtype)


def _plain_epilogue(accs, e_refs, o_refs):
    o_refs[0][...] = accs[0].astype(o_refs[0].dtype)


def _sigmoid_epilogue(accs, e_refs, o_refs):
    o_refs[0][...] = _sigmoid(accs[0]).astype(o_refs[0].dtype)


def _small_epilogue(accs, e_refs, o_refs):
    a = accs[0]
    lane = lax.broadcasted_iota(jnp.int32, a.shape, 1)
    o_refs[0][...] = jnp.where(lane >= LANES, _sigmoid(a), a)


def _compress_body(seg_ref, w1_ref, w2_ref, pos_ref, c_ref, s1_ref, s2_ref, o_ref):
    n_seg = seg_ref.shape[0]
    half_k = seg_ref.shape[1]
    seg = seg_ref[...].astype(BF16)
    top = jnp.dot(seg, w1_ref[0:half_k, :], preferred_element_type=F32)
    bot = jnp.dot(seg, w1_ref[half_k:2 * half_k, :], preferred_element_type=F32)
    cb = jnp.dot(pos_ref[...], w1_ref[...], preferred_element_type=F32)[0:1, :]
    hid = _silu(top + pltpu.roll(bot, n_seg - 1, axis=0) + cb).astype(BF16)
    out = jnp.dot(hid, w2_ref[...], preferred_element_type=F32)
    out = _rope(out, c_ref[...], s1_ref[...], s2_ref[...])
    rowi = lax.broadcasted_iota(jnp.int32, out.shape, 0)
    o_ref[...] = jnp.where(rowi < n_seg - 1, out, 0.0).astype(o_ref.dtype)


def nsa_compress(kvseg, w1, w2, pos8, tabs):
    _, B, G, n_seg, K2 = kvseg.shape
    dh = NSA_HEAD_DIM
    hid = w1.shape[-1]
    tab_spec = pl.BlockSpec((None, None, n_seg, dh), lambda kv, b, g: (kv, b, 0, 0))
    return pl.pallas_call(
        _compress_body,
        grid=(2, B, G),
        in_specs=[pl.BlockSpec((None, None, None, n_seg, K2), lambda kv, b, g: (kv, b, g, 0, 0)),
                  pl.BlockSpec((None, 2 * K2, hid), lambda kv, b, g: (kv, 0, 0)),
                  pl.BlockSpec((None, hid, dh), lambda kv, b, g: (kv, 0, 0)),
                  pl.BlockSpec((SUBLANES, 2 * K2), lambda kv, b, g: (0, 0)),
                  tab_spec, tab_spec, tab_spec],
        out_specs=pl.BlockSpec((None, None, None, n_seg, dh), lambda kv, b, g: (kv, b, g, 0, 0)),
        out_shape=jax.ShapeDtypeStruct((2, B, G, n_seg, dh), BF16),
        compiler_params=_params(("arbitrary", "arbitrary", "arbitrary"), 40),
        name="nsa_compress",
    )(kvseg, w1, w2, pos8, *tabs)


def _attn_body(q_ref, kc_ref, vcxt_ref, kx_ref, vxt_ref, kw_ref, vwxt_ref, gate_ref, o_ref,
               qxt_ref, s0_ref, s1_ref, mt0_ref, mt1_ref, m_ref, acc_ref, out_ref, *, tq, tk, wk, hpg, top_n):
    dh = NSA_HEAD_DIM
    R = hpg * tq
    hw = R // 2
    n_cp = kc_ref.shape[0]
    t0 = pl.program_id(2) * tq
    t_q = t0 + lax.broadcasted_iota(jnp.int32, (1, tq), 1)
    for h in range(hpg):
        qxt_ref[0:dh, h * tq:(h + 1) * tq] = q_ref[h].astype(F32).T.astype(BF16)
    qt = qxt_ref[0:dh, :]
    gates = gate_ref[...]

    def gate_row(branch):
        return jnp.concatenate([gates[branch * hpg + h:branch * hpg + h + 1, :] for h in range(hpg)], axis=1)

    def all_heads(x):
        return jnp.concatenate([x] * hpg, axis=1)

    c_end = lax.broadcasted_iota(jnp.int32, (n_cp, 1), 0) * NSA_CMP_STRIDE + (NSA_CMP_BLOCK - 1)
    s_c = jnp.dot(kc_ref[...], qt, preferred_element_type=F32) + all_heads(jnp.where(c_end <= t_q, 0.0, NEG_INF))
    mx_c = jnp.max(s_c, axis=0, keepdims=True)
    vcxt = vcxt_ref[...]
    g_c = gate_row(0)
    imp_t = jnp.zeros((LANES, tq), F32)
    for half in range(2):
        cols = slice(half * hw, (half + 1) * hw)
        e = jnp.exp2(s_c[:, cols] - mx_c[:, cols])
        inv = jnp.where(mx_c[:, cols] > 0.5 * NEG_INF, 1.0 / jnp.sum(e, axis=0, keepdims=True), 0.0)
        r = jnp.dot(vcxt, e.astype(BF16), preferred_element_type=F32)
        out_ref[:, cols] = r[0:dh, :] * (g_c[:, cols] * inv)
        ri = r[dh:2 * dh, :] * inv
        for h in range(hpg // 2):
            imp_t = imp_t + ri[:, h * tq:(h + 1) * tq]

    w0 = pl.multiple_of(jnp.maximum(t0 + tq - wk, 0), tq)
    c0 = w0 // tq
    kpos_w = w0 + lax.broadcasted_iota(jnp.int32, (wk, 1), 0)
    w_bias = jnp.where((kpos_w <= t_q) & (kpos_w > t_q - NSA_WINDOW), 0.0, NEG_INF)
    s_w = jnp.dot(kw_ref[pl.ds(w0, wk), :], qt, preferred_element_type=F32) + all_heads(w_bias)
    mx_w = jnp.max(s_w, axis=0, keepdims=True)

    j_i = lax.broadcasted_iota(jnp.int32, (LANES, 1), 0)
    j_f = j_i.astype(F32)
    cur = t_q // NSA_SEL_BLOCK
    causal_blk = j_i * NSA_SEL_BLOCK <= t_q
    forced = (j_i == 0) | (j_i == cur) | (j_i == cur - 1)
    work = jnp.where(causal_blk & jnp.logical_not(forced), imp_t, NEG_INF)
    sel = forced
    for _ in range(top_n - 3):
        mx = jnp.max(work, axis=0, keepdims=True)
        first = jnp.min(jnp.where(work == mx, j_f, float(LANES)), axis=0, keepdims=True)
        pick = j_f == first
        sel = sel | pick
        work = jnp.where(pick, -jnp.inf, work)
    bias_t = jnp.where(sel & causal_blk, 0.0, NEG_INF).astype(BF16)
    for h in range(hpg):
        qxt_ref[dh:2 * dh, h * tq:(h + 1) * tq] = bias_t

    s_refs = (s0_ref, s1_ref)
    mt_refs = (mt0_ref, mt1_ref)

    def score_tile(kt, slot):
        k0 = pl.multiple_of(kt * tk, tk)
        kpos = k0 + lax.broadcasted_iota(jnp.int32, (tk, 1), 0)
        s = jnp.dot(kx_ref[pl.ds(k0, tk), :], qxt_ref[...], preferred_element_type=F32)
        s = s + all_heads(jnp.where(kpos <= t_q, 0.0, NEG_INF))
        s_refs[slot][...] = s
        mt_refs[slot][...] = jnp.max(s, axis=0, keepdims=True)

    def softmax_pv(kt, slot):
        vt = vxt_ref[kt]
        for half in range(2):
            cols = slice(half * hw, (half + 1) * hw)
            m_old = m_ref[:, cols]
            m_new = jnp.maximum(m_old, mt_refs[slot][:, cols])
            p = jnp.exp2(s_refs[slot][:, cols] - m_new).astype(BF16)
            pv = jnp.dot(vt, p, preferred_element_type=F32)
            acc_ref[:, cols] = jnp.exp2(m_old - m_new) * acc_ref[:, cols] + pv
            m_ref[:, cols] = m_new

    score_tile(0, 0)

    vwt = jnp.concatenate([vwxt_ref[c0 + i] for i in range(wk // tq)], axis=1)
    g_w = gate_row(2)
    for half in range(2):
        cols = slice(half * hw, (half + 1) * hw)
        e = jnp.exp2(s_w[:, cols] - mx_w[:, cols]).astype(BF16)
        r = jnp.dot(vwt, e, preferred_element_type=F32)
        inv = jnp.where(mx_w[:, cols] > 0.5 * NEG_INF, 1.0 / r[dh:dh + 1, :], 0.0)
        out_ref[:, cols] += r[0:dh, :] * (g_w[:, cols] * inv)

    m_ref[...] = jnp.full(m_ref.shape, NEG_INF, F32)
    acc_ref[...] = jnp.zeros(acc_ref.shape, F32)
    n_full = t0 // tk

    def two_tiles(i, carry):
        score_tile(2 * i + 1, 1)
        softmax_pv(2 * i, 0)
        score_tile(2 * i + 2, 0)
        softmax_pv(2 * i + 1, 1)
        return carry

    lax.fori_loop(0, n_full // 2, two_tiles, 0)

    @pl.when(n_full % 2 == 0)
    def _():
        softmax_pv(n_full, 0)

    @pl.when(n_full % 2 == 1)
    def _():
        score_tile(n_full, 1)
        softmax_pv(n_full - 1, 0)
        softmax_pv(n_full, 1)

    for h in range(hpg):
        cols = slice(h * tq, (h + 1) * tq)
        acc = acc_ref[:, cols]
        l_s = acc[dh:dh + 1, :]
        ok = (m_ref[:, cols] > 0.5 * NEG_INF) & (l_s > 0.0)
        o_t = out_ref[:, cols] + acc[0:dh, :] * (gates[hpg + h:hpg + h + 1, :] * jnp.where(ok, 1.0 / l_s, 0.0))
        o_ref[:, h * dh:(h + 1) * dh] = o_t.T.astype(o_ref.dtype)


def nsa_attention(q, kvc, vcxt, kx, vxt, sw, vwxt, gates_t, seq):
    B, H, S, dh = q.shape
    G = kx.shape[1]
    hpg = H // G
    n_cp = kvc.shape[3]
    tq = _tile(ATTN_TQ, S)
    tk = _tile(ATTN_TK, S)
    wk = min(NSA_WINDOW + tq, S)
    assert tk % tq == 0 and NSA_WINDOW % tq == 0 and tq % NSA_SEL_BLOCK == 0 and S // NSA_SEL_BLOCK <= LANES
    assert hpg % 2 == 0 and wk % tq == 0 and tq == LANES
    top_n = min(NSA_SEL_TOPN, S // NSA_SEL_BLOCK)
    assert top_n > 3 and hpg < FORCE_BONUS
    R = hpg * tq
    nq = S // tq
    vr = vxt.shape[3]
    body = functools.partial(_attn_body, tq=tq, tk=tk, wk=wk, hpg=hpg, top_n=top_n)
    per_group = lambda *blk: pl.BlockSpec((None, None) + blk, lambda b, g, i: (b, g) + (0,) * len(blk))
    return pl.pallas_call(
        body,
        grid=(B, G, nq),
        in_specs=[pl.BlockSpec((None, hpg, tq, dh), lambda b, g, i: (b, g, i, 0)),
                  pl.BlockSpec((None, None, None, n_cp, dh), lambda b, g, i: (0, b, g, 0, 0)),
                  per_group(2 * dh, n_cp),
                  per_group(S, 2 * dh), per_group(S // tk, vr, tk),
                  pl.BlockSpec((None, None, None, S, dh), lambda b, g, i: (2, b, g, 0, 0)),
                  per_group(S // tq, vr, tq),
                  pl.BlockSpec((None, None, 3 * hpg, tq), lambda b, g, i: (b, g, 0, i))],
        out_specs=pl.BlockSpec((tq, hpg * dh), lambda b, g, i: (b * nq + i, g)),
        out_shape=jax.ShapeDtypeStruct((B * S, H * dh), BF16),
        scratch_shapes=[pltpu.VMEM((2 * dh, R), BF16),
                        pltpu.VMEM((tk, R), F32),
                        pltpu.VMEM((tk, R), F32),
                        pltpu.VMEM((1, R), F32),
                        pltpu.VMEM((1, R), F32),
                        pltpu.VMEM((1, R), F32),
                        pltpu.VMEM((vr, R), F32),
                        pltpu.VMEM((dh, R), F32)],
        compiler_params=_params(("parallel", "parallel", "arbitrary"), 56),
        name="nsa_attention",
    )(q, kvc, vcxt, kx, vxt, sw, vwxt, gates_t)


def _split3(x):
    hi = x.astype(BF16)
    r = x - hi.astype(F32)
    mid = r.astype(BF16)
    lo = (r - mid.astype(F32)).astype(BF16)
    return hi, mid, lo


def _softplus(x):
    return jnp.maximum(x, 0.0) + jnp.log1p(jnp.exp(-jnp.abs(x)))


def _ssd_body(x_ref, xh_ref, b_ref, bh_ref, c_ref, ch_ref, z_ref, dt_ref, dtt_ref,
              wx_ref, wb_ref, wc_ref, cbx_ref, cbb_ref, cbc_ref,
              dtb_ref, dtbt_ref, alog_ref, alogt_ref, dsk_ref, nw_ref, o_ref,
              state_ref, pad_ref, y_ref, acum_ref, dtg_ref, acumt_ref, dtt_s_ref, *, L, hpg, n_groups):
    P, N = SSD_HEAD_DIM, SSD_STATE
    GW = hpg * P
    HALO = SUBLANES
    c = pl.program_id(1)
    g = pl.program_id(2)
    n_heads = hpg * n_groups

    @pl.when((c == 0) & (g == 0))
    def _():
        state_ref[...] = jnp.zeros(state_ref.shape, F32)

    ri = lax.broadcasted_iota(jnp.int32, (L, L), 0)
    ci = lax.broadcasted_iota(jnp.int32, (L, L), 1)
    tril = ri >= ci

    @pl.when(g == 0)
    def _():
        dt = _softplus(dt_ref[:, 0:n_heads] + dtb_ref[...])
        da = dt * (-jnp.exp(alog_ref[...]))
        dtt = _softplus(dtt_ref[...] + dtbt_ref[...])
        dat = dtt * (-jnp.exp(alogt_ref[...]))
        lower = tril.astype(BF16)
        upper = (ri <= ci).astype(BF16)
        acum = sum(jnp.dot(lower, part, preferred_element_type=F32) for part in _split3(da))
        acumt_ref[...] = sum(jnp.dot(part, upper, preferred_element_type=F32) for part in _split3(dat))
        dtt_s_ref[...] = dtt
        for gg in range(n_groups):
            acum_ref[gg, :, 0:hpg] = acum[:, gg * hpg:(gg + 1) * hpg]
            dtg_ref[gg, :, 0:hpg] = dt[:, gg * hpg:(gg + 1) * hpg]

    first = c == 0
    pad_ref[0:HALO, 0:GW] = jnp.where(first, 0.0, xh_ref[...])
    pad_ref[0:HALO, GW:GW + N] = jnp.where(first, 0.0, bh_ref[...])
    pad_ref[0:HALO, GW + N:GW + 2 * N] = jnp.where(first, 0.0, ch_ref[...])
    pad_ref[HALO:HALO + L, 0:GW] = x_ref[...]
    pad_ref[HALO:HALO + L, GW:GW + N] = b_ref[...]
    pad_ref[HALO:HALO + L, GW + N:GW + 2 * N] = c_ref[...]

    def conv(lo, hi, w_ref, bias_ref):
        out = bias_ref[...]
        for k in range(SSD_CONV):
            off = HALO - (SSD_CONV - 1) + k
            out = out + w_ref[k:k + 1, :] * pad_ref[off:off + L, lo:hi]
        return _silu(out)

    xa = conv(0, GW, wx_ref, cbx_ref)
    bg = conv(GW, GW + N, wb_ref, cbb_ref).astype(BF16)
    cg = conv(GW + N, GW + 2 * N, wc_ref, cbc_ref).astype(BF16)
    xat = xa.T
    cb = _nt_dot(cg, bg)
    head_lane = lax.broadcasted_iota(jnp.int32, (1, LANES), 1) < hpg
    acg = jnp.where(head_lane, acum_ref[g], 0.0)
    dtg = jnp.where(head_lane, dtg_ref[g], 0.0)
    h0 = pl.multiple_of(g * hpg, hpg)
    act = acumt_ref[pl.ds(h0, hpg), :]
    dtt = dtt_s_ref[pl.ds(h0, hpg), :]
    dsk = dsk_ref[...]
    expand = (lax.broadcasted_iota(jnp.int32, (LANES, GW), 1) // P
              == lax.broadcasted_iota(jnp.int32, (LANES, GW), 0)).astype(BF16)
    dtx = sum(jnp.dot(part, expand, preferred_element_type=F32) for part in _split3(dtg))
    ecx = sum(jnp.dot(part, expand, preferred_element_type=F32) for part in _split3(jnp.exp(acg)))
    xdt_all = xa * dtx
    first_lane = lax.broadcasted_iota(jnp.int32, (1, 2 * P), 1) < P
    first_row = lax.broadcasted_iota(jnp.int32, (2 * P, 1), 0) < P
    for hp in range(hpg // 2):
        cols = slice(hp * 2 * P, (hp + 1) * 2 * P)
        xdt = xdt_all[:, cols]
        y = jnp.zeros((L, 2 * P), F32)
        w_rows = []
        for k in range(2):
            h = 2 * hp + k
            col = acg[:, h:h + 1]
            row = act[h:h + 1, :]
            decay = jnp.where(tril, jnp.exp(col - row), 0.0)
            own = first_lane if k == 0 else jnp.logical_not(first_lane)
            y = y + jnp.dot((cb * decay).astype(BF16), jnp.where(own, xdt, 0.0).astype(BF16),
                            preferred_element_type=F32)
            a_last = act[h:h + 1, L - 1:L]
            w_rows.append((dtt[h:h + 1, :] * jnp.exp(a_last - row), jnp.exp(a_last)))
        s0 = pl.multiple_of((g * hpg + 2 * hp) * P, 2 * P)
        st = state_ref[pl.ds(s0, 2 * P), :]
        y = y + _nt_dot(cg, st.astype(BF16)) * ecx[:, cols]
        xw_t = (xat[hp * 2 * P:(hp + 1) * 2 * P, :] * jnp.where(first_row, w_rows[0][0], w_rows[1][0])).astype(BF16)
        state_ref[pl.ds(s0, 2 * P), :] = (st * jnp.where(first_row, w_rows[0][1], w_rows[1][1])
                                          + jnp.dot(xw_t, bg, preferred_element_type=F32))
        y_ref[:, cols] = y + dsk[:, cols] * xa[:, cols]

    yg = y_ref[...] * _silu(z_ref[...])
    ms = jnp.mean(yg * yg, axis=-1, keepdims=True)
    o_ref[...] = (yg * lax.rsqrt(ms + RMS_EPS) * nw_ref[...]).astype(o_ref.dtype)


def ssd_mixer(z, xbc, small, dt_t, conv_w, conv_b, dt_bias, a_log, d_skip, norm_w, batch, seq):
    T, Di = z.shape
    Gs, N, P, Hs = SSD_GROUPS, SSD_STATE, SSD_HEAD_DIM, SSD_HEADS
    hpg = Hs // Gs
    GW = hpg * P
    L = math.gcd(seq, SSD_CHUNK)
    nc = seq // L
    HALO = SUBLANES
    assert GW % N == 0 and Hs <= LANES and hpg % 2 == 0 and 2 * P == LANES
    b_off = Di // N
    c_off = b_off + Gs

    def rows(b, c, g):
        return b * nc + c

    def halo_rows(b, c, g):
        return jnp.maximum((b * seq + c * L) // HALO - 1, 0)

    in_specs = [
        pl.BlockSpec((L, GW), lambda b, c, g: (rows(b, c, g), g)),
        pl.BlockSpec((HALO, GW), lambda b, c, g: (halo_rows(b, c, g), g)),
        pl.BlockSpec((L, N), lambda b, c, g: (rows(b, c, g), b_off + g)),
        pl.BlockSpec((HALO, N), lambda b, c, g: (halo_rows(b, c, g), b_off + g)),
        pl.BlockSpec((L, N), lambda b, c, g: (rows(b, c, g), c_off + g)),
        pl.BlockSpec((HALO, N), lambda b, c, g: (halo_rows(b, c, g), c_off + g)),
        pl.BlockSpec((L, GW), lambda b, c, g: (rows(b, c, g), g)),
        pl.BlockSpec((L, LANES), lambda b, c, g: (rows(b, c, g), 0)),
        pl.BlockSpec((None, Hs, L), lambda b, c, g: (b, 0, c)),
        pl.BlockSpec((SSD_CONV, GW), lambda b, c, g: (0, g)),
        pl.BlockSpec((SSD_CONV, N), lambda b, c, g: (0, b_off + g)),
        pl.BlockSpec((SSD_CONV, N), lambda b, c, g: (0, c_off + g)),
        pl.BlockSpec((1, GW), lambda b, c, g: (0, g)),
        pl.BlockSpec((1, N), lambda b, c, g: (0, b_off + g)),
        pl.BlockSpec((1, N), lambda b, c, g: (0, c_off + g)),
        pl.BlockSpec((1, Hs), lambda b, c, g: (0, 0)),
        pl.BlockSpec((Hs, 1), lambda b, c, g: (0, 0)),
        pl.BlockSpec((1, Hs), lambda b, c, g: (0, 0)),
        pl.BlockSpec((Hs, 1), lambda b, c, g: (0, 0)),
        pl.BlockSpec((1, GW), lambda b, c, g: (0, g)),
        pl.BlockSpec((1, GW), lambda b, c, g: (0, g)),
    ]
    cb2 = conv_b.reshape(1, -1)
    body = functools.partial(_ssd_body, L=L, hpg=hpg, n_groups=Gs)
    return pl.pallas_call(
        body,
        grid=(batch, nc, Gs),
        in_specs=in_specs,
        out_specs=pl.BlockSpec((L, GW), lambda b, c, g: (rows(b, c, g), g)),
        out_shape=jax.ShapeDtypeStruct((T, Di), BF16),
        scratch_shapes=[pltpu.VMEM((Hs * P, N), F32),
                        pltpu.VMEM((HALO + L, GW + 2 * N), F32),
                        pltpu.VMEM((L, GW), F32),
                        pltpu.VMEM((Gs, L, LANES), F32),
                        pltpu.VMEM((Gs, L, LANES), F32),
                        pltpu.VMEM((Hs, L), F32),
                        pltpu.VMEM((Hs, L), F32)],
        compiler_params=_params(("arbitrary", "arbitrary", "arbitrary"), 40),
        name="ssd",
    )(xbc, xbc, xbc, xbc, xbc, xbc, z, small, dt_t,
      conv_w, conv_w, conv_w, cb2, cb2, cb2,
      dt_bias.reshape(1, Hs), dt_bias.reshape(Hs, 1), a_log.reshape(1, Hs), a_log.reshape(Hs, 1),
      jnp.repeat(d_skip, P).reshape(1, Di), norm_w.reshape(1, Di))


def _merge_epilogue(accs, e_refs, o_refs):
    ga, gb = e_refs
    o_refs[0][...] = (ga[...].astype(F32) * accs[0] + gb[...].astype(F32) * accs[1]).astype(o_refs[0].dtype)


def _rope_lane_tables(pos):
    half = ROPE_DIM // 2
    inv_freq = jnp.float32(ROPE_THETA) ** (-jnp.arange(half, dtype=F32) / half)
    ang = pos.astype(F32)[..., None] * inv_freq
    cos, sin = jnp.cos(ang), jnp.sin(ang)
    pad = [(0, 0)] * (cos.ndim - 1)
    one = jnp.ones(cos.shape[:-1] + (LANES - ROPE_DIM,), F32)
    c = jnp.concatenate([cos, cos, one], axis=-1)
    s1 = jnp.pad(-sin, pad + [(0, LANES - half)])
    s2 = jnp.pad(sin, pad + [(half, LANES - ROPE_DIM)])
    return c, s1, s2


def _pad_cols(w, n):
    return jnp.pad(w, ((0, 0), (0, n - w.shape[1])))


def _layer(x, mod, positions, tabs, tabs_c, consts, p):
    B, S = positions.shape
    T, D = x.shape
    H, G, dh = NSA_HEADS, NSA_KV_GROUPS, NSA_HEAD_DIM
    hpg = H // G
    tm = _tile(TM_MATMUL, S)
    tpb = S // tm

    def ffn(h, x_res, wg, wu, wd, slot_gate):
        a = ffn_up(h, wg.astype(BF16), wu.astype(BF16))
        return ffn_down(a, wd.astype(BF16), x_res, mod, S, slot_gate, 0.5)

    h = modulate(x, mod, S, 0, 1)
    r = ffn(h, x, p["ffn1_w_gate"], p["ffn1_w_up"], p["ffn1_w_down"], 2)
    x, h = ln_mod(r, p["ln1_g"], p["ln1_b"], mod, S, 3, 4)

    w_in = p["w_in"]
    qkv_w = NSA_Q_WIDTH + 6 * NSA_KV_WIDTH
    w_head = w_in.astype(BF16)
    c_gn, c_z, c_xbc, c_dt, c_ga, c_gb = (qkv_w + int(v) for v in np.cumsum((0,) + IN_SPLIT_SIZES[7:])[:-1])
    w_zx = w_head[:, c_z:c_ga]
    w_gab = w_head[:, c_ga:]
    w_small = jnp.concatenate([_pad_cols(w_head[:, c_dt:c_ga], LANES), _pad_cols(w_head[:, c_gn:c_z], LANES)], axis=1)
    tab_specs = [pl.BlockSpec((tm, LANES), lambda i, j: (i, 0))] * 3

    tnq = hpg * dh
    q = fused_matmul([h], [w_head], [(0, 0)], functools.partial(_q_epilogue, hpg), tabs, tab_specs,
                     jax.ShapeDtypeStruct((B, H, S, dh), BF16),
                     pl.BlockSpec((None, hpg, tm, dh), lambda i, j: (i // tpb, j, i % tpb, 0)),
                     tm, tnq, 52, "in_proj_q", w_cols=(0, NSA_Q_WIDTH))
    tnk = G * dh
    kv_spec = pl.BlockSpec((None, None, G, tm, dh), lambda i, j: (j, i // tpb, 0, i % tpb, 0))
    kvcmp = fused_matmul([h], [w_head], [(0, 0)], functools.partial(_kv_plain_epilogue, G), [], [],
                         jax.ShapeDtypeStruct((2, B, G, S, dh), F32), kv_spec, tm, tnk, 52, "in_proj_kvcmp",
                         w_cols=(NSA_Q_WIDTH, 2 * tnk))
    ident = (jnp.ones_like(tabs[0]), jnp.zeros_like(tabs[0]), jnp.zeros_like(tabs[0]))
    tabs_kv = [jnp.stack([t, e]) for t, e in zip(tabs, ident)]
    tab_kv_specs = [pl.BlockSpec((None, tm, LANES), lambda i, j: (j % 2, i, 0))] * 3
    sw = fused_matmul([h], [w_head], [(0, 0)], functools.partial(_kv_rope_epilogue, G), tabs_kv, tab_kv_specs,
                      jax.ShapeDtypeStruct((4, B, G, S, dh), BF16), kv_spec, tm, tnk, 52, "in_proj_slcwin",
                      w_cols=(NSA_Q_WIDTH + 2 * tnk, 4 * tnk))
    plain_spec = lambda tn: pl.BlockSpec((tm, tn), lambda i, j: (i, j))
    tnt = math.gcd(1024, SSD_D_INNER, SSD_CONV_CH, 2 * D)
    z = fused_matmul([h], [w_zx], [(0, 0)], _plain_epilogue, [], [],
                     jax.ShapeDtypeStruct((T, SSD_D_INNER), F32), plain_spec(tnt), tm, tnt, 52, "in_proj_z",
                     w_cols=(0, SSD_D_INNER))
    xbc = fused_matmul([h], [w_zx], [(0, 0)], _plain_epilogue, [], [],
                       jax.ShapeDtypeStruct((T, SSD_CONV_CH), F32), plain_spec(tnt), tm, tnt, 52, "in_proj_xbc",
                       w_cols=(SSD_D_INNER, SSD_CONV_CH))
    gab = fused_matmul([h], [w_gab], [(0, 0)], _sigmoid_epilogue, [], [],
                       jax.ShapeDtypeStruct((T, 2 * D), BF16), plain_spec(tnt), tm, tnt, 52, "in_proj_gates")
    small = fused_matmul([h], [w_small], [(0, 0)], _small_epilogue, [], [],
                         jax.ShapeDtypeStruct((T, 2 * LANES), F32), plain_spec(2 * LANES), tm, 2 * LANES, 52,
                         "in_proj_small")

    n_seg = S // NSA_CMP_STRIDE
    kvseg = kvcmp.reshape(2, B, G, n_seg, NSA_CMP_STRIDE * dh)
    w1 = jnp.stack([p["nsa_cmp_k_w1"], p["nsa_cmp_v_w1"]]).astype(BF16)
    w2 = jnp.stack([p["nsa_cmp_k_w2"], p["nsa_cmp_v_w2"]]).astype(BF16)
    pos8 = jnp.broadcast_to(p["nsa_cmp_pos"].reshape(1, -1), (SUBLANES, NSA_CMP_BLOCK * dh)).astype(BF16)
    kvc = nsa_compress(kvseg, w1, w2, pos8, tabs_c)
    e_blk, ones_col, overlap = consts
    ones4 = jnp.broadcast_to(ones_col, (B, G, S, ones_col.shape[-1]))
    kx = jnp.concatenate([sw[0], jnp.broadcast_to(e_blk, (B, G, S, dh))], axis=-1)

    def chunked_t(v, chunk):
        return jnp.swapaxes(v.reshape(B, G, S // chunk, chunk, v.shape[-1]), -1, -2)

    vxt = chunked_t(jnp.concatenate([sw[1], ones4], axis=-1), _tile(ATTN_TK, S))
    vwxt = chunked_t(jnp.concatenate([sw[3], ones4], axis=-1), _tile(ATTN_TQ, S))
    vcxt = jnp.swapaxes(jnp.concatenate([kvc[1], jnp.broadcast_to(overlap, (B, G, n_seg, LANES))], axis=-1), -1, -2)
    g3 = small[:, LANES:LANES + 3 * H].reshape(B, S, 3, G, hpg)
    gates_t = jnp.transpose(g3, (0, 3, 2, 4, 1)).reshape(B, G, 3 * hpg, S)
    o_a = nsa_attention(q, kvc, vcxt, kx, vxt, sw, vwxt, gates_t, S)

    dt_t = jnp.transpose(small[:, :SSD_HEADS].reshape(B, S, SSD_HEADS), (0, 2, 1))
    o_b = ssd_mixer(z, xbc, small, dt_t, p["ssd_conv_w"], p["ssd_conv_b"], p["ssd_dt_bias"], p["ssd_a_log"],
                    p["ssd_d"], p["ssd_norm_w"], B, S)

    tm2 = _tile(TM_DUAL, S)
    tn2 = _tile(512, D)
    nga = D // tn2
    merged = fused_matmul([o_a, o_b], [p["w_branch_a"].astype(BF16), p["w_branch_b"].astype(BF16)],
                          [(0, 0), (1, 1)], _merge_epilogue, [gab, gab],
                          [pl.BlockSpec((tm2, tn2), lambda i, j: (i, j)),
                           pl.BlockSpec((tm2, tn2), lambda i, j: (i, nga + j))],
                          jax.ShapeDtypeStruct((T, D), BF16), pl.BlockSpec((tm2, tn2), lambda i, j: (i, j)),
                          tm2, tn2, 52, "branch_merge")
    tno = _tile(512, D)
    r = fused_matmul([merged], [p["w_out"].astype(BF16)], [(0, 0)], functools.partial(_residual_epilogue, 1.0),
                     [x, mod],
                     [pl.BlockSpec((tm, tno), lambda i, j: (i, j)),
                      pl.BlockSpec((None, 1, tno), lambda i, j: ((i // tpb) * N_ADA + 5, 0, j))],
                     jax.ShapeDtypeStruct((T, D), F32), pl.BlockSpec((tm, tno), lambda i, j: (i, j)),
                     tm, tno, 52, "out_proj")
    x, h = ln_mod(r, p["ln2_g"], p["ln2_b"], mod, S, 6, 7)

    r = ffn(h, x, p["ffn2_w_gate"], p["ffn2_w_up"], p["ffn2_w_down"], 8)
    return layer_norm(r, p["ln3_g"], p["ln3_b"])


def kernel(x, c, positions, w_ada, b_ada, ffn1_w_gate, ffn1_w_up, ffn1_w_down, w_in, nsa_cmp_pos, nsa_cmp_k_w1, nsa_cmp_k_w2, nsa_cmp_v_w1, nsa_cmp_v_w2, ssd_conv_w, ssd_conv_b, ssd_dt_bias, ssd_a_log, ssd_d, ssd_norm_w, w_branch_a, w_branch_b, w_out, ffn2_w_gate, ffn2_w_up, ffn2_w_down, ln1_g, ln1_b, ln2_g, ln2_b, ln3_g, ln3_b):
    B, S, D = x.shape
    dh = NSA_HEAD_DIM
    per_layer = dict(ffn1_w_gate=ffn1_w_gate, ffn1_w_up=ffn1_w_up, ffn1_w_down=ffn1_w_down, w_in=w_in,
                     nsa_cmp_pos=nsa_cmp_pos, nsa_cmp_k_w1=nsa_cmp_k_w1, nsa_cmp_k_w2=nsa_cmp_k_w2,
                     nsa_cmp_v_w1=nsa_cmp_v_w1, nsa_cmp_v_w2=nsa_cmp_v_w2, ssd_conv_w=ssd_conv_w,
                     ssd_conv_b=ssd_conv_b, ssd_dt_bias=ssd_dt_bias, ssd_a_log=ssd_a_log, ssd_d=ssd_d,
                     ssd_norm_w=ssd_norm_w, w_branch_a=w_branch_a, w_branch_b=w_branch_b, w_out=w_out,
                     ffn2_w_gate=ffn2_w_gate, ffn2_w_up=ffn2_w_up, ffn2_w_down=ffn2_w_down,
                     ln1_g=ln1_g, ln1_b=ln1_b, ln2_g=ln2_g, ln2_b=ln2_b, ln3_g=ln3_g, ln3_b=ln3_b)

    tabs = [t.reshape(B * S, LANES) for t in _rope_lane_tables(positions)]
    n_seg = S // NSA_CMP_STRIDE
    c_end = jnp.minimum(NSA_CMP_STRIDE * jnp.arange(n_seg) + NSA_CMP_BLOCK - 1, S - 1)
    ck, s1k, s2k = _rope_lane_tables(positions[:, c_end])
    tabs_c = [jnp.stack([ck, jnp.ones_like(ck)]), jnp.stack([s1k, jnp.zeros_like(s1k)]),
              jnp.stack([s2k, jnp.zeros_like(s2k)])]

    key_blk = np.arange(S) // NSA_SEL_BLOCK
    e_blk = jnp.asarray(key_blk[:, None] == np.arange(LANES)[None, :], BF16)
    ones_col = jnp.asarray(np.arange(2 * SUBLANES)[None, :] == 0, BF16) * jnp.ones((S, 1), BF16)
    n_cmp = (S - NSA_CMP_BLOCK) // NSA_CMP_STRIDE + 1
    c_start = NSA_CMP_STRIDE * np.arange(n_seg)
    sel_start = NSA_SEL_BLOCK * np.arange(LANES)
    ov = ((c_start[:, None] < sel_start[None, :] + NSA_SEL_BLOCK)
          & (c_start[:, None] + NSA_CMP_BLOCK - 1 >= sel_start[None, :])
          & (np.arange(n_seg)[:, None] < n_cmp) & (np.arange(LANES)[None, :] < S // NSA_SEL_BLOCK))
    consts = (e_blk, ones_col, jnp.asarray(ov, BF16))

    xt = x.reshape(B * S, D)
    for l in range(DEPTH):
        mod = ada_proj(c, w_ada[l], b_ada[l]).reshape(B * N_ADA, 1, D)
        xt = _layer(xt, mod, positions, tabs, tabs_c, consts, {k: v[l] for k, v in per_layer.items()})
    return xt.reshape(B, S, D)
```

```python
import functools
import math

import jax
import jax.numpy as jnp
import numpy as np
from jax import lax
from jax.experimental import pallas as pl
from jax.experimental.pallas import tpu as pltpu

D_MODEL = 4096
BATCH = 2
SEQ = 8192
DEPTH = 1

NSA_HEADS = 32
NSA_KV_GROUPS = 4
NSA_HEAD_DIM = 128
NSA_CMP_BLOCK = 32
NSA_CMP_STRIDE = 16
NSA_CMP_HIDDEN = 256
NSA_SEL_BLOCK = 64
NSA_SEL_TOPN = 16
NSA_WINDOW = 512
ROPE_THETA = 500000.0
ROPE_DIM = NSA_HEAD_DIM // 4
FORCE_BONUS = 1.0e4
NEG_INF = -1.0e30

SSD_D_INNER = D_MODEL
SSD_HEAD_DIM = 64
SSD_HEADS = SSD_D_INNER // SSD_HEAD_DIM
SSD_GROUPS = 8
SSD_STATE = 128
SSD_CONV = 4
SSD_CHUNK = 256
SSD_CONV_CH = SSD_D_INNER + 2 * SSD_GROUPS * SSD_STATE

FFN_DIM = 256 * ((8 * D_MODEL // 3 + 255) // 256)
N_ADA = 9
LN_EPS = 1e-5
RMS_EPS = 1e-5
DEEPNORM_ALPHA = (2 * DEPTH) ** 0.25

NSA_Q_WIDTH = NSA_HEADS * NSA_HEAD_DIM
NSA_KV_WIDTH = NSA_KV_GROUPS * NSA_HEAD_DIM
IN_SPLIT_SIZES = (NSA_Q_WIDTH,) + (NSA_KV_WIDTH,) * 6 + (3 * NSA_HEADS, SSD_D_INNER, SSD_CONV_CH, SSD_HEADS, D_MODEL, D_MODEL)

F32 = jnp.float32
BF16 = jnp.bfloat16
LANES = 128
SUBLANES = 8
LOG2E = 1.4426950408889634
MIB = 1024 * 1024

TM_MATMUL = 1024
TM_DUAL = 512
TM_ROWWISE = 256
ATTN_TQ = 128
ATTN_TK = 512
TM_FFN_UP = 2048
FFN_GU = 256


def _tile(pref, dim):
    t = min(pref, dim)
    assert dim % t == 0, (pref, dim)
    return t


def _params(sem, vmem_mib):
    return pltpu.CompilerParams(dimension_semantics=sem, vmem_limit_bytes=vmem_mib * MIB)


def _sigmoid(x):
    return 1.0 / (1.0 + jnp.exp(-x))


def _silu(x):
    return x * _sigmoid(x)


def _nt_dot(a, b):
    return lax.dot_general(a, b, (((1,), (1,)), ((), ())), preferred_element_type=F32)


def _ada_body(c_ref, w_ref, b_ref, o_ref):
    c = c_ref[...]
    ca = _silu(c).astype(BF16)
    o_ref[...] = jnp.dot(ca, w_ref[...].astype(BF16), preferred_element_type=F32) + b_ref[...]


def ada_proj(c, w_ada, b_ada):
    B, D = c.shape
    N = w_ada.shape[1]
    tn = _tile(512, N)
    c8 = jnp.zeros((SUBLANES, D), F32).at[:B].set(c)
    out = pl.pallas_call(
        _ada_body,
        grid=(N // tn,),
        in_specs=[pl.BlockSpec((SUBLANES, D), lambda j: (0, 0)),
                  pl.BlockSpec((D, tn), lambda j: (0, j)),
                  pl.BlockSpec((1, tn), lambda j: (0, j))],
        out_specs=pl.BlockSpec((SUBLANES, tn), lambda j: (0, j)),
        out_shape=jax.ShapeDtypeStruct((SUBLANES, N), F32),
        compiler_params=_params(("arbitrary",), 40),
        name="ada_proj",
    )(c8, w_ada, b_ada.reshape(1, N))
    return out[:B]


def _mod_body(x_ref, shift_ref, scale_ref, h_ref):
    h_ref[...] = (x_ref[...] * (1.0 + scale_ref[...]) + shift_ref[...]).astype(h_ref.dtype)


def _ln_mod_body(r_ref, g_ref, b_ref, shift_ref, scale_ref, x_ref, h_ref):
    r = r_ref[...]
    mu = jnp.mean(r, axis=-1, keepdims=True)
    d = r - mu
    var = jnp.mean(d * d, axis=-1, keepdims=True)
    x = d * lax.rsqrt(var + LN_EPS) * g_ref[...] + b_ref[...]
    x_ref[...] = x
    h_ref[...] = (x * (1.0 + scale_ref[...]) + shift_ref[...]).astype(h_ref.dtype)


def _ln_body(r_ref, g_ref, b_ref, x_ref):
    r = r_ref[...]
    mu = jnp.mean(r, axis=-1, keepdims=True)
    d = r - mu
    var = jnp.mean(d * d, axis=-1, keepdims=True)
    x_ref[...] = d * lax.rsqrt(var + LN_EPS) * g_ref[...] + b_ref[...]


def _mod_spec(D, tiles_per_batch, slot):
    return pl.BlockSpec((None, 1, D), lambda i: ((i // tiles_per_batch) * N_ADA + slot, 0, 0))


def modulate(x, mod, seq, slot_shift, slot_scale):
    T, D = x.shape
    tm = _tile(TM_ROWWISE, seq)
    tpb = seq // tm
    return pl.pallas_call(
        _mod_body,
        grid=(T // tm,),
        in_specs=[pl.BlockSpec((tm, D), lambda i: (i, 0)),
                  _mod_spec(D, tpb, slot_shift), _mod_spec(D, tpb, slot_scale)],
        out_specs=pl.BlockSpec((tm, D), lambda i: (i, 0)),
        out_shape=jax.ShapeDtypeStruct((T, D), BF16),
        compiler_params=_params(("parallel",), 32),
        name="modulate",
    )(x, mod, mod)


def ln_mod(r, g, b, mod, seq, slot_shift, slot_scale):
    T, D = r.shape
    tm = _tile(TM_ROWWISE, seq)
    tpb = seq // tm
    row = pl.BlockSpec((tm, D), lambda i: (i, 0))
    vec = pl.BlockSpec((1, D), lambda i: (0, 0))
    return pl.pallas_call(
        _ln_mod_body,
        grid=(T // tm,),
        in_specs=[row, vec, vec, _mod_spec(D, tpb, slot_shift), _mod_spec(D, tpb, slot_scale)],
        out_specs=[row, row],
        out_shape=[jax.ShapeDtypeStruct((T, D), F32), jax.ShapeDtypeStruct((T, D), BF16)],
        compiler_params=_params(("parallel",), 40),
        name="ln_mod",
    )(r, g.reshape(1, D), b.reshape(1, D), mod, mod)


def layer_norm(r, g, b):
    T, D = r.shape
    tm = _tile(TM_ROWWISE, T)
    row = pl.BlockSpec((tm, D), lambda i: (i, 0))
    vec = pl.BlockSpec((1, D), lambda i: (0, 0))
    return pl.pallas_call(
        _ln_body,
        grid=(T // tm,),
        in_specs=[row, vec, vec],
        out_specs=row,
        out_shape=jax.ShapeDtypeStruct((T, D), F32),
        compiler_params=_params(("parallel",), 32),
        name="layer_norm",
    )(r, g.reshape(1, D), b.reshape(1, D))


def _fused_matmul_body(nx, nw, ne, pairs, epilogue, *refs):
    x_refs = refs[:nx]
    w_refs = refs[nx:nx + nw]
    e_refs = refs[nx + nw:nx + nw + ne]
    o_refs = refs[nx + nw + ne:]
    accs = [jnp.dot(x_refs[a][...], w_refs[b][...], preferred_element_type=F32) for a, b in pairs]
    epilogue(accs, e_refs, o_refs)


def fused_matmul(xs, ws, pairs, epilogue, extras, extra_specs, out_shapes, out_specs, tm, tn, vmem_mib, name,
                 w_cols=None):
    M = xs[0].shape[0]
    col0, N = w_cols if w_cols is not None else (0, ws[0].shape[1])
    assert col0 % tn == 0 and N % tn == 0 and M % tm == 0
    j0 = col0 // tn
    in_specs = ([pl.BlockSpec((tm, x.shape[1]), lambda i, j: (i, 0)) for x in xs]
                + [pl.BlockSpec((w.shape[0], tn), lambda i, j: (0, j0 + j)) for w in ws]
                + list(extra_specs))
    body = functools.partial(_fused_matmul_body, len(xs), len(ws), len(extras), tuple(pairs), epilogue)
    return pl.pallas_call(
        body,
        grid=(M // tm, N // tn),
        in_specs=in_specs,
        out_specs=out_specs,
        out_shape=out_shapes,
        compiler_params=_params(("parallel", "arbitrary"), vmem_mib),
        name=name,
    )(*xs, *ws, *extras)


def _swiglu_epilogue(accs, e_refs, o_refs):
    g, u = accs
    o_refs[0][...] = (_silu(g) * u).astype(o_refs[0].dtype)


def ffn_up(h, wg, wu):
    T, D = h.shape
    F = wg.shape[1]
    tm = _tile(TM_FFN_UP, T)
    return fused_matmul([h], [wg, wu], [(0, 0), (0, 1)], _swiglu_epilogue, [], [],
                        jax.ShapeDtypeStruct((T, F), BF16), pl.BlockSpec((tm, FFN_GU), lambda i, j: (i, j)),
                        tm, FFN_GU, 58, "ffn_up")


def _residual_epilogue(coef, accs, e_refs, o_refs):
    x_ref, gate_ref = e_refs
    o_refs[0][...] = DEEPNORM_ALPHA * x_ref[...] + (coef * gate_ref[...]) * accs[0]


def ffn_down(a, wd, x, mod, seq, slot_gate, coef):
    T, D = x.shape
    tm, tn = _tile(TM_DUAL, seq), _tile(512, D)
    tpb = seq // tm
    return fused_matmul([a], [wd], [(0, 0)], functools.partial(_residual_epilogue, coef), [x, mod],
                        [pl.BlockSpec((tm, tn), lambda i, j: (i, j)),
                         pl.BlockSpec((None, 1, tn), lambda i, j: ((i // tpb) * N_ADA + slot_gate, 0, j))],
                        jax.ShapeDtypeStruct((T, D), F32), pl.BlockSpec((tm, tn), lambda i, j: (i, j)),
                        tm, tn, 58, "ffn_down")


def _rope(t, c, s1, s2):
    half = ROPE_DIM // 2
    return t * c + pltpu.roll(t, LANES - half, axis=1) * s1 + pltpu.roll(t, half, axis=1) * s2


def _q_epilogue(n_heads, accs, e_refs, o_refs):
    c, s1, s2 = (r[...] for r in e_refs)
    qscale = NSA_HEAD_DIM ** -0.5 * LOG2E
    for h in range(n_heads):
        t = accs[0][:, h * LANES:(h + 1) * LANES]
        o_refs[0][h] = (_rope(t, c, s1, s2) * qscale).astype(o_refs[0].dtype)


def _kv_plain_epilogue(n_groups, accs, e_refs, o_refs):
    for g in range(n_groups):
        o_refs[0][g] = accs[0][:, g * LANES:(g + 1) * LANES].astype(o_refs[0].dtype)


def _kv_rope_epilogue(n_groups, accs, e_refs, o_refs):
    c, s1, s2 = (r[...] for r in e_refs)
    for g in range(n_groups):
        t = accs[0][:, g * LANES:(g + 1) * LANES]
        o_refs[0][g] = _rope(t, c, s1, s2).astype(o_refs[0].dtype)


def _plain_epilogue(accs, e_refs, o_refs):
    o_refs[0][...] = accs[0].astype(o_refs[0].dtype)


def _sigmoid_epilogue(accs, e_refs, o_refs):
    o_refs[0][...] = _sigmoid(accs[0]).astype(o_refs[0].dtype)


def _small_epilogue(accs, e_refs, o_refs):
    a = accs[0]
    lane = lax.broadcasted_iota(jnp.int32, a.shape, 1)
    o_refs[0][...] = jnp.where(lane >= LANES, _sigmoid(a), a)


def _compress_body(seg_ref, w1_ref, w2_ref, pos_ref, c_ref, s1_ref, s2_ref, o_ref):
    n_seg = seg_ref.shape[0]
    half_k = seg_ref.shape[1]
    seg = seg_ref[...].astype(BF16)
    top = jnp.dot(seg, w1_ref[0:half_k, :], preferred_element_type=F32)
    bot = jnp.dot(seg, w1_ref[half_k:2 * half_k, :], preferred_element_type=F32)
    cb = jnp.dot(pos_ref[...], w1_ref[...], preferred_element_type=F32)[0:1, :]
    hid = _silu(top + pltpu.roll(bot, n_seg - 1, axis=0) + cb).astype(BF16)
    out = jnp.dot(hid, w2_ref[...], preferred_element_type=F32)
    out = _rope(out, c_ref[...], s1_ref[...], s2_ref[...])
    rowi = lax.broadcasted_iota(jnp.int32, out.shape, 0)
    o_ref[...] = jnp.where(rowi < n_seg - 1, out, 0.0).astype(o_ref.dtype)


def nsa_compress(kvseg, w1, w2, pos8, tabs):
    _, B, G, n_seg, K2 = kvseg.shape
    dh = NSA_HEAD_DIM
    hid = w1.shape[-1]
    tab_spec = pl.BlockSpec((None, None, n_seg, dh), lambda kv, b, g: (kv, b, 0, 0))
    return pl.pallas_call(
        _compress_body,
        grid=(2, B, G),
        in_specs=[pl.BlockSpec((None, None, None, n_seg, K2), lambda kv, b, g: (kv, b, g, 0, 0)),
                  pl.BlockSpec((None, 2 * K2, hid), lambda kv, b, g: (kv, 0, 0)),
                  pl.BlockSpec((None, hid, dh), lambda kv, b, g: (kv, 0, 0)),
                  pl.BlockSpec((SUBLANES, 2 * K2), lambda kv, b, g: (0, 0)),
                  tab_spec, tab_spec, tab_spec],
        out_specs=pl.BlockSpec((None, None, None, n_seg, dh), lambda kv, b, g: (kv, b, g, 0, 0)),
        out_shape=jax.ShapeDtypeStruct((2, B, G, n_seg, dh), BF16),
        compiler_params=_params(("arbitrary", "arbitrary", "arbitrary"), 40),
        name="nsa_compress",
    )(kvseg, w1, w2, pos8, *tabs)


def _attn_body(q_ref, kc_ref, vcxt_ref, kx_ref, vxt_ref, kw_ref, vwxt_ref, vxd_ref, gate_ref, o_ref,
               qxt_ref, s0_ref, s1_ref, mt0_ref, mt1_ref, m_ref, acc_ref, out_ref, *, tq, tk, wk, hpg, top_n):
    dh = NSA_HEAD_DIM
    R = hpg * tq
    hw = R // 2
    n_cp = kc_ref.shape[0]
    t0 = pl.program_id(2) * tq
    t_q = t0 + lax.broadcasted_iota(jnp.int32, (1, tq), 1)
    for h in range(hpg):
        qxt_ref[0:dh, h * tq:(h + 1) * tq] = q_ref[h].astype(F32).T.astype(BF16)
    qt = qxt_ref[0:dh, :]
    gates = gate_ref[...]

    def gate_row(branch):
        return jnp.concatenate([gates[branch * hpg + h:branch * hpg + h + 1, :] for h in range(hpg)], axis=1)

    def all_heads(x):
        return jnp.concatenate([x] * hpg, axis=1)

    c_end = lax.broadcasted_iota(jnp.int32, (n_cp, 1), 0) * NSA_CMP_STRIDE + (NSA_CMP_BLOCK - 1)
    s_c = jnp.dot(kc_ref[...], qt, preferred_element_type=F32) + all_heads(jnp.where(c_end <= t_q, 0.0, NEG_INF))
    mx_c = jnp.max(s_c, axis=0, keepdims=True)
    vcxt = vcxt_ref[...]
    g_c = gate_row(0)
    imp_t = jnp.zeros((LANES, tq), F32)
    for half in range(2):
        cols = slice(half * hw, (half + 1) * hw)
        e = jnp.exp2(s_c[:, cols] - mx_c[:, cols])
        inv = jnp.where(mx_c[:, cols] > 0.5 * NEG_INF, 1.0 / jnp.sum(e, axis=0, keepdims=True), 0.0)
        r = jnp.dot(vcxt, e.astype(BF16), preferred_element_type=F32)
        out_ref[:, cols] = r[0:dh, :] * (g_c[:, cols] * inv)
        ri = r[dh:2 * dh, :] * inv
        for h in range(hpg // 2):
            imp_t = imp_t + ri[:, h * tq:(h + 1) * tq]

    w0 = pl.multiple_of(jnp.maximum(t0 + tq - wk, 0), tq)
    c0 = w0 // tq
    kpos_w = w0 + lax.broadcasted_iota(jnp.int32, (wk, 1), 0)
    w_bias = jnp.where((kpos_w <= t_q) & (kpos_w > t_q - NSA_WINDOW), 0.0, NEG_INF)
    s_w = jnp.dot(kw_ref[pl.ds(w0, wk), :], qt, preferred_element_type=F32) + all_heads(w_bias)
    mx_w = jnp.max(s_w, axis=0, keepdims=True)

    j_i = lax.broadcasted_iota(jnp.int32, (LANES, 1), 0)
    j_f = j_i.astype(F32)
    cur = t_q // NSA_SEL_BLOCK
    causal_blk = j_i * NSA_SEL_BLOCK <= t_q
    forced = (j_i == 0) | (j_i == cur) | (j_i == cur - 1)
    work = jnp.where(causal_blk & jnp.logical_not(forced), imp_t, NEG_INF)
    sel = forced
    for _ in range(top_n - 3):
        mx = jnp.max(work, axis=0, keepdims=True)
        first = jnp.min(jnp.where(work == mx, j_f, float(LANES)), axis=0, keepdims=True)
        pick = j_f == first
        sel = sel | pick
        work = jnp.where(pick, -jnp.inf, work)
    bias_t = jnp.where(sel & (j_i * NSA_SEL_BLOCK < t0), 0.0, NEG_INF).astype(BF16)
    for h in range(hpg):
        qxt_ref[dh:2 * dh, h * tq:(h + 1) * tq] = bias_t

    s_refs = (s0_ref, s1_ref)
    mt_refs = (mt0_ref, mt1_ref)

    def score_tile(kt, slot):
        k0 = pl.multiple_of(kt * tk, tk)
        s = jnp.dot(kx_ref[pl.ds(k0, tk), :], qxt_ref[...], preferred_element_type=F32)
        s_refs[slot][...] = s
        mt_refs[slot][...] = jnp.max(s, axis=0, keepdims=True)

    def softmax_pv(kt, slot):
        vt = vxt_ref[kt]
        for half in range(2):
            cols = slice(half * hw, (half + 1) * hw)
            m_old = m_ref[:, cols]
            m_new = jnp.maximum(m_old, mt_refs[slot][:, cols])
            p = jnp.exp2(s_refs[slot][:, cols] - m_new).astype(BF16)
            pv = jnp.dot(vt, p, preferred_element_type=F32)
            acc_ref[:, cols] = jnp.exp2(m_old - m_new) * acc_ref[:, cols] + pv
            m_ref[:, cols] = m_new

    score_tile(0, 0)

    vwt = jnp.concatenate([vwxt_ref[c0 + i] for i in range(wk // tq)], axis=1)
    g_w = gate_row(2)
    for half in range(2):
        cols = slice(half * hw, (half + 1) * hw)
        e = jnp.exp2(s_w[:, cols] - mx_w[:, cols]).astype(BF16)
        r = jnp.dot(vwt, e, preferred_element_type=F32)
        inv = jnp.where(mx_w[:, cols] > 0.5 * NEG_INF, 1.0 / r[dh:dh + 1, :], 0.0)
        out_ref[:, cols] += r[0:dh, :] * (g_w[:, cols] * inv)

    m_ref[...] = jnp.full(m_ref.shape, NEG_INF, F32)
    acc_ref[...] = jnp.zeros(acc_ref.shape, F32)
    n_full = jnp.maximum((t0 + tk - 1) // tk, 1) - 1

    def two_tiles(i, carry):
        score_tile(2 * i + 1, 1)
        softmax_pv(2 * i, 0)
        score_tile(2 * i + 2, 0)
        softmax_pv(2 * i + 1, 1)
        return carry

    lax.fori_loop(0, n_full // 2, two_tiles, 0)

    @pl.when(n_full % 2 == 0)
    def _():
        softmax_pv(n_full, 0)

    @pl.when(n_full % 2 == 1)
    def _():
        score_tile(n_full, 1)
        softmax_pv(n_full - 1, 0)
        softmax_pv(n_full, 1)

    k_i = lax.broadcasted_iota(jnp.int32, (tq, tq), 0)
    q_i = lax.broadcasted_iota(jnp.int32, (tq, tq), 1)
    s_d = (jnp.dot(kx_ref[pl.ds(pl.multiple_of(t0, tq), tq), 0:dh], qt, preferred_element_type=F32)
           + all_heads(jnp.where(k_i <= q_i, 0.0, NEG_INF)))
    m_old = m_ref[...]
    m_fin = jnp.maximum(m_old, jnp.max(s_d, axis=0, keepdims=True))
    pv_d = jnp.dot(vxd_ref[...], jnp.exp2(s_d - m_fin).astype(BF16), preferred_element_type=F32)
    alpha = jnp.exp2(m_old - m_fin)

    for h in range(hpg):
        cols = slice(h * tq, (h + 1) * tq)
        acc = alpha[:, cols] * acc_ref[:, cols] + pv_d[:, cols]
        l_s = acc[dh:dh + 1, :]
        ok = (m_fin[:, cols] > 0.5 * NEG_INF) & (l_s > 0.0)
        o_t = out_ref[:, cols] + acc[0:dh, :] * (gates[hpg + h:hpg + h + 1, :] * jnp.where(ok, 1.0 / l_s, 0.0))
        o_ref[:, h * dh:(h + 1) * dh] = o_t.T.astype(o_ref.dtype)


def nsa_attention(q, kvc, vcxt, kx, vxt, sw, vwxt, vxd, gates_t, seq):
    B, H, S, dh = q.shape
    G = kx.shape[1]
    hpg = H // G
    n_cp = kvc.shape[3]
    tq = _tile(ATTN_TQ, S)
    tk = _tile(ATTN_TK, S)
    wk = min(NSA_WINDOW + tq, S)
    assert tk % tq == 0 and NSA_WINDOW % tq == 0 and tq % NSA_SEL_BLOCK == 0 and S // NSA_SEL_BLOCK <= LANES
    assert hpg % 2 == 0 and wk % tq == 0 and tq == LANES and tq == 2 * NSA_SEL_BLOCK
    top_n = min(NSA_SEL_TOPN, S // NSA_SEL_BLOCK)
    assert top_n > 3 and hpg < FORCE_BONUS
    R = hpg * tq
    nq = S // tq
    vr = vxt.shape[3]
    body = functools.partial(_attn_body, tq=tq, tk=tk, wk=wk, hpg=hpg, top_n=top_n)
    per_group = lambda *blk: pl.BlockSpec((None, None) + blk, lambda b, g, i: (b, g) + (0,) * len(blk))
    return pl.pallas_call(
        body,
        grid=(B, G, nq),
        in_specs=[pl.BlockSpec((None, hpg, tq, dh), lambda b, g, i: (b, g, i, 0)),
                  pl.BlockSpec((None, None, None, n_cp, dh), lambda b, g, i: (0, b, g, 0, 0)),
                  per_group(2 * dh, n_cp),
                  per_group(S, 2 * dh), per_group(S // tk, vr, tk),
                  pl.BlockSpec((None, None, None, S, dh), lambda b, g, i: (2, b, g, 0, 0)),
                  per_group(S // tq, vr, tq),
                  pl.BlockSpec((None, None, None, vr, tq), lambda b, g, i: (b, g, i, 0, 0)),
                  pl.BlockSpec((None, None, 3 * hpg, tq), lambda b, g, i: (b, g, 0, i))],
        out_specs=pl.BlockSpec((tq, hpg * dh), lambda b, g, i: (b * nq + i, g)),
        out_shape=jax.ShapeDtypeStruct((B * S, H * dh), BF16),
        scratch_shapes=[pltpu.VMEM((2 * dh, R), BF16),
                        pltpu.VMEM((tk, R), F32),
                        pltpu.VMEM((tk, R), F32),
                        pltpu.VMEM((1, R), F32),
                        pltpu.VMEM((1, R), F32),
                        pltpu.VMEM((1, R), F32),
                        pltpu.VMEM((vr, R), F32),
                        pltpu.VMEM((dh, R), F32)],
        compiler_params=_params(("parallel", "parallel", "arbitrary"), 56),
        name="nsa_attention",
    )(q, kvc, vcxt, kx, vxt, sw, vwxt, vxd, gates_t)


def _split3(x):
    hi = x.astype(BF16)
    r = x - hi.astype(F32)
    mid = r.astype(BF16)
    lo = (r - mid.astype(F32)).astype(BF16)
    return hi, mid, lo


def _softplus(x):
    return jnp.maximum(x, 0.0) + jnp.log1p(jnp.exp(-jnp.abs(x)))


def _ssd_body(x_ref, xh_ref, b_ref, bh_ref, c_ref, ch_ref, z_ref, dt_ref, dtt_ref,
              wx_ref, wb_ref, wc_ref, cbx_ref, cbb_ref, cbc_ref,
              dtb_ref, dtbt_ref, alog_ref, alogt_ref, dsk_ref, nw_ref, o_ref,
              state_ref, pad_ref, y_ref, acum_ref, dtg_ref, acumt_ref, dtt_s_ref, *, L, hpg, n_groups):
    P, N = SSD_HEAD_DIM, SSD_STATE
    GW = hpg * P
    HALO = SUBLANES
    c = pl.program_id(1)
    g = pl.program_id(2)
    n_heads = hpg * n_groups

    @pl.when((c == 0) & (g == 0))
    def _():
        state_ref[...] = jnp.zeros(state_ref.shape, F32)

    ri = lax.broadcasted_iota(jnp.int32, (L, L), 0)
    ci = lax.broadcasted_iota(jnp.int32, (L, L), 1)
    tril = ri >= ci

    @pl.when(g == 0)
    def _():
        dt = _softplus(dt_ref[:, 0:n_heads] + dtb_ref[...])
        da = dt * (-jnp.exp(alog_ref[...]))
        dtt = _softplus(dtt_ref[...] + dtbt_ref[...])
        dat = dtt * (-jnp.exp(alogt_ref[...]))
        lower = tril.astype(BF16)
        upper = (ri <= ci).astype(BF16)
        acum = sum(jnp.dot(lower, part, preferred_element_type=F32) for part in _split3(da))
        acumt_ref[...] = sum(jnp.dot(part, upper, preferred_element_type=F32) for part in _split3(dat))
        dtt_s_ref[...] = dtt
        for gg in range(n_groups):
            acum_ref[gg, :, 0:hpg] = acum[:, gg * hpg:(gg + 1) * hpg]
            dtg_ref[gg, :, 0:hpg] = dt[:, gg * hpg:(gg + 1) * hpg]

    first = c == 0
    pad_ref[0:HALO, 0:GW] = jnp.where(first, 0.0, xh_ref[...])
    pad_ref[0:HALO, GW:GW + N] = jnp.where(first, 0.0, bh_ref[...])
    pad_ref[0:HALO, GW + N:GW + 2 * N] = jnp.where(first, 0.0, ch_ref[...])
    pad_ref[HALO:HALO + L, 0:GW] = x_ref[...]
    pad_ref[HALO:HALO + L, GW:GW + N] = b_ref[...]
    pad_ref[HALO:HALO + L, GW + N:GW + 2 * N] = c_ref[...]

    def conv(lo, hi, w_ref, bias_ref):
        out = bias_ref[...]
        for k in range(SSD_CONV):
            off = HALO - (SSD_CONV - 1) + k
            out = out + w_ref[k:k + 1, :] * pad_ref[off:off + L, lo:hi]
        return _silu(out)

    xa = conv(0, GW, wx_ref, cbx_ref)
    bg = conv(GW, GW + N, wb_ref, cbb_ref).astype(BF16)
    cg = conv(GW + N, GW + 2 * N, wc_ref, cbc_ref).astype(BF16)
    xat = xa.T
    cb = _nt_dot(cg, bg)
    head_lane = lax.broadcasted_iota(jnp.int32, (1, LANES), 1) < hpg
    acg = jnp.where(head_lane, acum_ref[g], 0.0)
    dtg = jnp.where(head_lane, dtg_ref[g], 0.0)
    h0 = pl.multiple_of(g * hpg, hpg)
    act = acumt_ref[pl.ds(h0, hpg), :]
    dtt = dtt_s_ref[pl.ds(h0, hpg), :]
    dsk = dsk_ref[...]
    expand = (lax.broadcasted_iota(jnp.int32, (LANES, GW), 1) // P
              == lax.broadcasted_iota(jnp.int32, (LANES, GW), 0)).astype(BF16)
    dtx = sum(jnp.dot(part, expand, preferred_element_type=F32) for part in _split3(dtg))
    ecx = sum(jnp.dot(part, expand, preferred_element_type=F32) for part in _split3(jnp.exp(acg)))
    xdt_all = xa * dtx
    first_lane = lax.broadcasted_iota(jnp.int32, (1, 2 * P), 1) < P
    first_row = lax.broadcasted_iota(jnp.int32, (2 * P, 1), 0) < P
    for hp in range(hpg // 2):
        cols = slice(hp * 2 * P, (hp + 1) * 2 * P)
        xdt = xdt_all[:, cols]
        y = jnp.zeros((L, 2 * P), F32)
        w_rows = []
        for k in range(2):
            h = 2 * hp + k
            col = acg[:, h:h + 1]
            row = act[h:h + 1, :]
            decay = jnp.where(tril, jnp.exp(col - row), 0.0)
            own = first_lane if k == 0 else jnp.logical_not(first_lane)
            y = y + jnp.dot((cb * decay).astype(BF16), jnp.where(own, xdt, 0.0).astype(BF16),
                            preferred_element_type=F32)
            a_last = act[h:h + 1, L - 1:L]
            w_rows.append((dtt[h:h + 1, :] * jnp.exp(a_last - row), jnp.exp(a_last)))
        s0 = pl.multiple_of((g * hpg + 2 * hp) * P, 2 * P)
        st = state_ref[pl.ds(s0, 2 * P), :]
        y = y + _nt_dot(cg, st.astype(BF16)) * ecx[:, cols]
        xw_t = (xat[hp * 2 * P:(hp + 1) * 2 * P, :] * jnp.where(first_row, w_rows[0][0], w_rows[1][0])).astype(BF16)
        state_ref[pl.ds(s0, 2 * P), :] = (st * jnp.where(first_row, w_rows[0][1], w_rows[1][1])
                                          + jnp.dot(xw_t, bg, preferred_element_type=F32))
        y_ref[:, cols] = y + dsk[:, cols] * xa[:, cols]

    yg = y_ref[...] * _silu(z_ref[...])
    ms = jnp.mean(yg * yg, axis=-1, keepdims=True)
    o_ref[...] = (yg * lax.rsqrt(ms + RMS_EPS) * nw_ref[...]).astype(o_ref.dtype)


def ssd_mixer(z, xbc, small, dt_t, conv_w, conv_b, dt_bias, a_log, d_skip, norm_w, batch, seq):
    T, Di = z.shape
    Gs, N, P, Hs = SSD_GROUPS, SSD_STATE, SSD_HEAD_DIM, SSD_HEADS
    hpg = Hs // Gs
    GW = hpg * P
    L = math.gcd(seq, SSD_CHUNK)
    nc = seq // L
    HALO = SUBLANES
    assert GW % N == 0 and Hs <= LANES and hpg % 2 == 0 and 2 * P == LANES
    b_off = Di // N
    c_off = b_off + Gs

    def rows(b, c, g):
        return b * nc + c

    def halo_rows(b, c, g):
        return jnp.maximum((b * seq + c * L) // HALO - 1, 0)

    in_specs = [
        pl.BlockSpec((L, GW), lambda b, c, g: (rows(b, c, g), g)),
        pl.BlockSpec((HALO, GW), lambda b, c, g: (halo_rows(b, c, g), g)),
        pl.BlockSpec((L, N), lambda b, c, g: (rows(b, c, g), b_off + g)),
        pl.BlockSpec((HALO, N), lambda b, c, g: (halo_rows(b, c, g), b_off + g)),
        pl.BlockSpec((L, N), lambda b, c, g: (rows(b, c, g), c_off + g)),
        pl.BlockSpec((HALO, N), lambda b, c, g: (halo_rows(b, c, g), c_off + g)),
        pl.BlockSpec((L, GW), lambda b, c, g: (rows(b, c, g), g)),
        pl.BlockSpec((L, LANES), lambda b, c, g: (rows(b, c, g), 0)),
        pl.BlockSpec((None, Hs, L), lambda b, c, g: (b, 0, c)),
        pl.BlockSpec((SSD_CONV, GW), lambda b, c, g: (0, g)),
        pl.BlockSpec((SSD_CONV, N), lambda b, c, g: (0, b_off + g)),
        pl.BlockSpec((SSD_CONV, N), lambda b, c, g: (0, c_off + g)),
        pl.BlockSpec((1, GW), lambda b, c, g: (0, g)),
        pl.BlockSpec((1, N), lambda b, c, g: (0, b_off + g)),
        pl.BlockSpec((1, N), lambda b, c, g: (0, c_off + g)),
        pl.BlockSpec((1, Hs), lambda b, c, g: (0, 0)),
        pl.BlockSpec((Hs, 1), lambda b, c, g: (0, 0)),
        pl.BlockSpec((1, Hs), lambda b, c, g: (0, 0)),
        pl.BlockSpec((Hs, 1), lambda b, c, g: (0, 0)),
        pl.BlockSpec((1, GW), lambda b, c, g: (0, g)),
        pl.BlockSpec((1, GW), lambda b, c, g: (0, g)),
    ]
    cb2 = conv_b.reshape(1, -1)
    body = functools.partial(_ssd_body, L=L, hpg=hpg, n_groups=Gs)
    return pl.pallas_call(
        body,
        grid=(batch, nc, Gs),
        in_specs=in_specs,
        out_specs=pl.BlockSpec((L, GW), lambda b, c, g: (rows(b, c, g), g)),
        out_shape=jax.ShapeDtypeStruct((T, Di), BF16),
        scratch_shapes=[pltpu.VMEM((Hs * P, N), F32),
                        pltpu.VMEM((HALO + L, GW + 2 * N), F32),
                        pltpu.VMEM((L, GW), F32),
                        pltpu.VMEM((Gs, L, LANES), F32),
                        pltpu.VMEM((Gs, L, LANES), F32),
                        pltpu.VMEM((Hs, L), F32),
                        pltpu.VMEM((Hs, L), F32)],
        compiler_params=_params(("arbitrary", "arbitrary", "arbitrary"), 40),
        name="ssd",
    )(xbc, xbc, xbc, xbc, xbc, xbc, z, small, dt_t,
      conv_w, conv_w, conv_w, cb2, cb2, cb2,
      dt_bias.reshape(1, Hs), dt_bias.reshape(Hs, 1), a_log.reshape(1, Hs), a_log.reshape(Hs, 1),
      jnp.repeat(d_skip, P).reshape(1, Di), norm_w.reshape(1, Di))


def _merge_epilogue(accs, e_refs, o_refs):
    ga, gb = e_refs
    o_refs[0][...] = (ga[...].astype(F32) * accs[0] + gb[...].astype(F32) * accs[1]).astype(o_refs[0].dtype)


def _rope_lane_tables(pos):
    half = ROPE_DIM // 2
    inv_freq = jnp.float32(ROPE_THETA) ** (-jnp.arange(half, dtype=F32) / half)
    ang = pos.astype(F32)[..., None] * inv_freq
    cos, sin = jnp.cos(ang), jnp.sin(ang)
    pad = [(0, 0)] * (cos.ndim - 1)
    one = jnp.ones(cos.shape[:-1] + (LANES - ROPE_DIM,), F32)
    c = jnp.concatenate([cos, cos, one], axis=-1)
    s1 = jnp.pad(-sin, pad + [(0, LANES - half)])
    s2 = jnp.pad(sin, pad + [(half, LANES - ROPE_DIM)])
    return c, s1, s2


def _pad_cols(w, n):
    return jnp.pad(w, ((0, 0), (0, n - w.shape[1])))


def _layer(x, mod, positions, tabs, tabs_c, consts, p):
    B, S = positions.shape
    T, D = x.shape
    H, G, dh = NSA_HEADS, NSA_KV_GROUPS, NSA_HEAD_DIM
    hpg = H // G
    tm = _tile(TM_MATMUL, S)
    tpb = S // tm

    def ffn(h, x_res, wg, wu, wd, slot_gate):
        a = ffn_up(h, wg.astype(BF16), wu.astype(BF16))
        return ffn_down(a, wd.astype(BF16), x_res, mod, S, slot_gate, 0.5)

    h = modulate(x, mod, S, 0, 1)
    r = ffn(h, x, p["ffn1_w_gate"], p["ffn1_w_up"], p["ffn1_w_down"], 2)
    x, h = ln_mod(r, p["ln1_g"], p["ln1_b"], mod, S, 3, 4)

    w_in = p["w_in"]
    qkv_w = NSA_Q_WIDTH + 6 * NSA_KV_WIDTH
    w_head = w_in.astype(BF16)
    c_gn, c_z, c_xbc, c_dt, c_ga, c_gb = (qkv_w + int(v) for v in np.cumsum((0,) + IN_SPLIT_SIZES[7:])[:-1])
    w_zx = w_head[:, c_z:c_ga]
    w_gab = w_head[:, c_ga:]
    w_small = jnp.concatenate([_pad_cols(w_head[:, c_dt:c_ga], LANES), _pad_cols(w_head[:, c_gn:c_z], LANES)], axis=1)
    tab_specs = [pl.BlockSpec((tm, LANES), lambda i, j: (i, 0))] * 3

    tnq = hpg * dh
    q = fused_matmul([h], [w_head], [(0, 0)], functools.partial(_q_epilogue, hpg), tabs, tab_specs,
                     jax.ShapeDtypeStruct((B, H, S, dh), BF16),
                     pl.BlockSpec((None, hpg, tm, dh), lambda i, j: (i // tpb, j, i % tpb, 0)),
                     tm, tnq, 52, "in_proj_q", w_cols=(0, NSA_Q_WIDTH))
    tnk = G * dh
    kv_spec = pl.BlockSpec((None, None, G, tm, dh), lambda i, j: (j, i // tpb, 0, i % tpb, 0))
    kvcmp = fused_matmul([h], [w_head], [(0, 0)], functools.partial(_kv_plain_epilogue, G), [], [],
                         jax.ShapeDtypeStruct((2, B, G, S, dh), F32), kv_spec, tm, tnk, 52, "in_proj_kvcmp",
                         w_cols=(NSA_Q_WIDTH, 2 * tnk))
    ident = (jnp.ones_like(tabs[0]), jnp.zeros_like(tabs[0]), jnp.zeros_like(tabs[0]))
    tabs_kv = [jnp.stack([t, e]) for t, e in zip(tabs, ident)]
    tab_kv_specs = [pl.BlockSpec((None, tm, LANES), lambda i, j: (j % 2, i, 0))] * 3
    sw = fused_matmul([h], [w_head], [(0, 0)], functools.partial(_kv_rope_epilogue, G), tabs_kv, tab_kv_specs,
                      jax.ShapeDtypeStruct((4, B, G, S, dh), BF16), kv_spec, tm, tnk, 52, "in_proj_slcwin",
                      w_cols=(NSA_Q_WIDTH + 2 * tnk, 4 * tnk))
    plain_spec = lambda tn: pl.BlockSpec((tm, tn), lambda i, j: (i, j))
    tnt = math.gcd(1024, SSD_D_INNER, SSD_CONV_CH, 2 * D)
    z = fused_matmul([h], [w_zx], [(0, 0)], _plain_epilogue, [], [],
                     jax.ShapeDtypeStruct((T, SSD_D_INNER), F32), plain_spec(tnt), tm, tnt, 52, "in_proj_z",
                     w_cols=(0, SSD_D_INNER))
    xbc = fused_matmul([h], [w_zx], [(0, 0)], _plain_epilogue, [], [],
                       jax.ShapeDtypeStruct((T, SSD_CONV_CH), F32), plain_spec(tnt), tm, tnt, 52, "in_proj_xbc",
                       w_cols=(SSD_D_INNER, SSD_CONV_CH))
    gab = fused_matmul([h], [w_gab], [(0, 0)], _sigmoid_epilogue, [], [],
                       jax.ShapeDtypeStruct((T, 2 * D), BF16), plain_spec(tnt), tm, tnt, 52, "in_proj_gates")
    small = fused_matmul([h], [w_small], [(0, 0)], _small_epilogue, [], [],
                         jax.ShapeDtypeStruct((T, 2 * LANES), F32), plain_spec(2 * LANES), tm, 2 * LANES, 52,
                         "in_proj_small")

    n_seg = S // NSA_CMP_STRIDE
    kvseg = kvcmp.reshape(2, B, G, n_seg, NSA_CMP_STRIDE * dh)
    w1 = jnp.stack([p["nsa_cmp_k_w1"], p["nsa_cmp_v_w1"]]).astype(BF16)
    w2 = jnp.stack([p["nsa_cmp_k_w2"], p["nsa_cmp_v_w2"]]).astype(BF16)
    pos8 = jnp.broadcast_to(p["nsa_cmp_pos"].reshape(1, -1), (SUBLANES, NSA_CMP_BLOCK * dh)).astype(BF16)
    kvc = nsa_compress(kvseg, w1, w2, pos8, tabs_c)
    e_blk, ones_col, overlap = consts
    ones4 = jnp.broadcast_to(ones_col, (B, G, S, ones_col.shape[-1]))
    kx = jnp.concatenate([sw[0], jnp.broadcast_to(e_blk, (B, G, S, dh))], axis=-1)

    def chunked_t(v, chunk):
        return jnp.swapaxes(v.reshape(B, G, S // chunk, chunk, v.shape[-1]), -1, -2)

    vx = jnp.concatenate([sw[1], ones4], axis=-1)
    vxt = chunked_t(vx, _tile(ATTN_TK, S))
    vxd = chunked_t(vx, _tile(ATTN_TQ, S))
    vwxt = chunked_t(jnp.concatenate([sw[3], ones4], axis=-1), _tile(ATTN_TQ, S))
    vcxt = jnp.swapaxes(jnp.concatenate([kvc[1], jnp.broadcast_to(overlap, (B, G, n_seg, LANES))], axis=-1), -1, -2)
    g3 = small[:, LANES:LANES + 3 * H].reshape(B, S, 3, G, hpg)
    gates_t = jnp.transpose(g3, (0, 3, 2, 4, 1)).reshape(B, G, 3 * hpg, S)
    o_a = nsa_attention(q, kvc, vcxt, kx, vxt, sw, vwxt, vxd, gates_t, S)

    dt_t = jnp.transpose(small[:, :SSD_HEADS].reshape(B, S, SSD_HEADS), (0, 2, 1))
    o_b = ssd_mixer(z, xbc, small, dt_t, p["ssd_conv_w"], p["ssd_conv_b"], p["ssd_dt_bias"], p["ssd_a_log"],
                    p["ssd_d"], p["ssd_norm_w"], B, S)

    tm2 = _tile(TM_DUAL, S)
    tn2 = _tile(512, D)
    nga = D // tn2
    merged = fused_matmul([o_a, o_b], [p["w_branch_a"].astype(BF16), p["w_branch_b"].astype(BF16)],
                          [(0, 0), (1, 1)], _merge_epilogue, [gab, gab],
                          [pl.BlockSpec((tm2, tn2), lambda i, j: (i, j)),
                           pl.BlockSpec((tm2, tn2), lambda i, j: (i, nga + j))],
                          jax.ShapeDtypeStruct((T, D), BF16), pl.BlockSpec((tm2, tn2), lambda i, j: (i, j)),
                          tm2, tn2, 52, "branch_merge")
    tno = _tile(512, D)
    r = fused_matmul([merged], [p["w_out"].astype(BF16)], [(0, 0)], functools.partial(_residual_epilogue, 1.0),
                     [x, mod],
                     [pl.BlockSpec((tm, tno), lambda i, j: (i, j)),
                      pl.BlockSpec((None, 1, tno), lambda i, j: ((i // tpb) * N_ADA + 5, 0, j))],
                     jax.ShapeDtypeStruct((T, D), F32), pl.BlockSpec((tm, tno), lambda i, j: (i, j)),
                     tm, tno, 52, "out_proj")
    x, h = ln_mod(r, p["ln2_g"], p["ln2_b"], mod, S, 6, 7)

    r = ffn(h, x, p["ffn2_w_gate"], p["ffn2_w_up"], p["ffn2_w_down"], 8)
    return layer_norm(r, p["ln3_g"], p["ln3_b"])


def kernel(x, c, positions, w_ada, b_ada, ffn1_w_gate, ffn1_w_up, ffn1_w_down, w_in, nsa_cmp_pos, nsa_cmp_k_w1, nsa_cmp_k_w2, nsa_cmp_v_w1, nsa_cmp_v_w2, ssd_conv_w, ssd_conv_b, ssd_dt_bias, ssd_a_log, ssd_d, ssd_norm_w, w_branch_a, w_branch_b, w_out, ffn2_w_gate, ffn2_w_up, ffn2_w_down, ln1_g, ln1_b, ln2_g, ln2_b, ln3_g, ln3_b):
    B, S, D = x.shape
    dh = NSA_HEAD_DIM
    per_layer = dict(ffn1_w_gate=ffn1_w_gate, ffn1_w_up=ffn1_w_up, ffn1_w_down=ffn1_w_down, w_in=w_in,
                     nsa_cmp_pos=nsa_cmp_pos, nsa_cmp_k_w1=nsa_cmp_k_w1, nsa_cmp_k_w2=nsa_cmp_k_w2,
                     nsa_cmp_v_w1=nsa_cmp_v_w1, nsa_cmp_v_w2=nsa_cmp_v_w2, ssd_conv_w=ssd_conv_w,
                     ssd_conv_b=ssd_conv_b, ssd_dt_bias=ssd_dt_bias, ssd_a_log=ssd_a_log, ssd_d=ssd_d,
                     ssd_norm_w=ssd_norm_w, w_branch_a=w_branch_a, w_branch_b=w_branch_b, w_out=w_out,
                     ffn2_w_gate=ffn2_w_gate, ffn2_w_up=ffn2_w_up, ffn2_w_down=ffn2_w_down,
                     ln1_g=ln1_g, ln1_b=ln1_b, ln2_g=ln2_g, ln2_b=ln2_b, ln3_g=ln3_g, ln3_b=ln3_b)

    tabs = [t.reshape(B * S, LANES) for t in _rope_lane_tables(positions)]
    n_seg = S // NSA_CMP_STRIDE
    c_end = jnp.minimum(NSA_CMP_STRIDE * jnp.arange(n_seg) + NSA_CMP_BLOCK - 1, S - 1)
    ck, s1k, s2k = _rope_lane_tables(positions[:, c_end])
    tabs_c = [jnp.stack([ck, jnp.ones_like(ck)]), jnp.stack([s1k, jnp.zeros_like(s1k)]),
              jnp.stack([s2k, jnp.zeros_like(s2k)])]

    key_blk = np.arange(S) // NSA_SEL_BLOCK
    e_blk = jnp.asarray(key_blk[:, None] == np.arange(LANES)[None, :], BF16)
    ones_col = jnp.asarray(np.arange(2 * SUBLANES)[None, :] == 0, BF16) * jnp.ones((S, 1), BF16)
    n_cmp = (S - NSA_CMP_BLOCK) // NSA_CMP_STRIDE + 1
    c_start = NSA_CMP_STRIDE * np.arange(n_seg)
    sel_start = NSA_SEL_BLOCK * np.arange(LANES)
    ov = ((c_start[:, None] < sel_start[None, :] + NSA_SEL_BLOCK)
          & (c_start[:, None] + NSA_CMP_BLOCK - 1 >= sel_start[None, :])
          & (np.arange(n_seg)[:, None] < n_cmp) & (np.arange(LANES)[None, :] < S // NSA_SEL_BLOCK))
    consts = (e_blk, ones_col, jnp.asarray(ov, BF16))

    xt = x.reshape(B * S, D)
    for l in range(DEPTH):
        mod = ada_proj(c, w_ada[l], b_ada[l]).reshape(B * N_ADA, 1, D)
        xt = _layer(xt, mod, positions, tabs, tabs_c, consts, {k: v[l] for k, v in per_layer.items()})
    return xt.reshape(B, S, D)
```

```python
import functools
import math

import jax
import jax.numpy as jnp
import numpy as np
from jax import lax
from jax.experimental import pallas as pl
from jax.experimental.pallas import tpu as pltpu

D_MODEL = 4096
BATCH = 2
SEQ = 8192
DEPTH = 1

NSA_HEADS = 32
NSA_KV_GROUPS = 4
NSA_HEAD_DIM = 128
NSA_CMP_BLOCK = 32
NSA_CMP_STRIDE = 16
NSA_CMP_HIDDEN = 256
NSA_SEL_BLOCK = 64
NSA_SEL_TOPN = 16
NSA_WINDOW = 512
ROPE_THETA = 500000.0
ROPE_DIM = NSA_HEAD_DIM // 4
FORCE_BONUS = 1.0e4
NEG_INF = -1.0e30

SSD_D_INNER = D_MODEL
SSD_HEAD_DIM = 64
SSD_HEADS = SSD_D_INNER // SSD_HEAD_DIM
SSD_GROUPS = 8
SSD_STATE = 128
SSD_CONV = 4
SSD_CHUNK = 256
SSD_CONV_CH = SSD_D_INNER + 2 * SSD_GROUPS * SSD_STATE

FFN_DIM = 256 * ((8 * D_MODEL // 3 + 255) // 256)
N_ADA = 9
LN_EPS = 1e-5
RMS_EPS = 1e-5
DEEPNORM_ALPHA = (2 * DEPTH) ** 0.25

NSA_Q_WIDTH = NSA_HEADS * NSA_HEAD_DIM
NSA_KV_WIDTH = NSA_KV_GROUPS * NSA_HEAD_DIM
IN_SPLIT_SIZES = (NSA_Q_WIDTH,) + (NSA_KV_WIDTH,) * 6 + (3 * NSA_HEADS, SSD_D_INNER, SSD_CONV_CH, SSD_HEADS, D_MODEL, D_MODEL)

F32 = jnp.float32
BF16 = jnp.bfloat16
LANES = 128
SUBLANES = 8
LOG2E = 1.4426950408889634
MIB = 1024 * 1024

TM_MATMUL = 1024
TM_DUAL = 512
TM_ROWWISE = 256
ATTN_TQ = 128
ATTN_TK = 512
FFN_GU = 256


def _tile(pref, dim):
    t = min(pref, dim)
    assert dim % t == 0, (pref, dim)
    return t


def _params(sem, vmem_mib):
    return pltpu.CompilerParams(dimension_semantics=sem, vmem_limit_bytes=vmem_mib * MIB)


def _sigmoid(x):
    return 1.0 / (1.0 + jnp.exp(-x))


def _silu(x):
    return x * _sigmoid(x)


def _nt_dot(a, b):
    return lax.dot_general(a, b, (((1,), (1,)), ((), ())), preferred_element_type=F32)


def _ada_body(c_ref, w_ref, b_ref, o_ref):
    c = c_ref[...]
    ca = _silu(c).astype(BF16)
    o_ref[...] = jnp.dot(ca, w_ref[...].astype(BF16), preferred_element_type=F32) + b_ref[...]


def ada_proj(c, w_ada, b_ada):
    B, D = c.shape
    N = w_ada.shape[1]
    tn = _tile(512, N)
    c8 = jnp.zeros((SUBLANES, D), F32).at[:B].set(c)
    out = pl.pallas_call(
        _ada_body,
        grid=(N // tn,),
        in_specs=[pl.BlockSpec((SUBLANES, D), lambda j: (0, 0)),
                  pl.BlockSpec((D, tn), lambda j: (0, j)),
                  pl.BlockSpec((1, tn), lambda j: (0, j))],
        out_specs=pl.BlockSpec((SUBLANES, tn), lambda j: (0, j)),
        out_shape=jax.ShapeDtypeStruct((SUBLANES, N), F32),
        compiler_params=_params(("arbitrary",), 40),
        name="ada_proj",
    )(c8, w_ada, b_ada.reshape(1, N))
    return out[:B]


def _mod_body(x_ref, shift_ref, scale_ref, h_ref):
    h_ref[...] = (x_ref[...] * (1.0 + scale_ref[...]) + shift_ref[...]).astype(h_ref.dtype)


def _ln_mod_body(r_ref, g_ref, b_ref, shift_ref, scale_ref, x_ref, h_ref):
    r = r_ref[...]
    mu = jnp.mean(r, axis=-1, keepdims=True)
    d = r - mu
    var = jnp.mean(d * d, axis=-1, keepdims=True)
    x = d * lax.rsqrt(var + LN_EPS) * g_ref[...] + b_ref[...]
    x_ref[...] = x
    h_ref[...] = (x * (1.0 + scale_ref[...]) + shift_ref[...]).astype(h_ref.dtype)


def _ln_body(r_ref, g_ref, b_ref, x_ref):
    r = r_ref[...]
    mu = jnp.mean(r, axis=-1, keepdims=True)
    d = r - mu
    var = jnp.mean(d * d, axis=-1, keepdims=True)
    x_ref[...] = d * lax.rsqrt(var + LN_EPS) * g_ref[...] + b_ref[...]


def _mod_spec(D, tiles_per_batch, slot):
    return pl.BlockSpec((None, 1, D), lambda i: ((i // tiles_per_batch) * N_ADA + slot, 0, 0))


def modulate(x, mod, seq, slot_shift, slot_scale):
    T, D = x.shape
    tm = _tile(TM_ROWWISE, seq)
    tpb = seq // tm
    return pl.pallas_call(
        _mod_body,
        grid=(T // tm,),
        in_specs=[pl.BlockSpec((tm, D), lambda i: (i, 0)),
                  _mod_spec(D, tpb, slot_shift), _mod_spec(D, tpb, slot_scale)],
        out_specs=pl.BlockSpec((tm, D), lambda i: (i, 0)),
        out_shape=jax.ShapeDtypeStruct((T, D), BF16),
        compiler_params=_params(("parallel",), 32),
        name="modulate",
    )(x, mod, mod)


def ln_mod(r, g, b, mod, seq, slot_shift, slot_scale):
    T, D = r.shape
    tm = _tile(TM_ROWWISE, seq)
    tpb = seq // tm
    row = pl.BlockSpec((tm, D), lambda i: (i, 0))
    vec = pl.BlockSpec((1, D), lambda i: (0, 0))
    return pl.pallas_call(
        _ln_mod_body,
        grid=(T // tm,),
        in_specs=[row, vec, vec, _mod_spec(D, tpb, slot_shift), _mod_spec(D, tpb, slot_scale)],
        out_specs=[row, row],
        out_shape=[jax.ShapeDtypeStruct((T, D), F32), jax.ShapeDtypeStruct((T, D), BF16)],
        compiler_params=_params(("parallel",), 40),
        name="ln_mod",
    )(r, g.reshape(1, D), b.reshape(1, D), mod, mod)


def layer_norm(r, g, b):
    T, D = r.shape
    tm = _tile(TM_ROWWISE, T)
    row = pl.BlockSpec((tm, D), lambda i: (i, 0))
    vec = pl.BlockSpec((1, D), lambda i: (0, 0))
    return pl.pallas_call(
        _ln_body,
        grid=(T // tm,),
        in_specs=[row, vec, vec],
        out_specs=row,
        out_shape=jax.ShapeDtypeStruct((T, D), F32),
        compiler_params=_params(("parallel",), 32),
        name="layer_norm",
    )(r, g.reshape(1, D), b.reshape(1, D))


def _fused_matmul_body(nx, nw, ne, pairs, epilogue, *refs):
    x_refs = refs[:nx]
    w_refs = refs[nx:nx + nw]
    e_refs = refs[nx + nw:nx + nw + ne]
    o_refs = refs[nx + nw + ne:]
    accs = [jnp.dot(x_refs[a][...], w_refs[b][...].astype(BF16), preferred_element_type=F32) for a, b in pairs]
    epilogue(accs, e_refs, o_refs)


def fused_matmul(xs, ws, pairs, epilogue, extras, extra_specs, out_shapes, out_specs, tm, tn, vmem_mib, name,
                 w_cols=None):
    M = xs[0].shape[0]
    col0, N = w_cols if w_cols is not None else (0, ws[0].shape[1])
    assert col0 % tn == 0 and N % tn == 0 and M % tm == 0
    j0 = col0 // tn
    in_specs = ([pl.BlockSpec((tm, x.shape[1]), lambda i, j: (i, 0)) for x in xs]
                + [pl.BlockSpec((w.shape[0], tn), lambda i, j: (0, j0 + j)) for w in ws]
                + list(extra_specs))
    body = functools.partial(_fused_matmul_body, len(xs), len(ws), len(extras), tuple(pairs), epilogue)
    return pl.pallas_call(
        body,
        grid=(M // tm, N // tn),
        in_specs=in_specs,
        out_specs=out_specs,
        out_shape=out_shapes,
        compiler_params=_params(("parallel", "arbitrary"), vmem_mib),
        name=name,
    )(*xs, *ws, *extras)


def _swiglu_epilogue(accs, e_refs, o_refs):
    g, u = accs
    o_refs[0][...] = (_silu(g) * u).astype(o_refs[0].dtype)


def ffn_up(h, wg, wu):
    T, D = h.shape
    F = wg.shape[1]
    tm = _tile(TM_MATMUL, T)
    return fused_matmul([h], [wg, wu], [(0, 0), (0, 1)], _swiglu_epilogue, [], [],
                        jax.ShapeDtypeStruct((T, F), BF16), pl.BlockSpec((tm, FFN_GU), lambda i, j: (i, j)),
                        tm, FFN_GU, 58, "ffn_up")


def _residual_epilogue(coef, accs, e_refs, o_refs):
    x_ref, gate_ref = e_refs
    o_refs[0][...] = DEEPNORM_ALPHA * x_ref[...] + (coef * gate_ref[...]) * accs[0]


def ffn_down(a, wd, x, mod, seq, slot_gate, coef):
    T, D = x.shape
    tm, tn = _tile(TM_DUAL, seq), _tile(512, D)
    tpb = seq // tm
    return fused_matmul([a], [wd], [(0, 0)], functools.partial(_residual_epilogue, coef), [x, mod],
                        [pl.BlockSpec((tm, tn), lambda i, j: (i, j)),
                         pl.BlockSpec((None, 1, tn), lambda i, j: ((i // tpb) * N_ADA + slot_gate, 0, j))],
                        jax.ShapeDtypeStruct((T, D), F32), pl.BlockSpec((tm, tn), lambda i, j: (i, j)),
                        tm, tn, 58, "ffn_down")


def _rope(t, c, s1, s2):
    half = ROPE_DIM // 2
    return t * c + pltpu.roll(t, LANES - half, axis=1) * s1 + pltpu.roll(t, half, axis=1) * s2


def _q_epilogue(n_heads, accs, e_refs, o_refs):
    c, s1, s2 = (r[...] for r in e_refs)
    qscale = NSA_HEAD_DIM ** -0.5 * LOG2E
    for h in range(n_heads):
        t = accs[0][:, h * LANES:(h + 1) * LANES]
        o_refs[0][h] = (_rope(t, c, s1, s2) * qscale).astype(o_refs[0].dtype)


def _kv_plain_epilogue(n_groups, accs, e_refs, o_refs):
    for g in range(n_groups):
        o_refs[0][g] = accs[0][:, g * LANES:(g + 1) * LANES].astype(o_refs[0].dtype)


def _kv_rope_epilogue(n_groups, accs, e_refs, o_refs):
    c, s1, s2 = (r[...] for r in e_refs)
    for g in range(n_groups):
        t = accs[0][:, g * LANES:(g + 1) * LANES]
        o_refs[0][g] = _rope(t, c, s1, s2).astype(o_refs[0].dtype)


def _plain_epilogue(accs, e_refs, o_refs):
    o_refs[0][...] = accs[0].astype(o_refs[0].dtype)


def _sigmoid_epilogue(accs, e_refs, o_refs):
    o_refs[0][...] = _sigmoid(accs[0]).astype(o_refs[0].dtype)


def _small_epilogue(accs, e_refs, o_refs):
    a = accs[0]
    lane = lax.broadcasted_iota(jnp.int32, a.shape, 1)
    o_refs[0][...] = jnp.where(lane >= LANES, _sigmoid(a), a)


def _compress_body(seg_ref, w1_ref, w2_ref, pos_ref, c_ref, s1_ref, s2_ref, o_ref):
    n_seg = seg_ref.shape[0]
    half_k = seg_ref.shape[1]
    seg = seg_ref[...].astype(BF16)
    top = jnp.dot(seg, w1_ref[0:half_k, :], preferred_element_type=F32)
    bot = jnp.dot(seg, w1_ref[half_k:2 * half_k, :], preferred_element_type=F32)
    cb = jnp.dot(pos_ref[...], w1_ref[...], preferred_element_type=F32)[0:1, :]
    hid = _silu(top + pltpu.roll(bot, n_seg - 1, axis=0) + cb).astype(BF16)
    out = jnp.dot(hid, w2_ref[...], preferred_element_type=F32)
    out = _rope(out, c_ref[...], s1_ref[...], s2_ref[...])
    rowi = lax.broadcasted_iota(jnp.int32, out.shape, 0)
    o_ref[...] = jnp.where(rowi < n_seg - 1, out, 0.0).astype(o_ref.dtype)


def nsa_compress(kvseg, w1, w2, pos8, tabs):
    _, B, G, n_seg, K2 = kvseg.shape
    dh = NSA_HEAD_DIM
    hid = w1.shape[-1]
    tab_spec = pl.BlockSpec((None, None, n_seg, dh), lambda kv, b, g: (kv, b, 0, 0))
    return pl.pallas_call(
        _compress_body,
        grid=(2, B, G),
        in_specs=[pl.BlockSpec((None, None, None, n_seg, K2), lambda kv, b, g: (kv, b, g, 0, 0)),
                  pl.BlockSpec((None, 2 * K2, hid), lambda kv, b, g: (kv, 0, 0)),
                  pl.BlockSpec((None, hid, dh), lambda kv, b, g: (kv, 0, 0)),
                  pl.BlockSpec((SUBLANES, 2 * K2), lambda kv, b, g: (0, 0)),
                  tab_spec, tab_spec, tab_spec],
        out_specs=pl.BlockSpec((None, None, None, n_seg, dh), lambda kv, b, g: (kv, b, g, 0, 0)),
        out_shape=jax.ShapeDtypeStruct((2, B, G, n_seg, dh), BF16),
        compiler_params=_params(("arbitrary", "arbitrary", "arbitrary"), 40),
        name="nsa_compress",
    )(kvseg, w1, w2, pos8, *tabs)


def _attn_body(q_ref, kc_ref, vcxt_ref, kx_ref, vxt_ref, kw_ref, vwxt_ref, vxd_ref, gate_ref, o_ref,
               qxt_ref, s0_ref, s1_ref, mt0_ref, mt1_ref, m_ref, acc_ref, out_ref, *, tq, tk, wk, hpg, top_n):
    dh = NSA_HEAD_DIM
    R = hpg * tq
    hw = R // 2
    n_cp = kc_ref.shape[0]
    t0 = pl.program_id(2) * tq
    t_q = t0 + lax.broadcasted_iota(jnp.int32, (1, tq), 1)
    for h in range(hpg):
        qxt_ref[0:dh, h * tq:(h + 1) * tq] = q_ref[h].astype(F32).T.astype(BF16)
    qt = qxt_ref[0:dh, :]
    gates = gate_ref[...]

    def gate_row(branch):
        return jnp.concatenate([gates[branch * hpg + h:branch * hpg + h + 1, :] for h in range(hpg)], axis=1)

    def all_heads(x):
        return jnp.concatenate([x] * hpg, axis=1)

    c_end = lax.broadcasted_iota(jnp.int32, (n_cp, 1), 0) * NSA_CMP_STRIDE + (NSA_CMP_BLOCK - 1)
    s_c = jnp.dot(kc_ref[...], qt, preferred_element_type=F32) + all_heads(jnp.where(c_end <= t_q, 0.0, NEG_INF))
    mx_c = jnp.max(s_c, axis=0, keepdims=True)
    vcxt = vcxt_ref[...]
    g_c = gate_row(0)
    imp_t = jnp.zeros((LANES, tq), F32)
    for half in range(2):
        cols = slice(half * hw, (half + 1) * hw)
        e = jnp.exp2(s_c[:, cols] - mx_c[:, cols])
        inv = jnp.where(mx_c[:, cols] > 0.5 * NEG_INF, 1.0 / jnp.sum(e, axis=0, keepdims=True), 0.0)
        r = jnp.dot(vcxt, e.astype(BF16), preferred_element_type=F32)
        out_ref[:, cols] = r[0:dh, :] * (g_c[:, cols] * inv)
        ri = r[dh:2 * dh, :] * inv
        for h in range(hpg // 2):
            imp_t = imp_t + ri[:, h * tq:(h + 1) * tq]

    w0 = pl.multiple_of(jnp.maximum(t0 + tq - wk, 0), tq)
    c0 = w0 // tq
    kpos_w = w0 + lax.broadcasted_iota(jnp.int32, (wk, 1), 0)
    w_bias = jnp.where((kpos_w <= t_q) & (kpos_w > t_q - NSA_WINDOW), 0.0, NEG_INF)
    s_w = jnp.dot(kw_ref[pl.ds(w0, wk), :], qt, preferred_element_type=F32) + all_heads(w_bias)
    mx_w = jnp.max(s_w, axis=0, keepdims=True)

    j_i = lax.broadcasted_iota(jnp.int32, (LANES, 1), 0)
    j_f = j_i.astype(F32)
    cur = t_q // NSA_SEL_BLOCK
    causal_blk = j_i * NSA_SEL_BLOCK <= t_q
    forced = (j_i == 0) | (j_i == cur) | (j_i == cur - 1)
    work = jnp.where(causal_blk & jnp.logical_not(forced), imp_t, NEG_INF)
    sel = forced
    for _ in range(top_n - 3):
        mx = jnp.max(work, axis=0, keepdims=True)
        first = jnp.min(jnp.where(work == mx, j_f, float(LANES)), axis=0, keepdims=True)
        pick = j_f == first
        sel = sel | pick
        work = jnp.where(pick, -jnp.inf, work)
    bias_t = jnp.where(sel & (j_i * NSA_SEL_BLOCK < t0), 0.0, NEG_INF).astype(BF16)
    for h in range(hpg):
        qxt_ref[dh:2 * dh, h * tq:(h + 1) * tq] = bias_t

    s_refs = (s0_ref, s1_ref)
    mt_refs = (mt0_ref, mt1_ref)

    def score_tile(kt, slot):
        k0 = pl.multiple_of(kt * tk, tk)
        s = jnp.dot(kx_ref[pl.ds(k0, tk), :], qxt_ref[...], preferred_element_type=F32)
        s_refs[slot][...] = s
        mt_refs[slot][...] = jnp.max(s, axis=0, keepdims=True)

    def softmax_pv(kt, slot):
        vt = vxt_ref[kt]
        for half in range(2):
            cols = slice(half * hw, (half + 1) * hw)
            m_old = m_ref[:, cols]
            m_new = jnp.maximum(m_old, mt_refs[slot][:, cols])
            p = jnp.exp2(s_refs[slot][:, cols] - m_new).astype(BF16)
            pv = jnp.dot(vt, p, preferred_element_type=F32)
            acc_ref[:, cols] = jnp.exp2(m_old - m_new) * acc_ref[:, cols] + pv
            m_ref[:, cols] = m_new

    score_tile(0, 0)

    vwt = jnp.concatenate([vwxt_ref[c0 + i] for i in range(wk // tq)], axis=1)
    g_w = gate_row(2)
    for half in range(2):
        cols = slice(half * hw, (half + 1) * hw)
        e = jnp.exp2(s_w[:, cols] - mx_w[:, cols]).astype(BF16)
        r = jnp.dot(vwt, e, preferred_element_type=F32)
        inv = jnp.where(mx_w[:, cols] > 0.5 * NEG_INF, 1.0 / r[dh:dh + 1, :], 0.0)
        out_ref[:, cols] += r[0:dh, :] * (g_w[:, cols] * inv)

    m_ref[...] = jnp.full(m_ref.shape, NEG_INF, F32)
    acc_ref[...] = jnp.zeros(acc_ref.shape, F32)
    n_full = jnp.maximum((t0 + tk - 1) // tk, 1) - 1

    def two_tiles(i, carry):
        score_tile(2 * i + 1, 1)
        softmax_pv(2 * i, 0)
        score_tile(2 * i + 2, 0)
        softmax_pv(2 * i + 1, 1)
        return carry

    lax.fori_loop(0, n_full // 2, two_tiles, 0)

    @pl.when(n_full % 2 == 0)
    def _():
        softmax_pv(n_full, 0)

    @pl.when(n_full % 2 == 1)
    def _():
        score_tile(n_full, 1)
        softmax_pv(n_full - 1, 0)
        softmax_pv(n_full, 1)

    k_i = lax.broadcasted_iota(jnp.int32, (tq, tq), 0)
    q_i = lax.broadcasted_iota(jnp.int32, (tq, tq), 1)
    s_d = (jnp.dot(kx_ref[pl.ds(pl.multiple_of(t0, tq), tq), 0:dh], qt, preferred_element_type=F32)
           + all_heads(jnp.where(k_i <= q_i, 0.0, NEG_INF)))
    m_old = m_ref[...]
    m_fin = jnp.maximum(m_old, jnp.max(s_d, axis=0, keepdims=True))
    pv_d = jnp.dot(vxd_ref[...], jnp.exp2(s_d - m_fin).astype(BF16), preferred_element_type=F32)
    alpha = jnp.exp2(m_old - m_fin)

    for h in range(hpg):
        cols = slice(h * tq, (h + 1) * tq)
        acc = alpha[:, cols] * acc_ref[:, cols] + pv_d[:, cols]
        l_s = acc[dh:dh + 1, :]
        ok = (m_fin[:, cols] > 0.5 * NEG_INF) & (l_s > 0.0)
        o_t = out_ref[:, cols] + acc[0:dh, :] * (gates[hpg + h:hpg + h + 1, :] * jnp.where(ok, 1.0 / l_s, 0.0))
        o_ref[:, h * dh:(h + 1) * dh] = o_t.T.astype(o_ref.dtype)


def nsa_attention(q, kvc, vcxt, kx, vxt, sw, vwxt, vxd, gates_t, seq):
    B, H, S, dh = q.shape
    G = kx.shape[1]
    hpg = H // G
    n_cp = kvc.shape[3]
    tq = _tile(ATTN_TQ, S)
    tk = _tile(ATTN_TK, S)
    wk = min(NSA_WINDOW + tq, S)
    assert tk % tq == 0 and NSA_WINDOW % tq == 0 and tq % NSA_SEL_BLOCK == 0 and S // NSA_SEL_BLOCK <= LANES
    assert hpg % 2 == 0 and wk % tq == 0 and tq == LANES and tq == 2 * NSA_SEL_BLOCK
    top_n = min(NSA_SEL_TOPN, S // NSA_SEL_BLOCK)
    assert top_n > 3 and hpg < FORCE_BONUS
    R = hpg * tq
    nq = S // tq
    vr = vxt.shape[3]
    body = functools.partial(_attn_body, tq=tq, tk=tk, wk=wk, hpg=hpg, top_n=top_n)
    per_group = lambda *blk: pl.BlockSpec((None, None) + blk, lambda b, g, i: (b, g) + (0,) * len(blk))
    return pl.pallas_call(
        body,
        grid=(B, G, nq),
        in_specs=[pl.BlockSpec((None, hpg, tq, dh), lambda b, g, i: (b, g, i, 0)),
                  pl.BlockSpec((None, None, None, n_cp, dh), lambda b, g, i: (0, b, g, 0, 0)),
                  per_group(2 * dh, n_cp),
                  per_group(S, 2 * dh), per_group(S // tk, vr, tk),
                  pl.BlockSpec((None, None, None, S, dh), lambda b, g, i: (2, b, g, 0, 0)),
                  per_group(S // tq, vr, tq),
                  pl.BlockSpec((None, None, None, vr, tq), lambda b, g, i: (b, g, i, 0, 0)),
                  pl.BlockSpec((None, None, 3 * hpg, tq), lambda b, g, i: (b, g, 0, i))],
        out_specs=pl.BlockSpec((tq, hpg * dh), lambda b, g, i: (b * nq + i, g)),
        out_shape=jax.ShapeDtypeStruct((B * S, H * dh), BF16),
        scratch_shapes=[pltpu.VMEM((2 * dh, R), BF16),
                        pltpu.VMEM((tk, R), F32),
                        pltpu.VMEM((tk, R), F32),
                        pltpu.VMEM((1, R), F32),
                        pltpu.VMEM((1, R), F32),
                        pltpu.VMEM((1, R), F32),
                        pltpu.VMEM((vr, R), F32),
                        pltpu.VMEM((dh, R), F32)],
        compiler_params=_params(("parallel", "parallel", "arbitrary"), 56),
        name="nsa_attention",
    )(q, kvc, vcxt, kx, vxt, sw, vwxt, vxd, gates_t)


def _split3(x):
    hi = x.astype(BF16)
    r = x - hi.astype(F32)
    mid = r.astype(BF16)
    lo = (r - mid.astype(F32)).astype(BF16)
    return hi, mid, lo


def _softplus(x):
    return jnp.maximum(x, 0.0) + jnp.log1p(jnp.exp(-jnp.abs(x)))


def _ssd_body(x_ref, xh_ref, b_ref, bh_ref, c_ref, ch_ref, z_ref, dt_ref, dtt_ref,
              wx_ref, wb_ref, wc_ref, cbx_ref, cbb_ref, cbc_ref,
              dtb_ref, dtbt_ref, alog_ref, alogt_ref, dsk_ref, nw_ref, o_ref,
              state_ref, pad_ref, y_ref, acum_ref, dtg_ref, acumt_ref, dtt_s_ref, *, L, hpg, n_groups):
    P, N = SSD_HEAD_DIM, SSD_STATE
    GW = hpg * P
    HALO = SUBLANES
    c = pl.program_id(1)
    g = pl.program_id(2)
    n_heads = hpg * n_groups

    @pl.when((c == 0) & (g == 0))
    def _():
        state_ref[...] = jnp.zeros(state_ref.shape, F32)

    ri = lax.broadcasted_iota(jnp.int32, (L, L), 0)
    ci = lax.broadcasted_iota(jnp.int32, (L, L), 1)
    tril = ri >= ci

    @pl.when(g == 0)
    def _():
        dt = _softplus(dt_ref[:, 0:n_heads] + dtb_ref[...])
        da = dt * (-jnp.exp(alog_ref[...]))
        dtt = _softplus(dtt_ref[...] + dtbt_ref[...])
        dat = dtt * (-jnp.exp(alogt_ref[...]))
        lower = tril.astype(BF16)
        upper = (ri <= ci).astype(BF16)
        acum = sum(jnp.dot(lower, part, preferred_element_type=F32) for part in _split3(da))
        acumt_ref[...] = sum(jnp.dot(part, upper, preferred_element_type=F32) for part in _split3(dat))
        dtt_s_ref[...] = dtt
        for gg in range(n_groups):
            acum_ref[gg, :, 0:hpg] = acum[:, gg * hpg:(gg + 1) * hpg]
            dtg_ref[gg, :, 0:hpg] = dt[:, gg * hpg:(gg + 1) * hpg]

    first = c == 0
    pad_ref[0:HALO, 0:GW] = jnp.where(first, 0.0, xh_ref[...])
    pad_ref[0:HALO, GW:GW + N] = jnp.where(first, 0.0, bh_ref[...])
    pad_ref[0:HALO, GW + N:GW + 2 * N] = jnp.where(first, 0.0, ch_ref[...])
    pad_ref[HALO:HALO + L, 0:GW] = x_ref[...]
    pad_ref[HALO:HALO + L, GW:GW + N] = b_ref[...]
    pad_ref[HALO:HALO + L, GW + N:GW + 2 * N] = c_ref[...]

    def conv(lo, hi, w_ref, bias_ref):
        out = bias_ref[...]
        for k in range(SSD_CONV):
            off = HALO - (SSD_CONV - 1) + k
            out = out + w_ref[k:k + 1, :] * pad_ref[off:off + L, lo:hi]
        return _silu(out)

    xa = conv(0, GW, wx_ref, cbx_ref)
    bg = conv(GW, GW + N, wb_ref, cbb_ref).astype(BF16)
    cg = conv(GW + N, GW + 2 * N, wc_ref, cbc_ref).astype(BF16)
    xat = xa.T
    cb = _nt_dot(cg, bg)
    head_lane = lax.broadcasted_iota(jnp.int32, (1, LANES), 1) < hpg
    acg = jnp.where(head_lane, acum_ref[g], 0.0)
    dtg = jnp.where(head_lane, dtg_ref[g], 0.0)
    h0 = pl.multiple_of(g * hpg, hpg)
    act = acumt_ref[pl.ds(h0, hpg), :]
    dtt = dtt_s_ref[pl.ds(h0, hpg), :]
    dsk = dsk_ref[...]
    expand = (lax.broadcasted_iota(jnp.int32, (LANES, GW), 1) // P
              == lax.broadcasted_iota(jnp.int32, (LANES, GW), 0)).astype(BF16)
    dtx = sum(jnp.dot(part, expand, preferred_element_type=F32) for part in _split3(dtg))
    ecx = sum(jnp.dot(part, expand, preferred_element_type=F32) for part in _split3(jnp.exp(acg)))
    xdt_all = xa * dtx
    first_lane = lax.broadcasted_iota(jnp.int32, (1, 2 * P), 1) < P
    first_row = lax.broadcasted_iota(jnp.int32, (2 * P, 1), 0) < P
    for hp in range(hpg // 2):
        cols = slice(hp * 2 * P, (hp + 1) * 2 * P)
        xdt = xdt_all[:, cols]
        y = jnp.zeros((L, 2 * P), F32)
        w_rows = []
        for k in range(2):
            h = 2 * hp + k
            col = acg[:, h:h + 1]
            row = act[h:h + 1, :]
            decay = jnp.where(tril, jnp.exp(col - row), 0.0)
            own = first_lane if k == 0 else jnp.logical_not(first_lane)
            y = y + jnp.dot((cb * decay).astype(BF16), jnp.where(own, xdt, 0.0).astype(BF16),
                            preferred_element_type=F32)
            a_last = act[h:h + 1, L - 1:L]
            w_rows.append((dtt[h:h + 1, :] * jnp.exp(a_last - row), jnp.exp(a_last)))
        s0 = pl.multiple_of((g * hpg + 2 * hp) * P, 2 * P)
        st = state_ref[pl.ds(s0, 2 * P), :]
        y = y + _nt_dot(cg, st.astype(BF16)) * ecx[:, cols]
        xw_t = (xat[hp * 2 * P:(hp + 1) * 2 * P, :] * jnp.where(first_row, w_rows[0][0], w_rows[1][0])).astype(BF16)
        state_ref[pl.ds(s0, 2 * P), :] = (st * jnp.where(first_row, w_rows[0][1], w_rows[1][1])
                                          + jnp.dot(xw_t, bg, preferred_element_type=F32))
        y_ref[:, cols] = y + dsk[:, cols] * xa[:, cols]

    yg = y_ref[...] * _silu(z_ref[...])
    ms = jnp.mean(yg * yg, axis=-1, keepdims=True)
    o_ref[...] = (yg * lax.rsqrt(ms + RMS_EPS) * nw_ref[...]).astype(o_ref.dtype)


def ssd_mixer(z, xbc, small, dt_t, conv_w, conv_b, dt_bias, a_log, d_skip, norm_w, batch, seq):
    T, Di = z.shape
    Gs, N, P, Hs = SSD_GROUPS, SSD_STATE, SSD_HEAD_DIM, SSD_HEADS
    hpg = Hs // Gs
    GW = hpg * P
    L = math.gcd(seq, SSD_CHUNK)
    nc = seq // L
    HALO = SUBLANES
    assert GW % N == 0 and Hs <= LANES and hpg % 2 == 0 and 2 * P == LANES
    b_off = Di // N
    c_off = b_off + Gs

    def rows(b, c, g):
        return b * nc + c

    def halo_rows(b, c, g):
        return jnp.maximum((b * seq + c * L) // HALO - 1, 0)

    in_specs = [
        pl.BlockSpec((L, GW), lambda b, c, g: (rows(b, c, g), g)),
        pl.BlockSpec((HALO, GW), lambda b, c, g: (halo_rows(b, c, g), g)),
        pl.BlockSpec((L, N), lambda b, c, g: (rows(b, c, g), b_off + g)),
        pl.BlockSpec((HALO, N), lambda b, c, g: (halo_rows(b, c, g), b_off + g)),
        pl.BlockSpec((L, N), lambda b, c, g: (rows(b, c, g), c_off + g)),
        pl.BlockSpec((HALO, N), lambda b, c, g: (halo_rows(b, c, g), c_off + g)),
        pl.BlockSpec((L, GW), lambda b, c, g: (rows(b, c, g), g)),
        pl.BlockSpec((L, LANES), lambda b, c, g: (rows(b, c, g), 0)),
        pl.BlockSpec((None, Hs, L), lambda b, c, g: (b, 0, c)),
        pl.BlockSpec((SSD_CONV, GW), lambda b, c, g: (0, g)),
        pl.BlockSpec((SSD_CONV, N), lambda b, c, g: (0, b_off + g)),
        pl.BlockSpec((SSD_CONV, N), lambda b, c, g: (0, c_off + g)),
        pl.BlockSpec((1, GW), lambda b, c, g: (0, g)),
        pl.BlockSpec((1, N), lambda b, c, g: (0, b_off + g)),
        pl.BlockSpec((1, N), lambda b, c, g: (0, c_off + g)),
        pl.BlockSpec((1, Hs), lambda b, c, g: (0, 0)),
        pl.BlockSpec((Hs, 1), lambda b, c, g: (0, 0)),
        pl.BlockSpec((1, Hs), lambda b, c, g: (0, 0)),
        pl.BlockSpec((Hs, 1), lambda b, c, g: (0, 0)),
        pl.BlockSpec((1, GW), lambda b, c, g: (0, g)),
        pl.BlockSpec((1, GW), lambda b, c, g: (0, g)),
    ]
    cb2 = conv_b.reshape(1, -1)
    body = functools.partial(_ssd_body, L=L, hpg=hpg, n_groups=Gs)
    return pl.pallas_call(
        body,
        grid=(batch, nc, Gs),
        in_specs=in_specs,
        out_specs=pl.BlockSpec((L, GW), lambda b, c, g: (rows(b, c, g), g)),
        out_shape=jax.ShapeDtypeStruct((T, Di), BF16),
        scratch_shapes=[pltpu.VMEM((Hs * P, N), F32),
                        pltpu.VMEM((HALO + L, GW + 2 * N), F32),
                        pltpu.VMEM((L, GW), F32),
                        pltpu.VMEM((Gs, L, LANES), F32),
                        pltpu.VMEM((Gs, L, LANES), F32),
                        pltpu.VMEM((Hs, L), F32),
                        pltpu.VMEM((Hs, L), F32)],
        compiler_params=_params(("arbitrary", "arbitrary", "arbitrary"), 40),
        name="ssd",
    )(xbc, xbc, xbc, xbc, xbc, xbc, z, small, dt_t,
      conv_w, conv_w, conv_w, cb2, cb2, cb2,
      dt_bias.reshape(1, Hs), dt_bias.reshape(Hs, 1), a_log.reshape(1, Hs), a_log.reshape(Hs, 1),
      jnp.repeat(d_skip, P).reshape(1, Di), norm_w.reshape(1, Di))


def _merge_epilogue(accs, e_refs, o_refs):
    ga, gb = e_refs
    o_refs[0][...] = (ga[...].astype(F32) * accs[0] + gb[...].astype(F32) * accs[1]).astype(o_refs[0].dtype)


def _rope_lane_tables(pos):
    half = ROPE_DIM // 2
    inv_freq = jnp.float32(ROPE_THETA) ** (-jnp.arange(half, dtype=F32) / half)
    ang = pos.astype(F32)[..., None] * inv_freq
    cos, sin = jnp.cos(ang), jnp.sin(ang)
    pad = [(0, 0)] * (cos.ndim - 1)
    one = jnp.ones(cos.shape[:-1] + (LANES - ROPE_DIM,), F32)
    c = jnp.concatenate([cos, cos, one], axis=-1)
    s1 = jnp.pad(-sin, pad + [(0, LANES - half)])
    s2 = jnp.pad(sin, pad + [(half, LANES - ROPE_DIM)])
    return c, s1, s2


def _pad_cols(w, n):
    return jnp.pad(w, ((0, 0), (0, n - w.shape[1])))


def _layer(x, mod, positions, tabs, tabs_c, consts, p):
    B, S = positions.shape
    T, D = x.shape
    H, G, dh = NSA_HEADS, NSA_KV_GROUPS, NSA_HEAD_DIM
    hpg = H // G
    tm = _tile(TM_MATMUL, S)
    tpb = S // tm

    def ffn(h, x_res, wg, wu, wd, slot_gate):
        a = ffn_up(h, wg, wu)
        return ffn_down(a, wd.astype(BF16), x_res, mod, S, slot_gate, 0.5)

    h = modulate(x, mod, S, 0, 1)
    r = ffn(h, x, p["ffn1_w_gate"], p["ffn1_w_up"], p["ffn1_w_down"], 2)
    x, h = ln_mod(r, p["ln1_g"], p["ln1_b"], mod, S, 3, 4)

    w_in = p["w_in"]
    qkv_w = NSA_Q_WIDTH + 6 * NSA_KV_WIDTH
    w_head = w_in.astype(BF16)
    c_gn, c_z, c_xbc, c_dt, c_ga, c_gb = (qkv_w + int(v) for v in np.cumsum((0,) + IN_SPLIT_SIZES[7:])[:-1])
    w_zx = w_head[:, c_z:c_ga]
    w_gab = w_head[:, c_ga:]
    w_small = jnp.concatenate([_pad_cols(w_head[:, c_dt:c_ga], LANES), _pad_cols(w_head[:, c_gn:c_z], LANES)], axis=1)
    tab_specs = [pl.BlockSpec((tm, LANES), lambda i, j: (i, 0))] * 3

    tnq = hpg * dh
    q = fused_matmul([h], [w_head], [(0, 0)], functools.partial(_q_epilogue, hpg), tabs, tab_specs,
                     jax.ShapeDtypeStruct((B, H, S, dh), BF16),
                     pl.BlockSpec((None, hpg, tm, dh), lambda i, j: (i // tpb, j, i % tpb, 0)),
                     tm, tnq, 52, "in_proj_q", w_cols=(0, NSA_Q_WIDTH))
    tnk = G * dh
    kv_spec = pl.BlockSpec((None, None, G, tm, dh), lambda i, j: (j, i // tpb, 0, i % tpb, 0))
    kvcmp = fused_matmul([h], [w_head], [(0, 0)], functools.partial(_kv_plain_epilogue, G), [], [],
                         jax.ShapeDtypeStruct((2, B, G, S, dh), F32), kv_spec, tm, tnk, 52, "in_proj_kvcmp",
                         w_cols=(NSA_Q_WIDTH, 2 * tnk))
    ident = (jnp.ones_like(tabs[0]), jnp.zeros_like(tabs[0]), jnp.zeros_like(tabs[0]))
    tabs_kv = [jnp.stack([t, e]) for t, e in zip(tabs, ident)]
    tab_kv_specs = [pl.BlockSpec((None, tm, LANES), lambda i, j: (j % 2, i, 0))] * 3
    sw = fused_matmul([h], [w_head], [(0, 0)], functools.partial(_kv_rope_epilogue, G), tabs_kv, tab_kv_specs,
                      jax.ShapeDtypeStruct((4, B, G, S, dh), BF16), kv_spec, tm, tnk, 52, "in_proj_slcwin",
                      w_cols=(NSA_Q_WIDTH + 2 * tnk, 4 * tnk))
    plain_spec = lambda tn: pl.BlockSpec((tm, tn), lambda i, j: (i, j))
    tnt = math.gcd(1024, SSD_D_INNER, SSD_CONV_CH, 2 * D)
    z = fused_matmul([h], [w_zx], [(0, 0)], _plain_epilogue, [], [],
                     jax.ShapeDtypeStruct((T, SSD_D_INNER), F32), plain_spec(tnt), tm, tnt, 52, "in_proj_z",
                     w_cols=(0, SSD_D_INNER))
    xbc = fused_matmul([h], [w_zx], [(0, 0)], _plain_epilogue, [], [],
                       jax.ShapeDtypeStruct((T, SSD_CONV_CH), F32), plain_spec(tnt), tm, tnt, 52, "in_proj_xbc",
                       w_cols=(SSD_D_INNER, SSD_CONV_CH))
    gab = fused_matmul([h], [w_gab], [(0, 0)], _sigmoid_epilogue, [], [],
                       jax.ShapeDtypeStruct((T, 2 * D), BF16), plain_spec(tnt), tm, tnt, 52, "in_proj_gates")
    small = fused_matmul([h], [w_small], [(0, 0)], _small_epilogue, [], [],
                         jax.ShapeDtypeStruct((T, 2 * LANES), F32), plain_spec(2 * LANES), tm, 2 * LANES, 52,
                         "in_proj_small")

    n_seg = S // NSA_CMP_STRIDE
    kvseg = kvcmp.reshape(2, B, G, n_seg, NSA_CMP_STRIDE * dh)
    w1 = jnp.stack([p["nsa_cmp_k_w1"], p["nsa_cmp_v_w1"]]).astype(BF16)
    w2 = jnp.stack([p["nsa_cmp_k_w2"], p["nsa_cmp_v_w2"]]).astype(BF16)
    pos8 = jnp.broadcast_to(p["nsa_cmp_pos"].reshape(1, -1), (SUBLANES, NSA_CMP_BLOCK * dh)).astype(BF16)
    kvc = nsa_compress(kvseg, w1, w2, pos8, tabs_c)
    e_blk, ones_col, overlap = consts
    ones4 = jnp.broadcast_to(ones_col, (B, G, S, ones_col.shape[-1]))
    kx = jnp.concatenate([sw[0], jnp.broadcast_to(e_blk, (B, G, S, dh))], axis=-1)

    def chunked_t(v, chunk):
        return jnp.swapaxes(v.reshape(B, G, S // chunk, chunk, v.shape[-1]), -1, -2)

    vx = jnp.concatenate([sw[1], ones4], axis=-1)
    vxt = chunked_t(vx, _tile(ATTN_TK, S))
    vxd = chunked_t(vx, _tile(ATTN_TQ, S))
    vwxt = chunked_t(jnp.concatenate([sw[3], ones4], axis=-1), _tile(ATTN_TQ, S))
    vcxt = jnp.swapaxes(jnp.concatenate([kvc[1], jnp.broadcast_to(overlap, (B, G, n_seg, LANES))], axis=-1), -1, -2)
    g3 = small[:, LANES:LANES + 3 * H].reshape(B, S, 3, G, hpg)
    gates_t = jnp.transpose(g3, (0, 3, 2, 4, 1)).reshape(B, G, 3 * hpg, S)
    o_a = nsa_attention(q, kvc, vcxt, kx, vxt, sw, vwxt, vxd, gates_t, S)

    dt_t = jnp.transpose(small[:, :SSD_HEADS].reshape(B, S, SSD_HEADS), (0, 2, 1))
    o_b = ssd_mixer(z, xbc, small, dt_t, p["ssd_conv_w"], p["ssd_conv_b"], p["ssd_dt_bias"], p["ssd_a_log"],
                    p["ssd_d"], p["ssd_norm_w"], B, S)

    tm2 = _tile(TM_MATMUL, S)
    tn2 = _tile(256, D)
    nga = D // tn2
    merged = fused_matmul([o_a, o_b], [p["w_branch_a"].astype(BF16), p["w_branch_b"].astype(BF16)],
                          [(0, 0), (1, 1)], _merge_epilogue, [gab, gab],
                          [pl.BlockSpec((tm2, tn2), lambda i, j: (i, j)),
                           pl.BlockSpec((tm2, tn2), lambda i, j: (i, nga + j))],
                          jax.ShapeDtypeStruct((T, D), BF16), pl.BlockSpec((tm2, tn2), lambda i, j: (i, j)),
                          tm2, tn2, 52, "branch_merge")
    tno = _tile(512, D)
    r = fused_matmul([merged], [p["w_out"].astype(BF16)], [(0, 0)], functools.partial(_residual_epilogue, 1.0),
                     [x, mod],
                     [pl.BlockSpec((tm, tno), lambda i, j: (i, j)),
                      pl.BlockSpec((None, 1, tno), lambda i, j: ((i // tpb) * N_ADA + 5, 0, j))],
                     jax.ShapeDtypeStruct((T, D), F32), pl.BlockSpec((tm, tno), lambda i, j: (i, j)),
                     tm, tno, 52, "out_proj")
    x, h = ln_mod(r, p["ln2_g"], p["ln2_b"], mod, S, 6, 7)

    r = ffn(h, x, p["ffn2_w_gate"], p["ffn2_w_up"], p["ffn2_w_down"], 8)
    return layer_norm(r, p["ln3_g"], p["ln3_b"])


def kernel(x, c, positions, w_ada, b_ada, ffn1_w_gate, ffn1_w_up, ffn1_w_down, w_in, nsa_cmp_pos, nsa_cmp_k_w1, nsa_cmp_k_w2, nsa_cmp_v_w1, nsa_cmp_v_w2, ssd_conv_w, ssd_conv_b, ssd_dt_bias, ssd_a_log, ssd_d, ssd_norm_w, w_branch_a, w_branch_b, w_out, ffn2_w_gate, ffn2_w_up, ffn2_w_down, ln1_g, ln1_b, ln2_g, ln2_b, ln3_g, ln3_b):
    B, S, D = x.shape
    dh = NSA_HEAD_DIM
    per_layer = dict(ffn1_w_gate=ffn1_w_gate, ffn1_w_up=ffn1_w_up, ffn1_w_down=ffn1_w_down, w_in=w_in,
                     nsa_cmp_pos=nsa_cmp_pos, nsa_cmp_k_w1=nsa_cmp_k_w1, nsa_cmp_k_w2=nsa_cmp_k_w2,
                     nsa_cmp_v_w1=nsa_cmp_v_w1, nsa_cmp_v_w2=nsa_cmp_v_w2, ssd_conv_w=ssd_conv_w,
                     ssd_conv_b=ssd_conv_b, ssd_dt_bias=ssd_dt_bias, ssd_a_log=ssd_a_log, ssd_d=ssd_d,
                     ssd_norm_w=ssd_norm_w, w_branch_a=w_branch_a, w_branch_b=w_branch_b, w_out=w_out,
                     ffn2_w_gate=ffn2_w_gate, ffn2_w_up=ffn2_w_up, ffn2_w_down=ffn2_w_down,
                     ln1_g=ln1_g, ln1_b=ln1_b, ln2_g=ln2_g, ln2_b=ln2_b, ln3_g=ln3_g, ln3_b=ln3_b)

    tabs = [t.reshape(B * S, LANES) for t in _rope_lane_tables(positions)]
    n_seg = S // NSA_CMP_STRIDE
    c_end = jnp.minimum(NSA_CMP_STRIDE * jnp.arange(n_seg) + NSA_CMP_BLOCK - 1, S - 1)
    ck, s1k, s2k = _rope_lane_tables(positions[:, c_end])
    tabs_c = [jnp.stack([ck, jnp.ones_like(ck)]), jnp.stack([s1k, jnp.zeros_like(s1k)]),
              jnp.stack([s2k, jnp.zeros_like(s2k)])]

    key_blk = np.arange(S) // NSA_SEL_BLOCK
    e_blk = jnp.asarray(key_blk[:, None] == np.arange(LANES)[None, :], BF16)
    ones_col = jnp.asarray(np.arange(2 * SUBLANES)[None, :] == 0, BF16) * jnp.ones((S, 1), BF16)
    n_cmp = (S - NSA_CMP_BLOCK) // NSA_CMP_STRIDE + 1
    c_start = NSA_CMP_STRIDE * np.arange(n_seg)
    sel_start = NSA_SEL_BLOCK * np.arange(LANES)
    ov = ((c_start[:, None] < sel_start[None, :] + NSA_SEL_BLOCK)
          & (c_start[:, None] + NSA_CMP_BLOCK - 1 >= sel_start[None, :])
          & (np.arange(n_seg)[:, None] < n_cmp) & (np.arange(LANES)[None, :] < S // NSA_SEL_BLOCK))
    consts = (e_blk, ones_col, jnp.asarray(ov, BF16))

    xt = x.reshape(B * S, D)
    for l in range(DEPTH):
        mod = ada_proj(c, w_ada[l], b_ada[l]).reshape(B * N_ADA, 1, D)
        xt = _layer(xt, mod, positions, tabs, tabs_c, consts, {k: v[l] for k, v in per_layer.items()})
    return xt.reshape(B, S, D)
```

```python
import functools
import math

import jax
import jax.numpy as jnp
import numpy as np
from jax import lax
from jax.experimental import pallas as pl
from jax.experimental.pallas import tpu as pltpu

D_MODEL = 4096
BATCH = 2
SEQ = 8192
DEPTH = 1

NSA_HEADS = 32
NSA_KV_GROUPS = 4
NSA_HEAD_DIM = 128
NSA_CMP_BLOCK = 32
NSA_CMP_STRIDE = 16
NSA_CMP_HIDDEN = 256
NSA_SEL_BLOCK = 64
NSA_SEL_TOPN = 16
NSA_WINDOW = 512
ROPE_THETA = 500000.0
ROPE_DIM = NSA_HEAD_DIM // 4
FORCE_BONUS = 1.0e4
NEG_INF = -1.0e30

SSD_D_INNER = D_MODEL
SSD_HEAD_DIM = 64
SSD_HEADS = SSD_D_INNER // SSD_HEAD_DIM
SSD_GROUPS = 8
SSD_STATE = 128
SSD_CONV = 4
SSD_CHUNK = 256
SSD_CONV_CH = SSD_D_INNER + 2 * SSD_GROUPS * SSD_STATE

FFN_DIM = 256 * ((8 * D_MODEL // 3 + 255) // 256)
N_ADA = 9
LN_EPS = 1e-5
RMS_EPS = 1e-5
DEEPNORM_ALPHA = (2 * DEPTH) ** 0.25

NSA_Q_WIDTH = NSA_HEADS * NSA_HEAD_DIM
NSA_KV_WIDTH = NSA_KV_GROUPS * NSA_HEAD_DIM
IN_SPLIT_SIZES = (NSA_Q_WIDTH,) + (NSA_KV_WIDTH,) * 6 + (3 * NSA_HEADS, SSD_D_INNER, SSD_CONV_CH, SSD_HEADS, D_MODEL, D_MODEL)

F32 = jnp.float32
BF16 = jnp.bfloat16
LANES = 128
SUBLANES = 8
LOG2E = 1.4426950408889634
MIB = 1024 * 1024

TM_MATMUL = 1024
TM_DUAL = 512
TM_ROWWISE = 256
ATTN_TQ = 128
ATTN_TK = 512
FFN_GU = 256


def _tile(pref, dim):
    t = min(pref, dim)
    assert dim % t == 0, (pref, dim)
    return t


def _params(sem, vmem_mib):
    return pltpu.CompilerParams(dimension_semantics=sem, vmem_limit_bytes=vmem_mib * MIB)


def _sigmoid(x):
    return 1.0 / (1.0 + jnp.exp(-x))


def _silu(x):
    return x * _sigmoid(x)


def _nt_dot(a, b):
    return lax.dot_general(a, b, (((1,), (1,)), ((), ())), preferred_element_type=F32)


def _ada_body(c_ref, w_ref, b_ref, o_ref):
    c = c_ref[...]
    ca = _silu(c).astype(BF16)
    o_ref[...] = jnp.dot(ca, w_ref[...].astype(BF16), preferred_element_type=F32) + b_ref[...]


def ada_proj(c, w_ada, b_ada):
    B, D = c.shape
    N = w_ada.shape[1]
    tn = _tile(512, N)
    c8 = jnp.zeros((SUBLANES, D), F32).at[:B].set(c)
    out = pl.pallas_call(
        _ada_body,
        grid=(N // tn,),
        in_specs=[pl.BlockSpec((SUBLANES, D), lambda j: (0, 0)),
                  pl.BlockSpec((D, tn), lambda j: (0, j)),
                  pl.BlockSpec((1, tn), lambda j: (0, j))],
        out_specs=pl.BlockSpec((SUBLANES, tn), lambda j: (0, j)),
        out_shape=jax.ShapeDtypeStruct((SUBLANES, N), F32),
        compiler_params=_params(("arbitrary",), 40),
        name="ada_proj",
    )(c8, w_ada, b_ada.reshape(1, N))
    return out[:B]


def _mod_body(x_ref, shift_ref, scale_ref, h_ref):
    h_ref[...] = (x_ref[...] * (1.0 + scale_ref[...]) + shift_ref[...]).astype(h_ref.dtype)


def _ln_mod_body(r_ref, g_ref, b_ref, shift_ref, scale_ref, x_ref, h_ref):
    r = r_ref[...]
    mu = jnp.mean(r, axis=-1, keepdims=True)
    d = r - mu
    var = jnp.mean(d * d, axis=-1, keepdims=True)
    x = d * lax.rsqrt(var + LN_EPS) * g_ref[...] + b_ref[...]
    x_ref[...] = x
    h_ref[...] = (x * (1.0 + scale_ref[...]) + shift_ref[...]).astype(h_ref.dtype)


def _ln_body(r_ref, g_ref, b_ref, x_ref):
    r = r_ref[...]
    mu = jnp.mean(r, axis=-1, keepdims=True)
    d = r - mu
    var = jnp.mean(d * d, axis=-1, keepdims=True)
    x_ref[...] = d * lax.rsqrt(var + LN_EPS) * g_ref[...] + b_ref[...]


def _mod_spec(D, tiles_per_batch, slot):
    return pl.BlockSpec((None, 1, D), lambda i: ((i // tiles_per_batch) * N_ADA + slot, 0, 0))


def modulate(x, mod, seq, slot_shift, slot_scale):
    T, D = x.shape
    tm = _tile(TM_ROWWISE, seq)
    tpb = seq // tm
    return pl.pallas_call(
        _mod_body,
        grid=(T // tm,),
        in_specs=[pl.BlockSpec((tm, D), lambda i: (i, 0)),
                  _mod_spec(D, tpb, slot_shift), _mod_spec(D, tpb, slot_scale)],
        out_specs=pl.BlockSpec((tm, D), lambda i: (i, 0)),
        out_shape=jax.ShapeDtypeStruct((T, D), BF16),
        compiler_params=_params(("parallel",), 32),
        name="modulate",
    )(x, mod, mod)


def ln_mod(r, g, b, mod, seq, slot_shift, slot_scale):
    T, D = r.shape
    tm = _tile(TM_ROWWISE, seq)
    tpb = seq // tm
    row = pl.BlockSpec((tm, D), lambda i: (i, 0))
    vec = pl.BlockSpec((1, D), lambda i: (0, 0))
    return pl.pallas_call(
        _ln_mod_body,
        grid=(T // tm,),
        in_specs=[row, vec, vec, _mod_spec(D, tpb, slot_shift), _mod_spec(D, tpb, slot_scale)],
        out_specs=[row, row],
        out_shape=[jax.ShapeDtypeStruct((T, D), F32), jax.ShapeDtypeStruct((T, D), BF16)],
        compiler_params=_params(("parallel",), 40),
        name="ln_mod",
    )(r, g.reshape(1, D), b.reshape(1, D), mod, mod)


def layer_norm(r, g, b):
    T, D = r.shape
    tm = _tile(TM_ROWWISE, T)
    row = pl.BlockSpec((tm, D), lambda i: (i, 0))
    vec = pl.BlockSpec((1, D), lambda i: (0, 0))
    return pl.pallas_call(
        _ln_body,
        grid=(T // tm,),
        in_specs=[row, vec, vec],
        out_specs=row,
        out_shape=jax.ShapeDtypeStruct((T, D), F32),
        compiler_params=_params(("parallel",), 32),
        name="layer_norm",
    )(r, g.reshape(1, D), b.reshape(1, D))


def _fused_matmul_body(nx, nw, ne, pairs, epilogue, *refs):
    x_refs = refs[:nx]
    w_refs = refs[nx:nx + nw]
    e_refs = refs[nx + nw:nx + nw + ne]
    o_refs = refs[nx + nw + ne:]
    accs = [jnp.dot(x_refs[a][...], w_refs[b][...].astype(BF16), preferred_element_type=F32) for a, b in pairs]
    epilogue(accs, e_refs, o_refs)


def fused_matmul(xs, ws, pairs, epilogue, extras, extra_specs, out_shapes, out_specs, tm, tn, vmem_mib, name,
                 w_cols=None):
    M = xs[0].shape[0]
    col0, N = w_cols if w_cols is not None else (0, ws[0].shape[1])
    assert col0 % tn == 0 and N % tn == 0 and M % tm == 0
    j0 = col0 // tn
    in_specs = ([pl.BlockSpec((tm, x.shape[1]), lambda i, j: (i, 0)) for x in xs]
                + [pl.BlockSpec((w.shape[0], tn), lambda i, j: (0, j0 + j)) for w in ws]
                + list(extra_specs))
    body = functools.partial(_fused_matmul_body, len(xs), len(ws), len(extras), tuple(pairs), epilogue)
    return pl.pallas_call(
        body,
        grid=(M // tm, N // tn),
        in_specs=in_specs,
        out_specs=out_specs,
        out_shape=out_shapes,
        compiler_params=_params(("parallel", "arbitrary"), vmem_mib),
        name=name,
    )(*xs, *ws, *extras)


def _swiglu_epilogue(accs, e_refs, o_refs):
    g, u = accs
    o_refs[0][...] = (_silu(g) * u).astype(o_refs[0].dtype)


def ffn_up(h, wg, wu):
    T, D = h.shape
    F = wg.shape[1]
    tm = _tile(TM_MATMUL, T)
    return fused_matmul([h], [wg, wu], [(0, 0), (0, 1)], _swiglu_epilogue, [], [],
                        jax.ShapeDtypeStruct((T, F), BF16), pl.BlockSpec((tm, FFN_GU), lambda i, j: (i, j)),
                        tm, FFN_GU, 58, "ffn_up")


def _residual_epilogue(coef, accs, e_refs, o_refs):
    x_ref, gate_ref = e_refs
    o_refs[0][...] = DEEPNORM_ALPHA * x_ref[...] + (coef * gate_ref[...]) * accs[0]


def ffn_down(a, wd, x, mod, seq, slot_gate, coef):
    T, D = x.shape
    tm, tn = _tile(TM_DUAL, seq), _tile(512, D)
    tpb = seq // tm
    return fused_matmul([a], [wd], [(0, 0)], functools.partial(_residual_epilogue, coef), [x, mod],
                        [pl.BlockSpec((tm, tn), lambda i, j: (i, j)),
                         pl.BlockSpec((None, 1, tn), lambda i, j: ((i // tpb) * N_ADA + slot_gate, 0, j))],
                        jax.ShapeDtypeStruct((T, D), F32), pl.BlockSpec((tm, tn), lambda i, j: (i, j)),
                        tm, tn, 58, "ffn_down")


def _rope(t, c, s1, s2):
    half = ROPE_DIM // 2
    return t * c + pltpu.roll(t, LANES - half, axis=1) * s1 + pltpu.roll(t, half, axis=1) * s2


def _q_epilogue(n_heads, accs, e_refs, o_refs):
    c, s1, s2 = (r[...] for r in e_refs)
    qscale = NSA_HEAD_DIM ** -0.5 * LOG2E
    for h in range(n_heads):
        t = accs[0][:, h * LANES:(h + 1) * LANES]
        o_refs[0][h] = (_rope(t, c, s1, s2) * qscale).astype(o_refs[0].dtype)


def _kv_plain_epilogue(n_groups, accs, e_refs, o_refs):
    for g in range(n_groups):
        o_refs[0][g] = accs[0][:, g * LANES:(g + 1) * LANES].astype(o_refs[0].dtype)


def _kv_rope_epilogue(n_groups, accs, e_refs, o_refs):
    c, s1, s2 = (r[...] for r in e_refs)
    for g in range(n_groups):
        t = accs[0][:, g * LANES:(g + 1) * LANES]
        o_refs[0][g] = _rope(t, c, s1, s2).astype(o_refs[0].dtype)


def _plain_epilogue(accs, e_refs, o_refs):
    o_refs[0][...] = accs[0].astype(o_refs[0].dtype)


def _sigmoid_epilogue(accs, e_refs, o_refs):
    o_refs[0][...] = _sigmoid(accs[0]).astype(o_refs[0].dtype)


def _small_epilogue(accs, e_refs, o_refs):
    a = accs[0]
    lane = lax.broadcasted_iota(jnp.int32, a.shape, 1)
    o_refs[0][...] = jnp.where(lane >= LANES, _sigmoid(a), a)


def _compress_body(seg_ref, w1_ref, w2_ref, pos_ref, c_ref, s1_ref, s2_ref, o_ref):
    n_seg = seg_ref.shape[0]
    half_k = seg_ref.shape[1]
    seg = seg_ref[...].astype(BF16)
    top = jnp.dot(seg, w1_ref[0:half_k, :], preferred_element_type=F32)
    bot = jnp.dot(seg, w1_ref[half_k:2 * half_k, :], preferred_element_type=F32)
    cb = jnp.dot(pos_ref[...], w1_ref[...], preferred_element_type=F32)[0:1, :]
    hid = _silu(top + pltpu.roll(bot, n_seg - 1, axis=0) + cb).astype(BF16)
    out = jnp.dot(hid, w2_ref[...], preferred_element_type=F32)
    out = _rope(out, c_ref[...], s1_ref[...], s2_ref[...])
    rowi = lax.broadcasted_iota(jnp.int32, out.shape, 0)
    o_ref[...] = jnp.where(rowi < n_seg - 1, out, 0.0).astype(o_ref.dtype)


def nsa_compress(kvseg, w1, w2, pos8, tabs):
    _, B, G, n_seg, K2 = kvseg.shape
    dh = NSA_HEAD_DIM
    hid = w1.shape[-1]
    tab_spec = pl.BlockSpec((None, None, n_seg, dh), lambda kv, b, g: (kv, b, 0, 0))
    return pl.pallas_call(
        _compress_body,
        grid=(2, B, G),
        in_specs=[pl.BlockSpec((None, None, None, n_seg, K2), lambda kv, b, g: (kv, b, g, 0, 0)),
                  pl.BlockSpec((None, 2 * K2, hid), lambda kv, b, g: (kv, 0, 0)),
                  pl.BlockSpec((None, hid, dh), lambda kv, b, g: (kv, 0, 0)),
                  pl.BlockSpec((SUBLANES, 2 * K2), lambda kv, b, g: (0, 0)),
                  tab_spec, tab_spec, tab_spec],
        out_specs=pl.BlockSpec((None, None, None, n_seg, dh), lambda kv, b, g: (kv, b, g, 0, 0)),
        out_shape=jax.ShapeDtypeStruct((2, B, G, n_seg, dh), BF16),
        compiler_params=_params(("arbitrary", "arbitrary", "arbitrary"), 40),
        name="nsa_compress",
    )(kvseg, w1, w2, pos8, *tabs)


def _attn_body(q_ref, kc_ref, vcxt_ref, kx_ref, vxt_ref, kw_ref, vwxt_ref, vxd_ref, gate_ref, o_ref,
               qxt_ref, s0_ref, s1_ref, mt0_ref, mt1_ref, m_ref, acc_ref, out_ref, *, tq, tk, wk, hpg, top_n):
    dh = NSA_HEAD_DIM
    R = hpg * tq
    hw = R // 2
    n_cp = kc_ref.shape[0]
    t0 = pl.program_id(2) * tq
    t_q = t0 + lax.broadcasted_iota(jnp.int32, (1, tq), 1)
    for h in range(hpg):
        qxt_ref[0:dh, h * tq:(h + 1) * tq] = q_ref[h].astype(F32).T.astype(BF16)
    qt = qxt_ref[0:dh, :]
    gates = gate_ref[...]

    def gate_row(branch):
        return jnp.concatenate([gates[branch * hpg + h:branch * hpg + h + 1, :] for h in range(hpg)], axis=1)

    def all_heads(x):
        return jnp.concatenate([x] * hpg, axis=1)

    c_end = lax.broadcasted_iota(jnp.int32, (n_cp, 1), 0) * NSA_CMP_STRIDE + (NSA_CMP_BLOCK - 1)
    s_c = jnp.dot(kc_ref[...], qt, preferred_element_type=F32) + all_heads(jnp.where(c_end <= t_q, 0.0, NEG_INF))
    mx_c = jnp.max(s_c, axis=0, keepdims=True)
    vcxt = vcxt_ref[...]
    g_c = gate_row(0)
    imp_t = jnp.zeros((LANES, tq), F32)
    for half in range(2):
        cols = slice(half * hw, (half + 1) * hw)
        e = jnp.exp2(s_c[:, cols] - mx_c[:, cols])
        inv = jnp.where(mx_c[:, cols] > 0.5 * NEG_INF, 1.0 / jnp.sum(e, axis=0, keepdims=True), 0.0)
        r = jnp.dot(vcxt, e.astype(BF16), preferred_element_type=F32)
        out_ref[:, cols] = r[0:dh, :] * (g_c[:, cols] * inv)
        ri = r[dh:2 * dh, :] * inv
        for h in range(hpg // 2):
            imp_t = imp_t + ri[:, h * tq:(h + 1) * tq]

    w0 = pl.multiple_of(jnp.maximum(t0 + tq - wk, 0), tq)
    c0 = w0 // tq
    kpos_w = w0 + lax.broadcasted_iota(jnp.int32, (wk, 1), 0)
    w_bias = jnp.where((kpos_w <= t_q) & (kpos_w > t_q - NSA_WINDOW), 0.0, NEG_INF)
    s_w = jnp.dot(kw_ref[pl.ds(w0, wk), :], qt, preferred_element_type=F32) + all_heads(w_bias)
    mx_w = jnp.max(s_w, axis=0, keepdims=True)

    j_i = lax.broadcasted_iota(jnp.int32, (LANES, 1), 0)
    j_f = j_i.astype(F32)
    cur = t_q // NSA_SEL_BLOCK
    causal_blk = j_i * NSA_SEL_BLOCK <= t_q
    forced = (j_i == 0) | (j_i == cur) | (j_i == cur - 1)
    work = jnp.where(causal_blk & jnp.logical_not(forced), imp_t, NEG_INF)
    sel = forced
    for _ in range(top_n - 3):
        mx = jnp.max(work, axis=0, keepdims=True)
        first = jnp.min(jnp.where(work == mx, j_f, float(LANES)), axis=0, keepdims=True)
        pick = j_f == first
        sel = sel | pick
        work = jnp.where(pick, -jnp.inf, work)
    bias_t = jnp.where(sel & (j_i * NSA_SEL_BLOCK < t0), 0.0, NEG_INF).astype(BF16)
    for h in range(hpg):
        qxt_ref[dh:2 * dh, h * tq:(h + 1) * tq] = bias_t

    s_refs = (s0_ref, s1_ref)
    mt_refs = (mt0_ref, mt1_ref)

    def score_tile(kt, slot):
        k0 = pl.multiple_of(kt * tk, tk)
        s = jnp.dot(kx_ref[pl.ds(k0, tk), :], qxt_ref[...], preferred_element_type=F32)
        s_refs[slot][...] = s
        mt_refs[slot][...] = jnp.max(s, axis=0, keepdims=True)

    def softmax_pv(kt, slot):
        vt = vxt_ref[kt]
        for half in range(2):
            cols = slice(half * hw, (half + 1) * hw)
            m_old = m_ref[:, cols]
            m_new = jnp.maximum(m_old, mt_refs[slot][:, cols])
            p = jnp.exp2(s_refs[slot][:, cols] - m_new).astype(BF16)
            pv = jnp.dot(vt, p, preferred_element_type=F32)
            acc_ref[:, cols] = jnp.exp2(m_old - m_new) * acc_ref[:, cols] + pv
            m_ref[:, cols] = m_new

    score_tile(0, 0)

    vwt = jnp.concatenate([vwxt_ref[c0 + i] for i in range(wk // tq)], axis=1)
    g_w = gate_row(2)
    for half in range(2):
        cols = slice(half * hw, (half + 1) * hw)
        e = jnp.exp2(s_w[:, cols] - mx_w[:, cols]).astype(BF16)
        r = jnp.dot(vwt, e, preferred_element_type=F32)
        inv = jnp.where(mx_w[:, cols] > 0.5 * NEG_INF, 1.0 / r[dh:dh + 1, :], 0.0)
        out_ref[:, cols] += r[0:dh, :] * (g_w[:, cols] * inv)

    m_ref[...] = jnp.full(m_ref.shape, NEG_INF, F32)
    acc_ref[...] = jnp.zeros(acc_ref.shape, F32)
    n_full = jnp.maximum((t0 + tk - 1) // tk, 1) - 1

    def two_tiles(i, carry):
        score_tile(2 * i + 1, 1)
        softmax_pv(2 * i, 0)
        score_tile(2 * i + 2, 0)
        softmax_pv(2 * i + 1, 1)
        return carry

    lax.fori_loop(0, n_full // 2, two_tiles, 0)

    @pl.when(n_full % 2 == 0)
    def _():
        softmax_pv(n_full, 0)

    @pl.when(n_full % 2 == 1)
    def _():
        score_tile(n_full, 1)
        softmax_pv(n_full - 1, 0)
        softmax_pv(n_full, 1)

    k_i = lax.broadcasted_iota(jnp.int32, (tq, tq), 0)
    q_i = lax.broadcasted_iota(jnp.int32, (tq, tq), 1)
    s_d = (jnp.dot(kx_ref[pl.ds(pl.multiple_of(t0, tq), tq), 0:dh], qt, preferred_element_type=F32)
           + all_heads(jnp.where(k_i <= q_i, 0.0, NEG_INF)))
    m_old = m_ref[...]
    m_fin = jnp.maximum(m_old, jnp.max(s_d, axis=0, keepdims=True))
    pv_d = jnp.dot(vxd_ref[...], jnp.exp2(s_d - m_fin).astype(BF16), preferred_element_type=F32)
    alpha = jnp.exp2(m_old - m_fin)

    for h in range(hpg):
        cols = slice(h * tq, (h + 1) * tq)
        acc = alpha[:, cols] * acc_ref[:, cols] + pv_d[:, cols]
        l_s = acc[dh:dh + 1, :]
        ok = (m_fin[:, cols] > 0.5 * NEG_INF) & (l_s > 0.0)
        o_t = out_ref[:, cols] + acc[0:dh, :] * (gates[hpg + h:hpg + h + 1, :] * jnp.where(ok, 1.0 / l_s, 0.0))
        o_ref[:, h * dh:(h + 1) * dh] = o_t.T.astype(o_ref.dtype)


def nsa_attention(q, kvc, vcxt, kx, vxt, sw, vwxt, vxd, gates_t, seq):
    B, H, S, dh = q.shape
    G = kx.shape[1]
    hpg = H // G
    n_cp = kvc.shape[3]
    tq = _tile(ATTN_TQ, S)
    tk = _tile(ATTN_TK, S)
    wk = min(NSA_WINDOW + tq, S)
    assert tk % tq == 0 and NSA_WINDOW % tq == 0 and tq % NSA_SEL_BLOCK == 0 and S // NSA_SEL_BLOCK <= LANES
    assert hpg % 2 == 0 and wk % tq == 0 and tq == LANES and tq == 2 * NSA_SEL_BLOCK
    top_n = min(NSA_SEL_TOPN, S // NSA_SEL_BLOCK)
    assert top_n > 3 and hpg < FORCE_BONUS
    R = hpg * tq
    nq = S // tq
    vr = vxt.shape[3]
    body = functools.partial(_attn_body, tq=tq, tk=tk, wk=wk, hpg=hpg, top_n=top_n)
    per_group = lambda *blk: pl.BlockSpec((None, None) + blk, lambda b, g, i: (b, g) + (0,) * len(blk))
    return pl.pallas_call(
        body,
        grid=(B, G, nq),
        in_specs=[pl.BlockSpec((None, hpg, tq, dh), lambda b, g, i: (b, g, i, 0)),
                  pl.BlockSpec((None, None, None, n_cp, dh), lambda b, g, i: (0, b, g, 0, 0)),
                  per_group(2 * dh, n_cp),
                  per_group(S, 2 * dh), per_group(S // tk, vr, tk),
                  pl.BlockSpec((None, None, None, S, dh), lambda b, g, i: (2, b, g, 0, 0)),
                  per_group(S // tq, vr, tq),
                  pl.BlockSpec((None, None, None, vr, tq), lambda b, g, i: (b, g, i, 0, 0)),
                  pl.BlockSpec((None, None, 3 * hpg, tq), lambda b, g, i: (b, g, 0, i))],
        out_specs=pl.BlockSpec((tq, hpg * dh), lambda b, g, i: (b * nq + i, g)),
        out_shape=jax.ShapeDtypeStruct((B * S, H * dh), BF16),
        scratch_shapes=[pltpu.VMEM((2 * dh, R), BF16),
                        pltpu.VMEM((tk, R), F32),
                        pltpu.VMEM((tk, R), F32),
                        pltpu.VMEM((1, R), F32),
                        pltpu.VMEM((1, R), F32),
                        pltpu.VMEM((1, R), F32),
                        pltpu.VMEM((vr, R), F32),
                        pltpu.VMEM((dh, R), F32)],
        compiler_params=_params(("parallel", "parallel", "arbitrary"), 56),
        name="nsa_attention",
    )(q, kvc, vcxt, kx, vxt, sw, vwxt, vxd, gates_t)


def _split3(x):
    hi = x.astype(BF16)
    r = x - hi.astype(F32)
    mid = r.astype(BF16)
    lo = (r - mid.astype(F32)).astype(BF16)
    return hi, mid, lo


def _softplus(x):
    return jnp.maximum(x, 0.0) + jnp.log1p(jnp.exp(-jnp.abs(x)))


def _ssd_body(x_ref, xh_ref, b_ref, bh_ref, c_ref, ch_ref, z_ref, dt_ref, dtt_ref,
              wx_ref, wb_ref, wc_ref, cbx_ref, cbb_ref, cbc_ref,
              dtb_ref, dtbt_ref, alog_ref, alogt_ref, dsk_ref, nw_ref, o_ref,
              state_ref, pad_ref, y_ref, acum_ref, dtg_ref, acumt_ref, dtt_s_ref, *, L, hpg, n_groups):
    P, N = SSD_HEAD_DIM, SSD_STATE
    GW = hpg * P
    HALO = SUBLANES
    c = pl.program_id(1)
    g = pl.program_id(2)
    n_heads = hpg * n_groups

    @pl.when((c == 0) & (g == 0))
    def _():
        state_ref[...] = jnp.zeros(state_ref.shape, F32)

    ri = lax.broadcasted_iota(jnp.int32, (L, L), 0)
    ci = lax.broadcasted_iota(jnp.int32, (L, L), 1)
    tril = ri >= ci

    @pl.when(g == 0)
    def _():
        dt = _softplus(dt_ref[:, 0:n_heads] + dtb_ref[...])
        da = dt * (-jnp.exp(alog_ref[...]))
        dtt = _softplus(dtt_ref[...] + dtbt_ref[...])
        dat = dtt * (-jnp.exp(alogt_ref[...]))
        lower = tril.astype(BF16)
        upper = (ri <= ci).astype(BF16)
        acum = sum(jnp.dot(lower, part, preferred_element_type=F32) for part in _split3(da))
        acumt_ref[...] = sum(jnp.dot(part, upper, preferred_element_type=F32) for part in _split3(dat))
        dtt_s_ref[...] = dtt
        for gg in range(n_groups):
            acum_ref[gg, :, 0:hpg] = acum[:, gg * hpg:(gg + 1) * hpg]
            dtg_ref[gg, :, 0:hpg] = dt[:, gg * hpg:(gg + 1) * hpg]

    first = c == 0
    pad_ref[0:HALO, 0:GW] = jnp.where(first, 0.0, xh_ref[...])
    pad_ref[0:HALO, GW:GW + N] = jnp.where(first, 0.0, bh_ref[...])
    pad_ref[0:HALO, GW + N:GW + 2 * N] = jnp.where(first, 0.0, ch_ref[...])
    pad_ref[HALO:HALO + L, 0:GW] = x_ref[...]
    pad_ref[HALO:HALO + L, GW:GW + N] = b_ref[...]
    pad_ref[HALO:HALO + L, GW + N:GW + 2 * N] = c_ref[...]

    def conv(lo, hi, w_ref, bias_ref):
        out = bias_ref[...]
        for k in range(SSD_CONV):
            off = HALO - (SSD_CONV - 1) + k
            out = out + w_ref[k:k + 1, :] * pad_ref[off:off + L, lo:hi]
        return _silu(out)

    xa = conv(0, GW, wx_ref, cbx_ref)
    bg = conv(GW, GW + N, wb_ref, cbb_ref).astype(BF16)
    cg = conv(GW + N, GW + 2 * N, wc_ref, cbc_ref).astype(BF16)
    xat = xa.T
    cb = _nt_dot(cg, bg)
    head_lane = lax.broadcasted_iota(jnp.int32, (1, LANES), 1) < hpg
    acg = jnp.where(head_lane, acum_ref[g], 0.0)
    dtg = jnp.where(head_lane, dtg_ref[g], 0.0)
    h0 = pl.multiple_of(g * hpg, hpg)
    act = acumt_ref[pl.ds(h0, hpg), :]
    dtt = dtt_s_ref[pl.ds(h0, hpg), :]
    dsk = dsk_ref[...]
    expand = (lax.broadcasted_iota(jnp.int32, (LANES, GW), 1) // P
              == lax.broadcasted_iota(jnp.int32, (LANES, GW), 0)).astype(BF16)
    dtx = sum(jnp.dot(part, expand, preferred_element_type=F32) for part in _split3(dtg))
    ecx = sum(jnp.dot(part, expand, preferred_element_type=F32) for part in _split3(jnp.exp(acg)))
    xdt_all = xa * dtx
    first_lane = lax.broadcasted_iota(jnp.int32, (1, 2 * P), 1) < P
    first_row = lax.broadcasted_iota(jnp.int32, (2 * P, 1), 0) < P
    for hp in range(hpg // 2):
        cols = slice(hp * 2 * P, (hp + 1) * 2 * P)
        xdt = xdt_all[:, cols]
        y = jnp.zeros((L, 2 * P), F32)
        w_rows = []
        for k in range(2):
            h = 2 * hp + k
            col = acg[:, h:h + 1]
            row = act[h:h + 1, :]
            decay = jnp.where(tril, jnp.exp(col - row), 0.0)
            own = first_lane if k == 0 else jnp.logical_not(first_lane)
            y = y + jnp.dot((cb * decay).astype(BF16), jnp.where(own, xdt, 0.0).astype(BF16),
                            preferred_element_type=F32)
            a_last = act[h:h + 1, L - 1:L]
            w_rows.append((dtt[h:h + 1, :] * jnp.exp(a_last - row), jnp.exp(a_last)))
        s0 = pl.multiple_of((g * hpg + 2 * hp) * P, 2 * P)
        st = state_ref[pl.ds(s0, 2 * P), :]
        y = y + _nt_dot(cg, st.astype(BF16)) * ecx[:, cols]
        xw_t = (xat[hp * 2 * P:(hp + 1) * 2 * P, :] * jnp.where(first_row, w_rows[0][0], w_rows[1][0])).astype(BF16)
        state_ref[pl.ds(s0, 2 * P), :] = (st * jnp.where(first_row, w_rows[0][1], w_rows[1][1])
                                          + jnp.dot(xw_t, bg, preferred_element_type=F32))
        y_ref[:, cols] = y + dsk[:, cols] * xa[:, cols]

    yg = y_ref[...] * _silu(z_ref[...])
    ms = jnp.mean(yg * yg, axis=-1, keepdims=True)
    o_ref[...] = (yg * lax.rsqrt(ms + RMS_EPS) * nw_ref[...]).astype(o_ref.dtype)


def ssd_mixer(z, xbc, small, dt_t, conv_w, conv_b, dt_bias, a_log, d_skip, norm_w, batch, seq):
    T, Di = z.shape
    Gs, N, P, Hs = SSD_GROUPS, SSD_STATE, SSD_HEAD_DIM, SSD_HEADS
    hpg = Hs // Gs
    GW = hpg * P
    L = math.gcd(seq, SSD_CHUNK)
    nc = seq // L
    HALO = SUBLANES
    assert GW % N == 0 and Hs <= LANES and hpg % 2 == 0 and 2 * P == LANES
    b_off = Di // N
    c_off = b_off + Gs

    def rows(b, c, g):
        return b * nc + c

    def halo_rows(b, c, g):
        return jnp.maximum((b * seq + c * L) // HALO - 1, 0)

    in_specs = [
        pl.BlockSpec((L, GW), lambda b, c, g: (rows(b, c, g), g)),
        pl.BlockSpec((HALO, GW), lambda b, c, g: (halo_rows(b, c, g), g)),
        pl.BlockSpec((L, N), lambda b, c, g: (rows(b, c, g), b_off + g)),
        pl.BlockSpec((HALO, N), lambda b, c, g: (halo_rows(b, c, g), b_off + g)),
        pl.BlockSpec((L, N), lambda b, c, g: (rows(b, c, g), c_off + g)),
        pl.BlockSpec((HALO, N), lambda b, c, g: (halo_rows(b, c, g), c_off + g)),
        pl.BlockSpec((L, GW), lambda b, c, g: (rows(b, c, g), g)),
        pl.BlockSpec((L, LANES), lambda b, c, g: (rows(b, c, g), 0)),
        pl.BlockSpec((None, Hs, L), lambda b, c, g: (b, 0, c)),
        pl.BlockSpec((SSD_CONV, GW), lambda b, c, g: (0, g)),
        pl.BlockSpec((SSD_CONV, N), lambda b, c, g: (0, b_off + g)),
        pl.BlockSpec((SSD_CONV, N), lambda b, c, g: (0, c_off + g)),
        pl.BlockSpec((1, GW), lambda b, c, g: (0, g)),
        pl.BlockSpec((1, N), lambda b, c, g: (0, b_off + g)),
        pl.BlockSpec((1, N), lambda b, c, g: (0, c_off + g)),
        pl.BlockSpec((1, Hs), lambda b, c, g: (0, 0)),
        pl.BlockSpec((Hs, 1), lambda b, c, g: (0, 0)),
        pl.BlockSpec((1, Hs), lambda b, c, g: (0, 0)),
        pl.BlockSpec((Hs, 1), lambda b, c, g: (0, 0)),
        pl.BlockSpec((1, GW), lambda b, c, g: (0, g)),
        pl.BlockSpec((1, GW), lambda b, c, g: (0, g)),
    ]
    cb2 = conv_b.reshape(1, -1)
    body = functools.partial(_ssd_body, L=L, hpg=hpg, n_groups=Gs)
    return pl.pallas_call(
        body,
        grid=(batch, nc, Gs),
        in_specs=in_specs,
        out_specs=pl.BlockSpec((L, GW), lambda b, c, g: (rows(b, c, g), g)),
        out_shape=jax.ShapeDtypeStruct((T, Di), BF16),
        scratch_shapes=[pltpu.VMEM((Hs * P, N), F32),
                        pltpu.VMEM((HALO + L, GW + 2 * N), F32),
                        pltpu.VMEM((L, GW), F32),
                        pltpu.VMEM((Gs, L, LANES), F32),
                        pltpu.VMEM((Gs, L, LANES), F32),
                        pltpu.VMEM((Hs, L), F32),
                        pltpu.VMEM((Hs, L), F32)],
        compiler_params=_params(("arbitrary", "arbitrary", "arbitrary"), 40),
        name="ssd",
    )(xbc, xbc, xbc, xbc, xbc, xbc, z, small, dt_t,
      conv_w, conv_w, conv_w, cb2, cb2, cb2,
      dt_bias.reshape(1, Hs), dt_bias.reshape(Hs, 1), a_log.reshape(1, Hs), a_log.reshape(Hs, 1),
      jnp.repeat(d_skip, P).reshape(1, Di), norm_w.reshape(1, Di))


def _merge_epilogue(accs, e_refs, o_refs):
    ga, gb = e_refs
    o_refs[0][...] = (ga[...].astype(F32) * accs[0] + gb[...].astype(F32) * accs[1]).astype(o_refs[0].dtype)


def _rope_lane_tables(pos):
    half = ROPE_DIM // 2
    inv_freq = jnp.float32(ROPE_THETA) ** (-jnp.arange(half, dtype=F32) / half)
    ang = pos.astype(F32)[..., None] * inv_freq
    cos, sin = jnp.cos(ang), jnp.sin(ang)
    pad = [(0, 0)] * (cos.ndim - 1)
    one = jnp.ones(cos.shape[:-1] + (LANES - ROPE_DIM,), F32)
    c = jnp.concatenate([cos, cos, one], axis=-1)
    s1 = jnp.pad(-sin, pad + [(0, LANES - half)])
    s2 = jnp.pad(sin, pad + [(half, LANES - ROPE_DIM)])
    return c, s1, s2


def _pad_cols(w, n):
    return jnp.pad(w, ((0, 0), (0, n - w.shape[1])))


def _layer(x, mod, positions, tabs, tabs_c, consts, p):
    B, S = positions.shape
    T, D = x.shape
    H, G, dh = NSA_HEADS, NSA_KV_GROUPS, NSA_HEAD_DIM
    hpg = H // G
    tm = _tile(TM_MATMUL, S)
    tpb = S // tm

    def ffn(h, x_res, wg, wu, wd, slot_gate):
        a = ffn_up(h, wg, wu)
        return ffn_down(a, wd.astype(BF16), x_res, mod, S, slot_gate, 0.5)

    h = modulate(x, mod, S, 0, 1)
    r = ffn(h, x, p["ffn1_w_gate"], p["ffn1_w_up"], p["ffn1_w_down"], 2)
    x, h = ln_mod(r, p["ln1_g"], p["ln1_b"], mod, S, 3, 4)

    w_in = p["w_in"]
    qkv_w = NSA_Q_WIDTH + 6 * NSA_KV_WIDTH
    w_head = w_in
    c_gn, c_z, c_xbc, c_dt, c_ga, c_gb = (qkv_w + int(v) for v in np.cumsum((0,) + IN_SPLIT_SIZES[7:])[:-1])
    w_zx = w_in[:, c_z:c_ga].astype(BF16)
    w_gab = w_in[:, c_ga:].astype(BF16)
    w_small = jnp.concatenate([_pad_cols(w_in[:, c_dt:c_ga], LANES), _pad_cols(w_in[:, c_gn:c_z], LANES)],
                              axis=1).astype(BF16)
    tab_specs = [pl.BlockSpec((tm, LANES), lambda i, j: (i, 0))] * 3

    hq = max(hpg // 2, 1)
    tnq = hq * dh
    q = fused_matmul([h], [w_head], [(0, 0)], functools.partial(_q_epilogue, hq), tabs, tab_specs,
                     jax.ShapeDtypeStruct((B, H, S, dh), BF16),
                     pl.BlockSpec((None, hq, tm, dh), lambda i, j: (i // tpb, j, i % tpb, 0)),
                     tm, tnq, 52, "in_proj_q", w_cols=(0, NSA_Q_WIDTH))
    tnk = G * dh
    kv_spec = pl.BlockSpec((None, None, G, tm, dh), lambda i, j: (j, i // tpb, 0, i % tpb, 0))
    kvcmp = fused_matmul([h], [w_head], [(0, 0)], functools.partial(_kv_plain_epilogue, G), [], [],
                         jax.ShapeDtypeStruct((2, B, G, S, dh), F32), kv_spec, tm, tnk, 52, "in_proj_kvcmp",
                         w_cols=(NSA_Q_WIDTH, 2 * tnk))
    ident = (jnp.ones_like(tabs[0]), jnp.zeros_like(tabs[0]), jnp.zeros_like(tabs[0]))
    tabs_kv = [jnp.stack([t, e]) for t, e in zip(tabs, ident)]
    tab_kv_specs = [pl.BlockSpec((None, tm, LANES), lambda i, j: (j % 2, i, 0))] * 3
    sw = fused_matmul([h], [w_head], [(0, 0)], functools.partial(_kv_rope_epilogue, G), tabs_kv, tab_kv_specs,
                      jax.ShapeDtypeStruct((4, B, G, S, dh), BF16), kv_spec, tm, tnk, 52, "in_proj_slcwin",
                      w_cols=(NSA_Q_WIDTH + 2 * tnk, 4 * tnk))
    plain_spec = lambda tn: pl.BlockSpec((tm, tn), lambda i, j: (i, j))
    tnt = math.gcd(1024, SSD_D_INNER, SSD_CONV_CH, 2 * D)
    z = fused_matmul([h], [w_zx], [(0, 0)], _plain_epilogue, [], [],
                     jax.ShapeDtypeStruct((T, SSD_D_INNER), F32), plain_spec(tnt), tm, tnt, 52, "in_proj_z",
                     w_cols=(0, SSD_D_INNER))
    xbc = fused_matmul([h], [w_zx], [(0, 0)], _plain_epilogue, [], [],
                       jax.ShapeDtypeStruct((T, SSD_CONV_CH), F32), plain_spec(tnt), tm, tnt, 52, "in_proj_xbc",
                       w_cols=(SSD_D_INNER, SSD_CONV_CH))
    gab = fused_matmul([h], [w_gab], [(0, 0)], _sigmoid_epilogue, [], [],
                       jax.ShapeDtypeStruct((T, 2 * D), BF16), plain_spec(tnt), tm, tnt, 52, "in_proj_gates")
    small = fused_matmul([h], [w_small], [(0, 0)], _small_epilogue, [], [],
                         jax.ShapeDtypeStruct((T, 2 * LANES), F32), plain_spec(2 * LANES), tm, 2 * LANES, 52,
                         "in_proj_small")

    n_seg = S // NSA_CMP_STRIDE
    kvseg = kvcmp.reshape(2, B, G, n_seg, NSA_CMP_STRIDE * dh)
    w1 = jnp.stack([p["nsa_cmp_k_w1"], p["nsa_cmp_v_w1"]]).astype(BF16)
    w2 = jnp.stack([p["nsa_cmp_k_w2"], p["nsa_cmp_v_w2"]]).astype(BF16)
    pos8 = jnp.broadcast_to(p["nsa_cmp_pos"].reshape(1, -1), (SUBLANES, NSA_CMP_BLOCK * dh)).astype(BF16)
    kvc = nsa_compress(kvseg, w1, w2, pos8, tabs_c)
    e_blk, ones_col, overlap = consts
    ones4 = jnp.broadcast_to(ones_col, (B, G, S, ones_col.shape[-1]))
    kx = jnp.concatenate([sw[0], jnp.broadcast_to(e_blk, (B, G, S, dh))], axis=-1)

    def chunked_t(v, chunk):
        return jnp.swapaxes(v.reshape(B, G, S // chunk, chunk, v.shape[-1]), -1, -2)

    vx = jnp.concatenate([sw[1], ones4], axis=-1)
    vxt = chunked_t(vx, _tile(ATTN_TK, S))
    vxd = chunked_t(vx, _tile(ATTN_TQ, S))
    vwxt = chunked_t(jnp.concatenate([sw[3], ones4], axis=-1), _tile(ATTN_TQ, S))
    vcxt = jnp.swapaxes(jnp.concatenate([kvc[1], jnp.broadcast_to(overlap, (B, G, n_seg, LANES))], axis=-1), -1, -2)
    g3 = small[:, LANES:LANES + 3 * H].reshape(B, S, 3, G, hpg)
    gates_t = jnp.transpose(g3, (0, 3, 2, 4, 1)).reshape(B, G, 3 * hpg, S)
    o_a = nsa_attention(q, kvc, vcxt, kx, vxt, sw, vwxt, vxd, gates_t, S)

    dt_t = jnp.transpose(small[:, :SSD_HEADS].reshape(B, S, SSD_HEADS), (0, 2, 1))
    o_b = ssd_mixer(z, xbc, small, dt_t, p["ssd_conv_w"], p["ssd_conv_b"], p["ssd_dt_bias"], p["ssd_a_log"],
                    p["ssd_d"], p["ssd_norm_w"], B, S)

    tm2 = _tile(TM_MATMUL, S)
    tn2 = _tile(256, D)
    nga = D // tn2
    merged = fused_matmul([o_a, o_b], [p["w_branch_a"].astype(BF16), p["w_branch_b"].astype(BF16)],
                          [(0, 0), (1, 1)], _merge_epilogue, [gab, gab],
                          [pl.BlockSpec((tm2, tn2), lambda i, j: (i, j)),
                           pl.BlockSpec((tm2, tn2), lambda i, j: (i, nga + j))],
                          jax.ShapeDtypeStruct((T, D), BF16), pl.BlockSpec((tm2, tn2), lambda i, j: (i, j)),
                          tm2, tn2, 52, "branch_merge")
    tno = _tile(512, D)
    r = fused_matmul([merged], [p["w_out"].astype(BF16)], [(0, 0)], functools.partial(_residual_epilogue, 1.0),
                     [x, mod],
                     [pl.BlockSpec((tm, tno), lambda i, j: (i, j)),
                      pl.BlockSpec((None, 1, tno), lambda i, j: ((i // tpb) * N_ADA + 5, 0, j))],
                     jax.ShapeDtypeStruct((T, D), F32), pl.BlockSpec((tm, tno), lambda i, j: (i, j)),
                     tm, tno, 52, "out_proj")
    x, h = ln_mod(r, p["ln2_g"], p["ln2_b"], mod, S, 6, 7)

    r = ffn(h, x, p["ffn2_w_gate"], p["ffn2_w_up"], p["ffn2_w_down"], 8)
    return layer_norm(r, p["ln3_g"], p["ln3_b"])


def kernel(x, c, positions, w_ada, b_ada, ffn1_w_gate, ffn1_w_up, ffn1_w_down, w_in, nsa_cmp_pos, nsa_cmp_k_w1, nsa_cmp_k_w2, nsa_cmp_v_w1, nsa_cmp_v_w2, ssd_conv_w, ssd_conv_b, ssd_dt_bias, ssd_a_log, ssd_d, ssd_norm_w, w_branch_a, w_branch_b, w_out, ffn2_w_gate, ffn2_w_up, ffn2_w_down, ln1_g, ln1_b, ln2_g, ln2_b, ln3_g, ln3_b):
    B, S, D = x.shape
    dh = NSA_HEAD_DIM
    per_layer = dict(ffn1_w_gate=ffn1_w_gate, ffn1_w_up=ffn1_w_up, ffn1_w_down=ffn1_w_down, w_in=w_in,
                     nsa_cmp_pos=nsa_cmp_pos, nsa_cmp_k_w1=nsa_cmp_k_w1, nsa_cmp_k_w2=nsa_cmp_k_w2,
                     nsa_cmp_v_w1=nsa_cmp_v_w1, nsa_cmp_v_w2=nsa_cmp_v_w2, ssd_conv_w=ssd_conv_w,
                     ssd_conv_b=ssd_conv_b, ssd_dt_bias=ssd_dt_bias, ssd_a_log=ssd_a_log, ssd_d=ssd_d,
                     ssd_norm_w=ssd_norm_w, w_branch_a=w_branch_a, w_branch_b=w_branch_b, w_out=w_out,
                     ffn2_w_gate=ffn2_w_gate, ffn2_w_up=ffn2_w_up, ffn2_w_down=ffn2_w_down,
                     ln1_g=ln1_g, ln1_b=ln1_b, ln2_g=ln2_g, ln2_b=ln2_b, ln3_g=ln3_g, ln3_b=ln3_b)

    tabs = [t.reshape(B * S, LANES) for t in _rope_lane_tables(positions)]
    n_seg = S // NSA_CMP_STRIDE
    c_end = jnp.minimum(NSA_CMP_STRIDE * jnp.arange(n_seg) + NSA_CMP_BLOCK - 1, S - 1)
    ck, s1k, s2k = _rope_lane_tables(positions[:, c_end])
    tabs_c = [jnp.stack([ck, jnp.ones_like(ck)]), jnp.stack([s1k, jnp.zeros_like(s1k)]),
              jnp.stack([s2k, jnp.zeros_like(s2k)])]

    key_blk = np.arange(S) // NSA_SEL_BLOCK
    e_blk = jnp.asarray(key_blk[:, None] == np.arange(LANES)[None, :], BF16)
    ones_col = jnp.asarray(np.arange(2 * SUBLANES)[None, :] == 0, BF16) * jnp.ones((S, 1), BF16)
    n_cmp = (S - NSA_CMP_BLOCK) // NSA_CMP_STRIDE + 1
    c_start = NSA_CMP_STRIDE * np.arange(n_seg)
    sel_start = NSA_SEL_BLOCK * np.arange(LANES)
    ov = ((c_start[:, None] < sel_start[None, :] + NSA_SEL_BLOCK)
          & (c_start[:, None] + NSA_CMP_BLOCK - 1 >= sel_start[None, :])
          & (np.arange(n_seg)[:, None] < n_cmp) & (np.arange(LANES)[None, :] < S // NSA_SEL_BLOCK))
    consts = (e_blk, ones_col, jnp.asarray(ov, BF16))

    xt = x.reshape(B * S, D)
    for l in range(DEPTH):
        mod = ada_proj(c, w_ada[l], b_ada[l]).reshape(B * N_ADA, 1, D)
        xt = _layer(xt, mod, positions, tabs, tabs_c, consts, {k: v[l] for k, v in per_layer.items()})
    return xt.reshape(B, S, D)
```
